```python
import math
import jax, jax.numpy as jnp
from jax import lax
import numpy as np

D_MODEL = 1024
BATCH = 8
SEQ = 2048
DEPTH = 1

CTX_LEN = 256
GRID_W = 64
H_A = 4
DK_A = 128
DV_A = 128
C_A = H_A * DV_A
SHORT_CONV = 5
CHUNK = 64
H_B = 8
N_B = 64
C_B = H_B * N_B
LORA_W = 64
LORA_A = 64
LORA_G = 128
A_COLS = 4 * C_A + 4 * H_A
B_COLS = 3 * C_B + 2 * LORA_W + 2 * LORA_A + LORA_G
IN_COLS = A_COLS + B_COLS + 2 * D_MODEL
N_GROUPS = 4
EXPERTS_PER_GROUP = 8
N_EXPERTS = N_GROUPS * EXPERTS_PER_GROUP
TOP_K = 2
D_EXPERT = 256
NORM_EPS = 1e-6
LNX_EPS = 1e-5 * N_B

kernel_name = 'hybrid_gdn_rwkv7_hmoe_dit_block'


def rmsnorm(x, g):
    xf = x.astype(jnp.float32)
    y = xf * lax.rsqrt(jnp.mean(xf * xf, axis=-1, keepdims=True) + NORM_EPS)
    return y.astype(x.dtype) * g


def l2norm(x):
    xf = x.astype(jnp.float32)
    return (xf * lax.rsqrt(jnp.sum(xf * xf, axis=-1, keepdims=True) + NORM_EPS)).astype(x.dtype)


def modulate(x, shift, scale):
    return x * (1 + scale) + shift


def both_dirs(t):
    return jnp.stack([t, jnp.flip(t, 1)], 0)


def per_dir(t):
    return jnp.stack([t[:, :, 0], jnp.flip(t[:, :, 1], 1)], 0)


def merge_dirs(y):
    return y[0] + jnp.flip(y[1], 1)


def to_col_major(x, rows):
    b, l, c = x.shape
    return x.reshape(b, rows, GRID_W, c).transpose(0, 2, 1, 3).reshape(b, l, c)


def to_row_major(x, rows):
    b, l, c = x.shape
    return x.reshape(b, GRID_W, rows, c).transpose(0, 2, 1, 3).reshape(b, l, c)


def depthwise_conv(x, w):
    pad = SHORT_CONV // 2
    return lax.conv_general_dilated(x, w[:, None, :], window_strides=(1,), padding=[(pad, pad)],
                                    dimension_numbers=('NWC', 'WIO', 'NWC'),
                                    feature_group_count=x.shape[-1])


def bidir_shift(x):
    xp = jnp.pad(x, ((0, 0), (1, 1), (0, 0)))
    return 0.5 * (xp[:, :-2] + xp[:, 2:])


def gated_delta_chunked(q, k, v, g, beta, s0):
    dtype = v.dtype
    q, k, v, g, beta = (t.astype(jnp.float32) for t in (q, k, v, g, beta))
    n = q.shape[-2] // CHUNK
    dv = v.shape[-1]

    def blk(t):
        return t.reshape(t.shape[:-2] + (n, CHUNK, t.shape[-1]))

    q, k, v = blk(q), blk(k), blk(v)
    g = g.reshape(g.shape[:-1] + (n, CHUNK))
    beta = beta.reshape(beta.shape[:-1] + (n, CHUNK))
    gc = jnp.cumsum(g, axis=-1)
    incl = jnp.tril(jnp.ones((CHUNK, CHUNK), bool))
    strict = jnp.tril(jnp.ones((CHUNK, CHUNK), bool), -1)
    decay = jnp.where(incl, jnp.exp(jnp.where(incl, gc[..., :, None] - gc[..., None, :], 0.0)), 0.0)
    k_beta = k * beta[..., None]
    lower = jnp.where(strict, jnp.einsum('...id,...jd->...ij', k_beta, k) * decay, 0.0)
    a_mat = jnp.eye(CHUNK, dtype=jnp.float32) + lower
    rhs = jnp.concatenate([v * beta[..., None], k_beta * jnp.exp(gc)[..., None]], axis=-1)
    sol = lax.linalg.triangular_solve(a_mat, rhs, left_side=True, lower=True, unit_diagonal=True)
    u, w = sol[..., :dv], sol[..., dv:]
    qk = jnp.where(incl, jnp.einsum('...id,...jd->...ij', q, k) * decay, 0.0)
    q_dec = q * jnp.exp(gc)[..., None]
    k_dec = k * jnp.exp(gc[..., -1:] - gc)[..., None]
    g_end = jnp.exp(gc[..., -1])
    xs = (jnp.moveaxis(u, -3, 0), jnp.moveaxis(w, -3, 0), jnp.moveaxis(qk, -3, 0),
          jnp.moveaxis(q_dec, -3, 0), jnp.moveaxis(k_dec, -3, 0), jnp.moveaxis(g_end, -1, 0))

    def step(s, inp):
        u_c, w_c, qk_c, qd_c, kd_c, ge_c = inp
        v_new = u_c - jnp.einsum('...ck,...kv->...cv', w_c, s)
        o_c = jnp.einsum('...ck,...kv->...cv', qd_c, s) + jnp.einsum('...ij,...jv->...iv', qk_c, v_new)
        s = s * ge_c[..., None, None] + jnp.einsum('...ck,...cv->...kv', kd_c, v_new)
        return s, o_c

    s_fin, o = lax.scan(step, s0.astype(jnp.float32), xs)
    o = jnp.moveaxis(o, 0, -3)
    return o.reshape(o.shape[:-3] + (n * CHUNK, dv)).astype(dtype), s_fin


def gdn_branch(cols, s0, conv_w, a_log, dt_bias, onorm_g):
    bsz, length, _ = cols.shape
    qkv, z, a, bt = jnp.split(cols, [3 * C_A, 4 * C_A, 4 * C_A + 2 * H_A], axis=-1)
    qkv = jax.nn.silu(depthwise_conv(qkv, conv_w))
    q, k, v = (t.reshape(bsz, length, H_A, DK_A) for t in jnp.split(qkv, 3, axis=-1))
    q = l2norm(q) * DK_A ** -0.5
    k = l2norm(k)
    g = -jnp.exp(a_log) * jax.nn.softplus(a.reshape(bsz, length, 2, H_A) + dt_bias)
    beta = jax.nn.sigmoid(bt.reshape(bsz, length, 2, H_A))
    sw = lambda t: jnp.swapaxes(t, 2, 3)
    o, s_fin = gated_delta_chunked(sw(both_dirs(q)), sw(both_dirs(k)), sw(both_dirs(v)),
                                   sw(per_dir(g)), sw(per_dir(beta)), s0)
    o = merge_dirs(sw(o))
    o = rmsnorm(o, onorm_g) * jax.nn.silu(z.reshape(bsz, length, H_A, DV_A))
    return o.reshape(bsz, length, C_A), s_fin


def rwkv7_scan(r, w, k, v, a, b, s0):
    dtype = r.dtype
    xs = tuple(jnp.moveaxis(t.astype(jnp.float32), 2, 0) for t in (r, w, k, v, a, b))

    def step(s, inp):
        r_t, w_t, k_t, v_t, a_t, b_t = inp
        sa = jnp.einsum('...vk,...k->...v', s, a_t)
        s = s * w_t[..., None, :] + sa[..., :, None] * b_t[..., None, :] + v_t[..., :, None] * k_t[..., None, :]
        return s, jnp.einsum('...vk,...k->...v', s, r_t)

    s_fin, y = lax.scan(step, s0.astype(jnp.float32), xs)
    return jnp.moveaxis(y, 0, 2).astype(dtype), s_fin


def rwkv_branch(cols, s0, mu, w0, w2, a0, a2, g2, k_k, k_a, r_k, lnx_g, lnx_b):
    bsz, length, _ = cols.shape
    cols = cols + (bidir_shift(cols) - cols) * mu
    r, k, v, w_lo, a_lo, g_lo = jnp.split(
        cols, [C_B, 2 * C_B, 3 * C_B, 3 * C_B + 2 * LORA_W, 3 * C_B + 2 * LORA_W + 2 * LORA_A], axis=-1)
    w_lo = w_lo.reshape(bsz, length, 2, LORA_W)
    a_lo = a_lo.reshape(bsz, length, 2, LORA_A)
    w_log = -jax.nn.softplus(-(w0 + jnp.einsum('bldr,drc->bldc', jnp.tanh(w_lo), w2))) - 0.5
    decay = jnp.exp(-jnp.exp(w_log.astype(jnp.float32))).astype(cols.dtype)
    iclr = jax.nn.sigmoid(a0 + jnp.einsum('bldr,drc->bldc', a_lo, a2))
    gate = jax.nn.sigmoid(g_lo) @ g2
    heads = lambda t: t.reshape(t.shape[:-1] + (H_B, N_B))
    kk = l2norm(heads(k * k_k))
    k_dir = heads(k[:, :, None, :] * (1 + (iclr - 1) * k_a))
    b_dir = kk[:, :, None] * heads(iclr)
    r_h, v_h = heads(r), heads(v)
    y, s_fin = rwkv7_scan(both_dirs(r_h), per_dir(heads(decay)), per_dir(k_dir), both_dirs(v_h),
                          both_dirs(-kk), per_dir(b_dir), s0)
    yf = merge_dirs(y).astype(jnp.float32)
    mean = jnp.mean(yf, axis=-1, keepdims=True)
    var = jnp.mean(jnp.square(yf - mean), axis=-1, keepdims=True)
    y = ((yf - mean) * lax.rsqrt(var + LNX_EPS)).astype(cols.dtype).reshape(bsz, length, C_B) * lnx_g + lnx_b
    bonus = jnp.einsum('blhn,bldhn,hn->blh', r_h, k_dir, r_k)[..., None] * v_h
    y = (y + bonus.reshape(bsz, length, C_B)) * gate
    return y, s_fin


def merge_branches(gates, o_a, o_b, w_o_a, w_o_b, w_out):
    g_a, g_b = jnp.split(gates, 2, axis=-1)
    y = jax.nn.sigmoid(g_a) * (o_a @ w_o_a) + jax.nn.sigmoid(g_b) * (o_b @ w_o_b)
    return y @ w_out


def mixing_sublayer(u_lat, u_ctx, rows, with_ctx_out, w_in, gdn_conv, gdn_a_log, gdn_dt_bias, gdn_onorm_g,
                    rwkv_mu, rwkv_w0, rwkv_w2, rwkv_a0, rwkv_a2, rwkv_g2, rwkv_k_k, rwkv_k_a, rwkv_r_k,
                    rwkv_lnx_g, rwkv_lnx_b, w_o_a, w_o_b, w_out):
    bsz = u_lat.shape[0]
    split_at = [A_COLS, A_COLS + B_COLS]
    a_lat, b_lat, gates_lat = jnp.split(u_lat @ w_in, split_at, axis=-1)
    a_ctx, b_ctx, gates_ctx = jnp.split(u_ctx @ w_in, split_at, axis=-1)

    def run_a(cols, s0):
        return gdn_branch(cols, s0, gdn_conv, gdn_a_log, gdn_dt_bias, gdn_onorm_g)

    def run_b(cols, s0):
        return rwkv_branch(cols, s0, rwkv_mu, rwkv_w0, rwkv_w2, rwkv_a0, rwkv_a2, rwkv_g2,
                           rwkv_k_k, rwkv_k_a, rwkv_r_k, rwkv_lnx_g, rwkv_lnx_b)

    zeros_a = jnp.zeros((2, bsz, H_A, DK_A, DV_A), jnp.float32)
    zeros_b = jnp.zeros((2, bsz, H_B, N_B, N_B), jnp.float32)
    oa_ctx, state_a = run_a(a_ctx, zeros_a)
    ob_ctx, state_b = run_b(b_ctx, zeros_b)
    oa_lat, _ = run_a(a_lat, state_a)
    ob_lat, _ = run_b(to_col_major(b_lat, rows), state_b)
    ob_lat = to_row_major(ob_lat, rows)
    y_lat = merge_branches(gates_lat, oa_lat, ob_lat, w_o_a, w_o_b, w_out)
    if not with_ctx_out:
        return y_lat, None
    return y_lat, merge_branches(gates_ctx, oa_ctx, ob_ctx, w_o_a, w_o_b, w_out)


def hier_moe(h, router_grp, router_grp_b, router_exp, router_exp_b, w_gate, w_up, w_down):
    shape = h.shape
    t = h.reshape(-1, shape[-1])
    grp_p = jax.nn.softmax((t @ router_grp + router_grp_b).astype(jnp.float32), axis=-1)
    p_grp, g_sel = lax.top_k(grp_p, 1)
    exp_logits = (t @ router_exp + router_exp_b).astype(jnp.float32).reshape(-1, N_GROUPS, EXPERTS_PER_GROUP)
    in_grp = jnp.take_along_axis(exp_logits, g_sel[:, :, None], axis=1)[:, 0]
    top_logit, top_idx = lax.top_k(in_grp, TOP_K)
    w_top = jax.nn.softmax(top_logit, axis=-1) * p_grp
    w_exp = jnp.einsum('tk,tke->te', w_top, jax.nn.one_hot(top_idx, EXPERTS_PER_GROUP, dtype=jnp.float32))
    combine = (jax.nn.one_hot(g_sel[:, 0], N_GROUPS, dtype=jnp.float32)[:, :, None]
               * w_exp[:, None, :]).astype(h.dtype)
    out = jnp.zeros_like(t)
    for grp in range(N_GROUPS):
        hid = jax.nn.silu(jnp.einsum('td,edf->tef', t, w_gate[grp])) * jnp.einsum('td,edf->tef', t, w_up[grp])
        out = out + jnp.einsum('tef,efd->td', hid * combine[:, grp, :, None], w_down[grp])
    return out.reshape(shape)


def setup_inputs(seed: int = 0) -> dict:
    key = jax.random.key(seed)
    ks = jax.random.split(key, 36)
    d = D_MODEL
    nl = DEPTH

    def nrm(k, shape, scale):
        return jax.random.normal(k, shape, jnp.float32) * scale

    dt = jnp.exp(jax.random.uniform(ks[11], (nl, 2, H_A), minval=math.log(1e-3), maxval=math.log(1e-1)))
    return {
        'x': nrm(ks[0], (BATCH, SEQ, d), 1.0),
        'c': nrm(ks[1], (BATCH, d), 1.0),
        'ctx': nrm(ks[2], (BATCH, CTX_LEN, d), 1.0),
        'c_ctx': nrm(ks[3], (d,), 1.0),
        'ada_w': nrm(ks[4], (nl, d, 6 * d), 0.5 * d ** -0.5),
        'ada_b': nrm(ks[5], (nl, 6 * d), 0.01),
        'norm_mix_g': 1.0 + nrm(ks[6], (nl, d), 0.05),
        'norm_ffn_g': 1.0 + nrm(ks[7], (nl, d), 0.05),
        'w_in': nrm(ks[8], (nl, d, IN_COLS), d ** -0.5),
        'gdn_conv': nrm(ks[9], (nl, SHORT_CONV, 3 * C_A), SHORT_CONV ** -0.5),
        'gdn_a_log': jnp.log(jax.random.uniform(ks[10], (nl, 2, H_A), minval=1.0, maxval=16.0)),
        'gdn_dt_bias': dt + jnp.log(-jnp.expm1(-dt)),
        'gdn_onorm_g': 1.0 + nrm(ks[12], (nl, DV_A), 0.05),
        'rwkv_mu': jax.random.uniform(ks[13], (nl, B_COLS)),
        'rwkv_w0': jax.random.uniform(ks[14], (nl, 2, C_B), minval=-6.0, maxval=-1.0),
        'rwkv_w2': nrm(ks[15], (nl, 2, LORA_W, C_B), 0.1),
        'rwkv_a0': nrm(ks[16], (nl, 2, C_B), 0.1),
        'rwkv_a2': nrm(ks[17], (nl, 2, LORA_A, C_B), 0.1),
        'rwkv_g2': nrm(ks[18], (nl, LORA_G, C_B), LORA_G ** -0.5),
        'rwkv_k_k': 0.85 + nrm(ks[19], (nl, C_B), 0.05),
        'rwkv_k_a': 1.0 + nrm(ks[20], (nl, C_B), 0.05),
        'rwkv_r_k': nrm(ks[21], (nl, H_B, N_B), 0.1),
        'rwkv_lnx_g': 1.0 + nrm(ks[22], (nl, C_B), 0.05),
        'rwkv_lnx_b': nrm(ks[23], (nl, C_B), 0.01),
        'w_o_a': nrm(ks[24], (nl, C_A, d), C_A ** -0.5),
        'w_o_b': nrm(ks[25], (nl, C_B, d), C_B ** -0.5),
        'w_out': nrm(ks[26], (nl, d, d), d ** -0.5),
        'router_grp': nrm(ks[27], (nl, d, N_GROUPS), d ** -0.5),
        'router_grp_b': nrm(ks[28], (nl, N_GROUPS), 0.01),
        'router_exp': nrm(ks[29], (nl, d, N_EXPERTS), d ** -0.5),
        'router_exp_b': nrm(ks[30], (nl, N_EXPERTS), 0.01),
        'moe_w_gate': nrm(ks[31], (nl, N_GROUPS, EXPERTS_PER_GROUP, d, D_EXPERT), d ** -0.5),
        'moe_w_up': nrm(ks[32], (nl, N_GROUPS, EXPERTS_PER_GROUP, d, D_EXPERT), d ** -0.5),
        'moe_w_down': nrm(ks[33], (nl, N_GROUPS, EXPERTS_PER_GROUP, D_EXPERT, d), D_EXPERT ** -0.5),
        'final_norm_g': 1.0 + nrm(ks[34], (d,), 0.05),
    }


def reference(x, c, ctx, c_ctx, ada_w, ada_b, norm_mix_g, norm_ffn_g, w_in, gdn_conv, gdn_a_log, gdn_dt_bias,
              gdn_onorm_g, rwkv_mu, rwkv_w0, rwkv_w2, rwkv_a0, rwkv_a2, rwkv_g2, rwkv_k_k, rwkv_k_a, rwkv_r_k,
              rwkv_lnx_g, rwkv_lnx_b, w_o_a, w_o_b, w_out, router_grp, router_grp_b, router_exp, router_exp_b,
              moe_w_gate, moe_w_up, moe_w_down, final_norm_g):
    rows = x.shape[1] // GRID_W
    s_lat = jax.nn.silu(c)[:, None, :]
    s_ctx = jax.nn.silu(c_ctx)
    h_lat, h_ctx = x, ctx
    for l in range(DEPTH):
        last = l == DEPTH - 1
        m_lat = jnp.split(s_lat @ ada_w[l] + ada_b[l], 6, axis=-1)
        m_ctx = jnp.split(s_ctx @ ada_w[l] + ada_b[l], 6, axis=-1)
        u_lat = modulate(rmsnorm(h_lat, norm_mix_g[l]), m_lat[0], m_lat[1])
        u_ctx = modulate(rmsnorm(h_ctx, norm_mix_g[l]), m_ctx[0], m_ctx[1])
        y_lat, y_ctx = mixing_sublayer(
            u_lat, u_ctx, rows, not last, w_in[l], gdn_conv[l], gdn_a_log[l], gdn_dt_bias[l], gdn_onorm_g[l],
            rwkv_mu[l], rwkv_w0[l], rwkv_w2[l], rwkv_a0[l], rwkv_a2[l], rwkv_g2[l], rwkv_k_k[l], rwkv_k_a[l],
            rwkv_r_k[l], rwkv_lnx_g[l], rwkv_lnx_b[l], w_o_a[l], w_o_b[l], w_out[l])
        h_lat = h_lat + m_lat[2] * y_lat
        h_lat = h_lat + m_lat[5] * hier_moe(
            modulate(rmsnorm(h_lat, norm_ffn_g[l]), m_lat[3], m_lat[4]), router_grp[l], router_grp_b[l],
            router_exp[l], router_exp_b[l], moe_w_gate[l], moe_w_up[l], moe_w_down[l])
        if not last:
            h_ctx = h_ctx + m_ctx[2] * y_ctx
            h_ctx = h_ctx + m_ctx[5] * hier_moe(
                modulate(rmsnorm(h_ctx, norm_ffn_g[l]), m_ctx[3], m_ctx[4]), router_grp[l], router_grp_b[l],
                router_exp[l], router_exp_b[l], moe_w_gate[l], moe_w_up[l], moe_w_down[l])
    return rmsnorm(h_lat, final_norm_g)
```

```python
import functools

import jax
import jax.numpy as jnp
from jax import lax
from jax.experimental import pallas as pl
from jax.experimental.pallas import tpu as pltpu

F32 = jnp.float32
BF16 = jnp.bfloat16
HIGHEST = lax.Precision.HIGHEST

GRID_W = 64
H_A = 4
DK_A = 128
C_A = H_A * DK_A
SHORT_CONV = 5
CHUNK = 64
H_B = 8
N_B = 64
C_B = H_B * N_B
LORA_W = 64
LORA_A = 64
LORA_G = 128
N_GROUPS = 4
EXPERTS_PER_GROUP = 8
N_EXPERTS = N_GROUPS * EXPERTS_PER_GROUP
D_EXPERT = 256
NORM_EPS = 1e-6
LNX_EPS = 1e-5 * N_B

LANE = 128
SUBLANE = 8
VMEM_LIMIT = 56 * 1024 * 1024


def _cparams(*sem):
    return pltpu.CompilerParams(dimension_semantics=sem, vmem_limit_bytes=VMEM_LIMIT)


def _dot(a, b, precision=None):
    return jnp.dot(a, b, preferred_element_type=F32, precision=precision)


def _dot_nt(a, b, precision=None):
    return lax.dot_general(a, b, (((1,), (1,)), ((), ())), preferred_element_type=F32, precision=precision)


def _dot_tn(a, b, precision=None):
    return lax.dot_general(a, b, (((0,), (0,)), ((), ())), preferred_element_type=F32, precision=precision)


def _sigmoid(x):
    return jax.nn.sigmoid(x)


def _silu(x):
    return x * jax.nn.sigmoid(x)


def _softplus(x):
    return jnp.maximum(x, 0.0) + jnp.log1p(jnp.exp(-jnp.abs(x)))


def _ada_kernel(c_ref, w_ref, b_ref, o_ref):
    s = _silu(c_ref[...])
    o_ref[...] = _dot(s, w_ref[...], HIGHEST) + b_ref[...]


def ada_modulation(cc, ada_w, ada_b):
    r, d = cc.shape
    n = ada_w.shape[1]
    tn = 1536
    return pl.pallas_call(
        _ada_kernel,
        grid=(n // tn,),
        in_specs=[pl.BlockSpec((r, d), lambda j: (0, 0)),
                  pl.BlockSpec((d, tn), lambda j: (0, j)),
                  pl.BlockSpec((1, tn), lambda j: (0, j))],
        out_specs=pl.BlockSpec((r, tn), lambda j: (0, j)),
        out_shape=jax.ShapeDtypeStruct((r, n), F32),
        compiler_params=_cparams("arbitrary"),
        name="ada_modulation",
    )(cc, ada_w, ada_b.reshape(1, n))


def _rms_modulate(x, g, shift, scale):
    ms = jnp.mean(x * x, axis=-1, keepdims=True)
    y = x * lax.rsqrt(ms + NORM_EPS) * g
    return y * (1.0 + scale) + shift


def _inproj_kernel(x_ref, g_ref, sh_ref, sc_ref, *rest, n_out):
    w_refs, o_refs = rest[:n_out], rest[n_out:]
    u = _rms_modulate(x_ref[...], g_ref[...], sh_ref[0], sc_ref[0]).astype(BF16)
    for w_ref, o_ref in zip(w_refs, o_refs):
        o_ref[...] = _dot(u, w_ref[...])


def norm_modulate_project(x2d, g, shift, scale, weights, rows_per_mod, tm=256):
    r, d = x2d.shape
    tiles_per_mod = rows_per_mod // tm
    mod_spec = pl.BlockSpec((1, 1, d), lambda i: (i // tiles_per_mod, 0, 0))
    in_specs = [pl.BlockSpec((tm, d), lambda i: (i, 0)),
                pl.BlockSpec((1, d), lambda i: (0, 0)), mod_spec, mod_spec]
    in_specs += [pl.BlockSpec(w.shape, lambda i: (0, 0)) for w in weights]
    return pl.pallas_call(
        functools.partial(_inproj_kernel, n_out=len(weights)),
        grid=(r // tm,),
        in_specs=in_specs,
        out_specs=[pl.BlockSpec((tm, w.shape[1]), lambda i: (i, 0)) for w in weights],
        out_shape=[jax.ShapeDtypeStruct((r, w.shape[1]), F32) for w in weights],
        compiler_params=_cparams("parallel"),
        name="norm_modulate_project",
    )(x2d, g.reshape(1, d), shift, scale, *weights)


def _iota2(shape, dim):
    return lax.broadcasted_iota(jnp.int32, shape, dim)


def _unit_tri_inverse(lm, nil):
    n = lm.shape[0]
    eye = jnp.where(_iota2((n, n), 0) == _iota2((n, n), 1), 1.0, 0.0).astype(F32)
    x = eye - lm
    p = _dot(lm, lm, HIGHEST)
    k = 2
    while 2 * k < nil:
        xp = _dot(jnp.concatenate([x, p], axis=0), p, HIGHEST)
        x = x + xp[:n]
        p = xp[n:]
        k *= 2
    return x + _dot(x, p, HIGHEST)


def _fill_padded(pad_ref, x, n):
    zeros = jnp.zeros((SUBLANE, LANE), F32)
    pad_ref[0:SUBLANE, :] = zeros
    pad_ref[SUBLANE:SUBLANE + n, :] = x
    pad_ref[SUBLANE + n:2 * SUBLANE + n, :] = zeros


def _gdn_chunk(q, k, v, gcol, grow, bcol, s, lower):
    c = q.shape[0]
    row, col = _iota2((c, c), 0), _iota2((c, c), 1)
    if lower:
        incl, strict, incl_t = row >= col, row > col, row <= col
    else:
        incl, strict, incl_t = row <= col, row < col, row >= col
    gc_col = jnp.sum(jnp.where(incl, grow, 0.0), axis=1, keepdims=True)
    gc_row = jnp.sum(jnp.where(incl_t, gcol, 0.0), axis=0, keepdims=True)
    tot = jnp.sum(grow, axis=1, keepdims=True)
    decay = jnp.where(incl, jnp.exp(jnp.where(incl, gc_col - gc_row, 0.0)), 0.0)
    egc = jnp.exp(gc_col)
    kb = k * bcol
    lm = jnp.where(strict, _dot_nt(kb, k, HIGHEST) * decay, 0.0)
    t = _unit_tri_inverse(lm, c)
    u = _dot(t, v * bcol, HIGHEST)
    w = _dot(t, kb * egc, HIGHEST)
    qk = jnp.where(incl, _dot_nt(q, k, HIGHEST) * decay, 0.0)
    v_new = u - _dot(w, s, HIGHEST)
    o = _dot(q * egc, s, HIGHEST) + _dot(qk, v_new, HIGHEST)
    s_new = s * jnp.exp(tot) + _dot_tn(k * jnp.exp(tot - gc_col), v_new, HIGHEST)
    return o, s_new


def _gdn_kernel(qc_ref, kc_ref, vc_ref, ql_ref, kl_ref, vl_ref, zl_ref, cwq_ref, cwk_ref, cwv_ref,
                gcol_ref, grow_ref, og_ref, o_ref, q_s, k_s, v_s, oacc_s, pad_s, st_s, *, lc, ll):
    tile = min(256, lc, ll)

    def prep(x_ref, cw_ref, dst, off, n, mode):
        _fill_padded(pad_s, x_ref[0], n)
        cw = cw_ref[...]
        for t0 in range(0, n, tile):
            acc = cw[0:1, :] * pad_s[pl.ds(SUBLANE - 2 + t0, tile), :]
            for j in range(1, SHORT_CONV):
                acc = acc + cw[j:j + 1, :] * pad_s[pl.ds(SUBLANE - 2 + j + t0, tile), :]
            y = _silu(acc)
            if mode != "v":
                y = y * lax.rsqrt(jnp.sum(y * y, axis=-1, keepdims=True) + NORM_EPS)
            if mode == "q":
                y = y * (DK_A ** -0.5)
            dst[pl.ds(off + t0, tile), :] = y

    prep(qc_ref, cwq_ref, q_s, 0, lc, "q")
    prep(kc_ref, cwk_ref, k_s, 0, lc, "k")
    prep(vc_ref, cwv_ref, v_s, 0, lc, "v")
    prep(ql_ref, cwq_ref, q_s, lc, ll, "q")
    prep(kl_ref, cwk_ref, k_s, lc, ll, "k")
    prep(vl_ref, cwv_ref, v_s, lc, ll, "v")

    oacc_s[...] = jnp.zeros(oacc_s.shape, F32)
    st_s[...] = jnp.zeros(st_s.shape, F32)
    nc_ctx = lc // CHUNK
    nc_tot = (lc + ll) // CHUNK

    def step(s, carry):
        cb = jnp.where(s < nc_ctx, nc_ctx - 1 - s, nc_tot - 1 - (s - nc_ctx))
        for d, c in ((0, s), (1, cb)):
            off = pl.multiple_of(c * CHUNK, CHUNK)
            g4 = gcol_ref[0, 0, c]
            r4 = grow_ref[0, 0, c]
            o, s_new = _gdn_chunk(q_s[pl.ds(off, CHUNK), :], k_s[pl.ds(off, CHUNK), :], v_s[pl.ds(off, CHUNK), :],
                                  g4[:, d:d + 1], r4[d:d + 1, :], g4[:, 2 + d:3 + d], st_s[d], lower=(d == 0))
            st_s[d] = s_new
            oacc_s[pl.ds(off, CHUNK), :] = oacc_s[pl.ds(off, CHUNK), :] + o
        return carry

    lax.fori_loop(0, nc_tot, step, 0)

    og = og_ref[...]
    for t0 in range(0, ll, tile):
        o = oacc_s[pl.ds(lc + t0, tile), :]
        o = o * lax.rsqrt(jnp.mean(o * o, axis=-1, keepdims=True) + NORM_EPS) * og
        o_ref[0, pl.ds(t0, tile), :] = o * _silu(zl_ref[0, pl.ds(t0, tile), :])


def gdn_branch(pa_ctx, pa_lat, conv_w, gcol, grow, onorm_g):
    b, lc, _ = pa_ctx.shape
    ll = pa_lat.shape[1]
    nc = (lc + ll) // CHUNK

    def col(blk, n):
        return pl.BlockSpec((1, n, LANE), lambda i, h: (i, 0, blk * H_A + h))

    def cw(blk):
        return pl.BlockSpec((SUBLANE, LANE), lambda i, h: (0, blk * H_A + h))

    in_specs = [col(0, lc), col(1, lc), col(2, lc), col(0, ll), col(1, ll), col(2, ll), col(3, ll),
                cw(0), cw(1), cw(2),
                pl.BlockSpec((1, 1, nc, CHUNK, 4), lambda i, h: (i, h, 0, 0, 0)),
                pl.BlockSpec((1, 1, nc, 4, CHUNK), lambda i, h: (i, h, 0, 0, 0)),
                pl.BlockSpec((1, LANE), lambda i, h: (0, 0))]
    lt = lc + ll
    return pl.pallas_call(
        functools.partial(_gdn_kernel, lc=lc, ll=ll),
        grid=(b, H_A),
        in_specs=in_specs,
        out_specs=pl.BlockSpec((1, ll, LANE), lambda i, h: (i, 0, h)),
        out_shape=jax.ShapeDtypeStruct((b, ll, C_A), F32),
        scratch_shapes=[pltpu.VMEM((lt, LANE), F32)] * 4
        + [pltpu.VMEM((max(lc, ll) + 2 * SUBLANE, LANE), F32), pltpu.VMEM((2, DK_A, DK_A), F32)],
        compiler_params=_cparams("parallel", "parallel"),
        name="gdn_branch",
    )(pa_ctx, pa_ctx, pa_ctx, pa_lat, pa_lat, pa_lat, pa_lat, conv_w, conv_w, conv_w, gcol, grow,
      onorm_g.reshape(1, LANE))


def _stack_heads(x, head0):
    return jnp.concatenate([jnp.where(head0, x, 0.0), jnp.where(head0, 0.0, x)], axis=0)


def _rwkv_chunk(r, v, a, logw, kdir, b, h, lower):
    c = r.shape[0]
    n = 2 * c
    row, col = _iota2((c, c), 0), _iota2((c, c), 1)
    tri = jnp.where(row >= col if lower else row <= col, 1.0, 0.0).astype(F32)
    lcum = _dot(tri, logw, HIGHEST)
    tot = jnp.sum(logw, axis=0, keepdims=True)
    e_in = jnp.exp(lcum)
    e_ex = jnp.exp(lcum - logw)
    e_neg = jnp.exp(-lcum)
    e_rem = jnp.exp(tot - lcum)
    head0 = _iota2((c, LANE), 1) < N_B
    a2 = _stack_heads(a * e_ex, head0)
    r2 = _stack_heads(r * e_in, head0)
    b2 = _stack_heads(b * e_neg, head0)
    k2 = _stack_heads(kdir * e_neg, head0)
    v2 = _stack_heads(v, head0)
    m = _dot_nt(jnp.concatenate([a2, r2], axis=0), jnp.concatenate([b2, k2], axis=0), HIGHEST)
    brow, bcol = _iota2((n, n), 0), _iota2((n, n), 1)
    same = jnp.logical_not(jnp.logical_xor(brow >= c, bcol >= c))
    tr, tc = jnp.where(brow >= c, brow - c, brow), jnp.where(bcol >= c, bcol - c, bcol)
    strict = same & ((tr > tc) if lower else (tr < tc))
    incl = same & ((tr >= tc) if lower else (tr <= tc))
    t = _unit_tri_inverse(jnp.where(strict, -m[:n, :n], 0.0), c)
    ak = jnp.where(strict, m[:n, n:], 0.0)
    rb = jnp.where(incl, m[n:, :n], 0.0)
    rk = jnp.where(incl, m[n:, n:], 0.0)
    akv_rkv = _dot(jnp.concatenate([ak, rk], axis=0), v2, HIGHEST)
    tw = _dot(t, jnp.concatenate([a2, akv_rkv[:n]], axis=1), HIGHEST)
    p = _dot(jnp.concatenate([tw[:, :LANE], r2], axis=0), h, HIGHEST)
    u = p[:n] + tw[:, LANE:]
    y2 = p[n:] + _dot(rb, u, HIGHEST) + akv_rkv[n:]
    bk = jnp.concatenate([_stack_heads(b * e_rem, head0), _stack_heads(kdir * e_rem, head0)], axis=0)
    g = _dot_tn(logw, jnp.ones((c, LANE), F32), HIGHEST)
    h_new = jnp.exp(g) * h + _dot_tn(bk, jnp.concatenate([u, v2], axis=0), HIGHEST)
    return y2[:c] + y2[c:], h_new


def _rwkv_kernel(*refs, lc, ll):
    ctx_refs, lat_refs = refs[0:6], refs[6:12]
    pch_ref, plo_ref, w0a0_ref, w2_ref, a2_ref, g2_ref, o_ref = refs[12:19]
    r_s, v_s, a_s, lw_s, kd_s, b_s, gate_s, bonus_s, y_s, pad_s, st_s = refs[19:]
    tile = min(256, lc, ll)
    pch = pch_ref[...]
    k_k, k_a, r_k, lnx_g, lnx_b = (pch[i:i + 1, :] for i in range(5))
    mus = [pch[5:6, :], pch[6:7, :], pch[7:8, :]] + [plo_ref[i:i + 1, :] for i in range(3)]
    w0a0 = w0a0_ref[...]
    rr, cc = _iota2((LANE, LANE), 0), _iota2((LANE, LANE), 1)
    seg = jnp.where(jnp.logical_xor(rr >= N_B, cc >= N_B), 0.0, 1.0).astype(F32)

    def prep(src_refs, off, n, is_lat):
        for j in range(6):
            _fill_padded(pad_s.at[j], src_refs[j][0], n)
        for t0 in range(0, n, tile):
            mixed = []
            for j in range(6):
                x = pad_s[j, pl.ds(SUBLANE + t0, tile), :]
                nb = pad_s[j, pl.ds(SUBLANE - 1 + t0, tile), :] + pad_s[j, pl.ds(SUBLANE + 1 + t0, tile), :]
                mixed.append(x + (0.5 * nb - x) * mus[j])
            r, k, v, wl, al, gl = mixed
            wl = jnp.tanh(wl)
            kk = k * k_k
            kk = kk * lax.rsqrt(_dot(kk * kk, seg, HIGHEST) + NORM_EPS)
            ksum = jnp.zeros_like(k)
            for d in range(2):
                w_log = -_softplus(-(w0a0[d:d + 1, :] + _dot(wl, w2_ref[d], HIGHEST))) - 0.5
                iclr = _sigmoid(w0a0[2 + d:3 + d, :] + _dot(al, a2_ref[d], HIGHEST))
                kdir = k * (1.0 + (iclr - 1.0) * k_a)
                ksum = ksum + kdir
                lw_s[d, pl.ds(off + t0, tile), :] = -jnp.exp(w_log)
                kd_s[d, pl.ds(off + t0, tile), :] = kdir
                b_s[d, pl.ds(off + t0, tile), :] = kk * iclr
            r_s[pl.ds(off + t0, tile), :] = r
            v_s[pl.ds(off + t0, tile), :] = v
            a_s[pl.ds(off + t0, tile), :] = -kk
            if is_lat:
                gate_s[pl.ds(t0, tile), :] = _dot(_sigmoid(gl), g2_ref[...], HIGHEST)
                bonus_s[pl.ds(t0, tile), :] = _dot(r * ksum * r_k, seg, HIGHEST) * v

    prep(ctx_refs, 0, lc, False)
    prep(lat_refs, lc, ll, True)

    y_s[...] = jnp.zeros(y_s.shape, F32)
    st_s[...] = jnp.zeros(st_s.shape, F32)
    nc_ctx = lc // CHUNK
    nc_tot = (lc + ll) // CHUNK

    def step(s, carry):
        cb = jnp.where(s < nc_ctx, nc_ctx - 1 - s, nc_tot - 1 - (s - nc_ctx))
        for d, c in ((0, s), (1, cb)):
            rows = pl.ds(pl.multiple_of(c * CHUNK, CHUNK), CHUNK)
            y, h_new = _rwkv_chunk(r_s[rows, :], v_s[rows, :], a_s[rows, :], lw_s[d, rows, :], kd_s[d, rows, :],
                                   b_s[d, rows, :], st_s[d], lower=(d == 0))
            st_s[d] = h_new
            y_s[rows, :] = y_s[rows, :] + y
        return carry

    lax.fori_loop(0, nc_tot, step, 0)

    inv_n = 1.0 / N_B
    for t0 in range(0, ll, tile):
        yf = y_s[pl.ds(lc + t0, tile), :]
        cen = yf - _dot(yf, seg, HIGHEST) * inv_n
        var = _dot(cen * cen, seg, HIGHEST) * inv_n
        y = cen * lax.rsqrt(var + LNX_EPS) * lnx_g + lnx_b
        o_ref[0, pl.ds(t0, tile), :] = (y + bonus_s[pl.ds(t0, tile), :]) * gate_s[pl.ds(t0, tile), :]


def rwkv_branch(pb_ctx, pb_lat, pch, plo, w0a0, w2pad, a2pad, g2):
    b, lc, _ = pb_ctx.shape
    ll = pb_lat.shape[1]
    lt = lc + ll
    pairs = C_B // LANE

    def col(blk, n, per_pair):
        if per_pair:
            return pl.BlockSpec((1, n, LANE), lambda i, p: (i, 0, blk * pairs + p))
        return pl.BlockSpec((1, n, LANE), lambda i, p: (i, 0, 3 * pairs + blk))

    def cols(n):
        return [col(0, n, True), col(1, n, True), col(2, n, True), col(0, n, False), col(1, n, False),
                col(2, n, False)]

    in_specs = cols(lc) + cols(ll) + [
        pl.BlockSpec((SUBLANE, LANE), lambda i, p: (0, p)),
        pl.BlockSpec((SUBLANE, LANE), lambda i, p: (0, 0)),
        pl.BlockSpec((SUBLANE, LANE), lambda i, p: (0, p)),
        pl.BlockSpec((2, LANE, LANE), lambda i, p: (0, 0, p)),
        pl.BlockSpec((2, LANE, LANE), lambda i, p: (0, 0, p)),
        pl.BlockSpec((LANE, LANE), lambda i, p: (0, p))]
    seq = pltpu.VMEM((lt, LANE), F32)
    seq2 = pltpu.VMEM((2, lt, LANE), F32)
    lat = pltpu.VMEM((ll, LANE), F32)
    return pl.pallas_call(
        functools.partial(_rwkv_kernel, lc=lc, ll=ll),
        grid=(b, pairs),
        in_specs=in_specs,
        out_specs=pl.BlockSpec((1, ll, LANE), lambda i, p: (i, 0, p)),
        out_shape=jax.ShapeDtypeStruct((b, ll, C_B), F32),
        scratch_shapes=[seq, seq, seq, seq2, seq2, seq2, lat, lat, seq,
                        pltpu.VMEM((6, max(lc, ll) + 2 * SUBLANE, LANE), F32),
                        pltpu.VMEM((2, LANE, LANE), F32)],
        compiler_params=_cparams("parallel", "parallel"),
        name="rwkv_branch",
    )(*([pb_ctx] * 6), *([pb_lat] * 6), pch, plo, w0a0, w2pad, a2pad, g2)


def _merge_kernel(x_ref, pg_ref, oa_ref, ob_ref, m2_ref, woa_ref, wob_ref, wout_ref, o_ref):
    d = x_ref.shape[1]
    ya = _dot(oa_ref[...].astype(BF16), woa_ref[...])
    yb = _dot(ob_ref[...].astype(BF16), wob_ref[...])
    y = _sigmoid(pg_ref[:, 0:d]) * ya + _sigmoid(pg_ref[:, d:2 * d]) * yb
    o_ref[...] = x_ref[...] + m2_ref[0] * _dot(y.astype(BF16), wout_ref[...])


def merge_residual(x2d, pg, oa, ob, m2, w_o_a, w_o_b, w_out, rows_per_mod, tm=512):
    r, d = x2d.shape

    def rows(n):
        return pl.BlockSpec((tm, n), lambda i: (i, 0))

    def full(w):
        return pl.BlockSpec(w.shape, lambda i: (0, 0))

    tiles_per_mod = rows_per_mod // tm
    return pl.pallas_call(
        _merge_kernel,
        grid=(r // tm,),
        in_specs=[rows(d), rows(2 * d), rows(oa.shape[1]), rows(ob.shape[1]),
                  pl.BlockSpec((1, 1, d), lambda i: (i // tiles_per_mod, 0, 0)),
                  full(w_o_a), full(w_o_b), full(w_out)],
        out_specs=rows(d),
        out_shape=jax.ShapeDtypeStruct((r, d), F32),
        compiler_params=_cparams("parallel"),
        name="merge_residual",
    )(x2d, pg, oa, ob, m2, w_o_a, w_o_b, w_out)


ROUTER_GROUP_LANE0 = N_EXPERTS


def _route_kernel(h_ref, g_ref, sh_ref, sc_ref, wr_ref, br_ref, t_ref, cw_ref):
    t = _rms_modulate(h_ref[...], g_ref[...], sh_ref[0], sc_ref[0])
    t_ref[...] = t.astype(BF16)
    lg = _dot(t, wr_ref[...], HIGHEST) + br_ref[...]
    lane = _iota2(lg.shape, 1)
    lane_f = lane.astype(F32)
    neg = jnp.float32(-jnp.inf)
    big = jnp.float32(2 * LANE)
    is_grp = (lane >= ROUTER_GROUP_LANE0) & (lane < ROUTER_GROUP_LANE0 + N_GROUPS)
    lgg = jnp.where(is_grp, lg, neg)
    mg = jnp.max(lgg, axis=-1, keepdims=True)
    p_grp = 1.0 / jnp.sum(jnp.where(is_grp, jnp.exp(lgg - mg), 0.0), axis=-1, keepdims=True)
    g_sel = jnp.min(jnp.where(lgg == mg, lane_f, big), axis=-1, keepdims=True) - ROUTER_GROUP_LANE0
    grp_of_lane = lax.shift_right_logical(lane, EXPERTS_PER_GROUP.bit_length() - 1).astype(F32)
    in_grp = (lane < N_EXPERTS) & (grp_of_lane == g_sel)
    l1 = jnp.where(in_grp, lg, neg)
    top1 = jnp.max(l1, axis=-1, keepdims=True)
    idx1 = jnp.min(jnp.where(l1 == top1, lane_f, big), axis=-1, keepdims=True)
    l2 = jnp.where(in_grp & (lane_f != idx1), lg, neg)
    top2 = jnp.max(l2, axis=-1, keepdims=True)
    idx2 = jnp.min(jnp.where(l2 == top2, lane_f, big), axis=-1, keepdims=True)
    e2 = jnp.exp(top2 - top1)
    w1 = p_grp / (1.0 + e2)
    cw_ref[...] = jnp.where(lane_f == idx1, w1, jnp.where(lane_f == idx2, w1 * e2, 0.0))


def route(h2d, g, shift, scale, w_router, b_router, rows_per_mod, tm=512):
    r, d = h2d.shape
    tiles_per_mod = rows_per_mod // tm
    mod_spec = pl.BlockSpec((1, 1, d), lambda i: (i // tiles_per_mod, 0, 0))
    return pl.pallas_call(
        _route_kernel,
        grid=(r // tm,),
        in_specs=[pl.BlockSpec((tm, d), lambda i: (i, 0)), pl.BlockSpec((1, d), lambda i: (0, 0)),
                  mod_spec, mod_spec,
                  pl.BlockSpec((d, LANE), lambda i: (0, 0)), pl.BlockSpec((1, LANE), lambda i: (0, 0))],
        out_specs=[pl.BlockSpec((tm, d), lambda i: (i, 0)), pl.BlockSpec((tm, LANE), lambda i: (i, 0))],
        out_shape=[jax.ShapeDtypeStruct((r, d), BF16), jax.ShapeDtypeStruct((r, LANE), F32)],
        compiler_params=_cparams("parallel"),
        name="moe_route",
    )(h2d, g.reshape(1, d), shift, scale, w_router, b_router)


def _experts_kernel(t_ref, cw_ref, h_ref, m5_ref, fg_ref, wg_ref, wu_ref, wd_ref, o_ref, acc_ref):
    e = pl.program_id(1)

    @pl.when(e == 0)
    def _():
        acc_ref[...] = jnp.zeros(acc_ref.shape, F32)

    t = t_ref[...]
    cw = cw_ref[...]
    w = jnp.sum(jnp.where(_iota2(cw.shape, 1) == e, cw, 0.0), axis=-1, keepdims=True)
    hid = _silu(_dot(t, wg_ref[0])) * _dot(t, wu_ref[0]) * w
    acc_ref[...] += _dot(hid.astype(BF16), wd_ref[0])

    @pl.when(e == pl.num_programs(1) - 1)
    def _():
        h2 = h_ref[...] + m5_ref[0] * acc_ref[...]
        ms = jnp.mean(h2 * h2, axis=-1, keepdims=True)
        o_ref[...] = h2 * lax.rsqrt(ms + NORM_EPS) * fg_ref[...]


def experts_residual_norm(t, cw, h2d, m5, final_g, w_gate, w_up, w_down, rows_per_mod, tm=1024):
    r, d = h2d.shape
    ne, _, f = w_gate.shape
    tiles_per_mod = rows_per_mod // tm
    return pl.pallas_call(
        _experts_kernel,
        grid=(r // tm, ne),
        in_specs=[pl.BlockSpec((tm, d), lambda i, e: (i, 0)), pl.BlockSpec((tm, LANE), lambda i, e: (i, 0)),
                  pl.BlockSpec((tm, d), lambda i, e: (i, 0)),
                  pl.BlockSpec((1, 1, d), lambda i, e: (i // tiles_per_mod, 0, 0)),
                  pl.BlockSpec((1, d), lambda i, e: (0, 0)),
                  pl.BlockSpec((1, d, f), lambda i, e: (e, 0, 0)), pl.BlockSpec((1, d, f), lambda i, e: (e, 0, 0)),
                  pl.BlockSpec((1, f, d), lambda i, e: (e, 0, 0))],
        out_specs=pl.BlockSpec((tm, d), lambda i, e: (i, 0)),
        out_shape=jax.ShapeDtypeStruct((r, d), F32),
        scratch_shapes=[pltpu.VMEM((tm, d), F32)],
        compiler_params=_cparams("parallel", "arbitrary"),
        name="moe_experts",
    )(t, cw, h2d, m5, final_g.reshape(1, d), w_gate, w_up, w_down)


def _to_col_major(x, rows):
    b, l, c = x.shape
    return x.reshape(b, rows, GRID_W, c).transpose(0, 2, 1, 3).reshape(b, l, c)


def _to_row_major(x, rows):
    b, l, c = x.shape
    return x.reshape(b, GRID_W, rows, c).transpose(0, 2, 1, 3).reshape(b, l, c)


def _pad_rows(a, n):
    return jnp.pad(a, ((0, n - a.shape[0]),) + ((0, 0),) * (a.ndim - 1))


def _gdn_gates(pa, a_log, dt_bias):
    b, l, _ = pa.shape
    ab = pa[..., 4 * C_A:4 * C_A + 4 * H_A]
    a = ab[..., :2 * H_A].reshape(b, l, 2, H_A)
    bt = ab[..., 2 * H_A:].reshape(b, l, 2, H_A)
    g = -jnp.exp(a_log) * jax.nn.softplus(a + dt_bias)
    beta = jax.nn.sigmoid(bt)
    return jnp.concatenate([g, beta], axis=2).transpose(0, 3, 1, 2)


def kernel(x, c, ctx, c_ctx, ada_w, ada_b, norm_mix_g, norm_ffn_g, w_in, gdn_conv, gdn_a_log, gdn_dt_bias,
           gdn_onorm_g, rwkv_mu, rwkv_w0, rwkv_w2, rwkv_a0, rwkv_a2, rwkv_g2, rwkv_k_k, rwkv_k_a, rwkv_r_k,
           rwkv_lnx_g, rwkv_lnx_b, w_o_a, w_o_b, w_out, router_grp, router_grp_b, router_exp, router_exp_b,
           moe_w_gate, moe_w_up, moe_w_down, final_norm_g):
    bsz, seq, d = x.shape
    lc = ctx.shape[1]
    rows = seq // GRID_W
    a_cols = 4 * C_A + 4 * H_A
    b_cols = 3 * C_B + 2 * LORA_W + 2 * LORA_A + LORA_G

    cc = _pad_rows(jnp.concatenate([c, c_ctx[None]], axis=0), 2 * SUBLANE)
    mod = ada_modulation(cc, ada_w[0], ada_b[0])
    m_lat = [mod[:bsz, i * d:(i + 1) * d].reshape(bsz, 1, d) for i in range(6)]
    m_ctx = [mod[bsz:bsz + 1, i * d:(i + 1) * d].reshape(1, 1, d) for i in range(2)]

    w = w_in[0]
    w_a = jnp.pad(w[:, :a_cols], ((0, 0), (0, 4 * C_A + LANE - a_cols))).astype(BF16)
    w_b = w[:, a_cols:a_cols + b_cols].astype(BF16)
    w_g = w[:, a_cols + b_cols:].astype(BF16)

    x2d = x.reshape(bsz * seq, d)
    pa_lat, pg_lat = norm_modulate_project(x2d, norm_mix_g[0], m_lat[0], m_lat[1], [w_a, w_g], seq)
    (pb_lat,) = norm_modulate_project(_to_col_major(x, rows).reshape(bsz * seq, d), norm_mix_g[0],
                                      m_lat[0], m_lat[1], [w_b], seq)
    pa_ctx, pb_ctx = norm_modulate_project(ctx.reshape(bsz * lc, d), norm_mix_g[0], m_ctx[0], m_ctx[1],
                                           [w_a, w_b], bsz * lc)
    pa_lat = pa_lat.reshape(bsz, seq, -1)
    pb_lat = pb_lat.reshape(bsz, seq, -1)
    pa_ctx = pa_ctx.reshape(bsz, lc, -1)
    pb_ctx = pb_ctx.reshape(bsz, lc, -1)

    gates = jnp.concatenate([_gdn_gates(pa_ctx, gdn_a_log[0], gdn_dt_bias[0]),
                             _gdn_gates(pa_lat, gdn_a_log[0], gdn_dt_bias[0])], axis=2)
    nc = (lc + seq) // CHUNK
    gcol = gates.reshape(bsz, H_A, nc, CHUNK, 4)
    grow = gcol.transpose(0, 1, 2, 4, 3)
    oa = gdn_branch(pa_ctx, pa_lat, _pad_rows(gdn_conv[0], SUBLANE), gcol, grow, gdn_onorm_g[0])

    mu = rwkv_mu[0]
    pch = jnp.stack([rwkv_k_k[0], rwkv_k_a[0], rwkv_r_k[0].reshape(C_B), rwkv_lnx_g[0], rwkv_lnx_b[0],
                     mu[:C_B], mu[C_B:2 * C_B], mu[2 * C_B:3 * C_B]])
    plo = _pad_rows(mu[3 * C_B:].reshape(3, LANE), SUBLANE)
    w0a0 = _pad_rows(jnp.concatenate([rwkv_w0[0], rwkv_a0[0]], axis=0), SUBLANE)
    zw = jnp.zeros((LORA_W, C_B), F32)
    w2pad = jnp.stack([jnp.concatenate([rwkv_w2[0, 0], zw]), jnp.concatenate([zw, rwkv_w2[0, 1]])])
    a2pad = jnp.stack([jnp.concatenate([rwkv_a2[0, 0], zw]), jnp.concatenate([zw, rwkv_a2[0, 1]])])
    ob = rwkv_branch(pb_ctx, pb_lat, pch, plo, w0a0, w2pad, a2pad, rwkv_g2[0])
    ob = _to_row_major(ob, rows)

    h1 = merge_residual(x2d, pg_lat, oa.reshape(bsz * seq, C_A), ob.reshape(bsz * seq, C_B), m_lat[2],
                        w_o_a[0].astype(BF16), w_o_b[0].astype(BF16), w_out[0].astype(BF16), seq)

    w_router = jnp.pad(jnp.concatenate([router_exp[0], router_grp[0]], axis=1),
                       ((0, 0), (0, LANE - N_EXPERTS - N_GROUPS)))
    b_router = jnp.pad(jnp.concatenate([router_exp_b[0], router_grp_b[0]]),
                       (0, LANE - N_EXPERTS - N_GROUPS)).reshape(1, LANE)
    t, cw = route(h1, norm_ffn_g[0], m_lat[3], m_lat[4], w_router, b_router, seq)
    out = experts_residual_norm(t, cw, h1, m_lat[5], final_norm_g,
                                moe_w_gate[0].reshape(N_EXPERTS, d, D_EXPERT).astype(BF16),
                                moe_w_up[0].reshape(N_EXPERTS, d, D_EXPERT).astype(BF16),
                                moe_w_down[0].reshape(N_EXPERTS, D_EXPERT, d).astype(BF16), seq)
    return out.reshape(bsz, seq, d)
```

```python
import functools

import jax
import jax.numpy as jnp
from jax import lax
from jax.experimental import pallas as pl
from jax.experimental.pallas import tpu as pltpu

F32 = jnp.float32
BF16 = jnp.bfloat16
HIGHEST = lax.Precision.HIGHEST

GRID_W = 64
H_A = 4
DK_A = 128
C_A = H_A * DK_A
SHORT_CONV = 5
CHUNK = 64
H_B = 8
N_B = 64
C_B = H_B * N_B
LORA_W = 64
LORA_A = 64
LORA_G = 128
N_GROUPS = 4
EXPERTS_PER_GROUP = 8
N_EXPERTS = N_GROUPS * EXPERTS_PER_GROUP
D_EXPERT = 256
NORM_EPS = 1e-6
LNX_EPS = 1e-5 * N_B

LANE = 128
SUBLANE = 8
VMEM_LIMIT = 56 * 1024 * 1024


def _cparams(*sem):
    return pltpu.CompilerParams(dimension_semantics=sem, vmem_limit_bytes=VMEM_LIMIT)


SINGLE = "bf16 operands, one MXU pass, f32 accumulation"
MM = SINGLE


def _operands(a, b, precision):
    if precision is SINGLE:
        return a.astype(BF16), b.astype(BF16), None
    return a, b, precision


def _dot(a, b, precision=None):
    a, b, precision = _operands(a, b, precision)
    return jnp.dot(a, b, preferred_element_type=F32, precision=precision)


def _dot_nt(a, b, precision=None):
    a, b, precision = _operands(a, b, precision)
    return lax.dot_general(a, b, (((1,), (1,)), ((), ())), preferred_element_type=F32, precision=precision)


def _dot_tn(a, b, precision=None):
    a, b, precision = _operands(a, b, precision)
    return lax.dot_general(a, b, (((0,), (0,)), ((), ())), preferred_element_type=F32, precision=precision)


def _split3(x):
    hi = x.astype(BF16)
    r1 = x - hi.astype(F32)
    mid = r1.astype(BF16)
    lo = (r1 - mid.astype(F32)).astype(BF16)
    return hi, mid, lo


def _dot_small_int_lhs(m, x):
    mb = m.astype(BF16)
    hi, mid, lo = _split3(x)
    return _dot(mb, hi) + _dot(mb, mid) + _dot(mb, lo)


def _dot_small_int_rhs(x, m):
    mb = m.astype(BF16)
    hi, mid, lo = _split3(x)
    return _dot(hi, mb) + _dot(mid, mb) + _dot(lo, mb)


def _sigmoid(x):
    return jax.nn.sigmoid(x)


def _silu(x):
    return x * jax.nn.sigmoid(x)


def _softplus(x):
    return jnp.maximum(x, 0.0) + jnp.log1p(jnp.exp(-jnp.abs(x)))


def _ada_kernel(c_ref, w_ref, b_ref, o_ref):
    s = _silu(c_ref[...])
    o_ref[...] = _dot(s, w_ref[...], HIGHEST) + b_ref[...]


def ada_modulation(cc, ada_w, ada_b):
    r, d = cc.shape
    n = ada_w.shape[1]
    tn = 1536
    return pl.pallas_call(
        _ada_kernel,
        grid=(n // tn,),
        in_specs=[pl.BlockSpec((r, d), lambda j: (0, 0)),
                  pl.BlockSpec((d, tn), lambda j: (0, j)),
                  pl.BlockSpec((1, tn), lambda j: (0, j))],
        out_specs=pl.BlockSpec((r, tn), lambda j: (0, j)),
        out_shape=jax.ShapeDtypeStruct((r, n), F32),
        compiler_params=_cparams("arbitrary"),
        name="ada_modulation",
    )(cc, ada_w, ada_b.reshape(1, n))


def _rms_modulate(x, g, shift, scale):
    ms = jnp.mean(x * x, axis=-1, keepdims=True)
    y = x * lax.rsqrt(ms + NORM_EPS) * g
    return y * (1.0 + scale) + shift


def _inproj_kernel(x_ref, g_ref, sh_ref, sc_ref, *rest, n_out):
    w_refs, o_refs = rest[:n_out], rest[n_out:]
    u = _rms_modulate(x_ref[...], g_ref[...], sh_ref[0], sc_ref[0]).astype(BF16)
    for w_ref, o_ref in zip(w_refs, o_refs):
        o_ref[...] = _dot(u, w_ref[...])


def norm_modulate_project(x2d, g, shift, scale, weights, rows_per_mod, tm=256):
    r, d = x2d.shape
    tiles_per_mod = rows_per_mod // tm
    mod_spec = pl.BlockSpec((1, 1, d), lambda i: (i // tiles_per_mod, 0, 0))
    in_specs = [pl.BlockSpec((tm, d), lambda i: (i, 0)),
                pl.BlockSpec((1, d), lambda i: (0, 0)), mod_spec, mod_spec]
    in_specs += [pl.BlockSpec(w.shape, lambda i: (0, 0)) for w in weights]
    return pl.pallas_call(
        functools.partial(_inproj_kernel, n_out=len(weights)),
        grid=(r // tm,),
        in_specs=in_specs,
        out_specs=[pl.BlockSpec((tm, w.shape[1]), lambda i: (i, 0)) for w in weights],
        out_shape=[jax.ShapeDtypeStruct((r, w.shape[1]), F32) for w in weights],
        compiler_params=_cparams("parallel"),
        name="norm_modulate_project",
    )(x2d, g.reshape(1, d), shift, scale, *weights)


def _iota2(shape, dim):
    return lax.broadcasted_iota(jnp.int32, shape, dim)


def _unit_tri_inverse(lm, nil):
    n = lm.shape[0]
    eye = jnp.where(_iota2((n, n), 0) == _iota2((n, n), 1), 1.0, 0.0).astype(F32)
    x = eye - lm
    p = _dot(lm, lm, MM)
    k = 2
    while 2 * k < nil:
        xp = _dot(jnp.concatenate([x, p], axis=0), p, MM)
        x = x + xp[:n]
        p = xp[n:]
        k *= 2
    return x + _dot(x, p, MM)


def _fill_padded(pad_ref, x, n):
    zeros = jnp.zeros((SUBLANE, LANE), F32)
    pad_ref[0:SUBLANE, :] = zeros
    pad_ref[SUBLANE:SUBLANE + n, :] = x
    pad_ref[SUBLANE + n:2 * SUBLANE + n, :] = zeros


def _gdn_chunk(q, k, v, gcol, grow, bcol, s, lower):
    c = q.shape[0]
    row, col = _iota2((c, c), 0), _iota2((c, c), 1)
    if lower:
        incl, strict, incl_t = row >= col, row > col, row <= col
    else:
        incl, strict, incl_t = row <= col, row < col, row >= col
    gc_col = jnp.sum(jnp.where(incl, grow, 0.0), axis=1, keepdims=True)
    gc_row = jnp.sum(jnp.where(incl_t, gcol, 0.0), axis=0, keepdims=True)
    tot = jnp.sum(grow, axis=1, keepdims=True)
    decay = jnp.where(incl, jnp.exp(jnp.where(incl, gc_col - gc_row, 0.0)), 0.0)
    egc = jnp.exp(gc_col)
    kb = k * bcol
    lm = jnp.where(strict, _dot_nt(kb, k, MM) * decay, 0.0)
    t = _unit_tri_inverse(lm, c)
    u = _dot(t, v * bcol, MM)
    w = _dot(t, kb * egc, MM)
    qk = jnp.where(incl, _dot_nt(q, k, MM) * decay, 0.0)
    v_new = u - _dot(w, s, MM)
    o = _dot(q * egc, s, MM) + _dot(qk, v_new, MM)
    s_new = s * jnp.exp(tot) + _dot_tn(k * jnp.exp(tot - gc_col), v_new, MM)
    return o, s_new


def _gdn_kernel(qc_ref, kc_ref, vc_ref, ql_ref, kl_ref, vl_ref, zl_ref, cwq_ref, cwk_ref, cwv_ref,
                gcol_ref, grow_ref, og_ref, o_ref, q_s, k_s, v_s, oacc_s, pad_s, st_s, *, lc, ll):
    tile = min(256, lc, ll)

    def prep(x_ref, cw_ref, dst, off, n, mode):
        _fill_padded(pad_s, x_ref[0], n)
        cw = cw_ref[...]
        for t0 in range(0, n, tile):
            acc = cw[0:1, :] * pad_s[pl.ds(SUBLANE - 2 + t0, tile), :]
            for j in range(1, SHORT_CONV):
                acc = acc + cw[j:j + 1, :] * pad_s[pl.ds(SUBLANE - 2 + j + t0, tile), :]
            y = _silu(acc)
            if mode != "v":
                y = y * lax.rsqrt(jnp.sum(y * y, axis=-1, keepdims=True) + NORM_EPS)
            if mode == "q":
                y = y * (DK_A ** -0.5)
            dst[pl.ds(off + t0, tile), :] = y

    prep(qc_ref, cwq_ref, q_s, 0, lc, "q")
    prep(kc_ref, cwk_ref, k_s, 0, lc, "k")
    prep(vc_ref, cwv_ref, v_s, 0, lc, "v")
    prep(ql_ref, cwq_ref, q_s, lc, ll, "q")
    prep(kl_ref, cwk_ref, k_s, lc, ll, "k")
    prep(vl_ref, cwv_ref, v_s, lc, ll, "v")

    oacc_s[...] = jnp.zeros(oacc_s.shape, F32)
    st_s[...] = jnp.zeros(st_s.shape, F32)
    nc_ctx = lc // CHUNK
    nc_tot = (lc + ll) // CHUNK

    def step(s, carry):
        cb = jnp.where(s < nc_ctx, nc_ctx - 1 - s, nc_tot - 1 - (s - nc_ctx))
        for d, c in ((0, s), (1, cb)):
            off = pl.multiple_of(c * CHUNK, CHUNK)
            g4 = gcol_ref[0, 0, c]
            r4 = grow_ref[0, 0, c]
            o, s_new = _gdn_chunk(q_s[pl.ds(off, CHUNK), :], k_s[pl.ds(off, CHUNK), :], v_s[pl.ds(off, CHUNK), :],
                                  g4[:, d:d + 1], r4[d:d + 1, :], g4[:, 2 + d:3 + d], st_s[d], lower=(d == 0))
            st_s[d] = s_new
            oacc_s[pl.ds(off, CHUNK), :] = oacc_s[pl.ds(off, CHUNK), :] + o
        return carry

    lax.fori_loop(0, nc_tot, step, 0)

    og = og_ref[...]
    for t0 in range(0, ll, tile):
        o = oacc_s[pl.ds(lc + t0, tile), :]
        o = o * lax.rsqrt(jnp.mean(o * o, axis=-1, keepdims=True) + NORM_EPS) * og
        o_ref[0, pl.ds(t0, tile), :] = o * _silu(zl_ref[0, pl.ds(t0, tile), :])


def gdn_branch(pa_ctx, pa_lat, conv_w, gcol, grow, onorm_g):
    b, lc, _ = pa_ctx.shape
    ll = pa_lat.shape[1]
    nc = (lc + ll) // CHUNK

    def col(blk, n):
        return pl.BlockSpec((1, n, LANE), lambda i, h: (i, 0, blk * H_A + h))

    def cw(blk):
        return pl.BlockSpec((SUBLANE, LANE), lambda i, h: (0, blk * H_A + h))

    in_specs = [col(0, lc), col(1, lc), col(2, lc), col(0, ll), col(1, ll), col(2, ll), col(3, ll),
                cw(0), cw(1), cw(2),
                pl.BlockSpec((1, 1, nc, CHUNK, 4), lambda i, h: (i, h, 0, 0, 0)),
                pl.BlockSpec((1, 1, nc, 4, CHUNK), lambda i, h: (i, h, 0, 0, 0)),
                pl.BlockSpec((1, LANE), lambda i, h: (0, 0))]
    lt = lc + ll
    return pl.pallas_call(
        functools.partial(_gdn_kernel, lc=lc, ll=ll),
        grid=(b, H_A),
        in_specs=in_specs,
        out_specs=pl.BlockSpec((1, ll, LANE), lambda i, h: (i, 0, h)),
        out_shape=jax.ShapeDtypeStruct((b, ll, C_A), F32),
        scratch_shapes=[pltpu.VMEM((lt, LANE), F32)] * 4
        + [pltpu.VMEM((max(lc, ll) + 2 * SUBLANE, LANE), F32), pltpu.VMEM((2, DK_A, DK_A), F32)],
        compiler_params=_cparams("parallel", "parallel"),
        name="gdn_branch",
    )(pa_ctx, pa_ctx, pa_ctx, pa_lat, pa_lat, pa_lat, pa_lat, conv_w, conv_w, conv_w, gcol, grow,
      onorm_g.reshape(1, LANE))


def _stack_heads(x, head0):
    return jnp.concatenate([jnp.where(head0, x, 0.0), jnp.where(head0, 0.0, x)], axis=0)


def _rwkv_chunk(r, v, a, logw, kdir, b, h, lower):
    c = r.shape[0]
    n = 2 * c
    row, col = _iota2((c, c), 0), _iota2((c, c), 1)
    tri = jnp.where(row >= col if lower else row <= col, 1.0, 0.0).astype(F32)
    lcum = _dot_small_int_lhs(tri, logw)
    tot = jnp.sum(logw, axis=0, keepdims=True)
    e_in = jnp.exp(lcum)
    e_ex = jnp.exp(lcum - logw)
    e_neg = jnp.exp(-lcum)
    e_rem = jnp.exp(tot - lcum)
    head0 = _iota2((c, LANE), 1) < N_B
    a2 = _stack_heads(a * e_ex, head0)
    r2 = _stack_heads(r * e_in, head0)
    b2 = _stack_heads(b * e_neg, head0)
    k2 = _stack_heads(kdir * e_neg, head0)
    v2 = _stack_heads(v, head0)
    m = _dot_nt(jnp.concatenate([a2, r2], axis=0), jnp.concatenate([b2, k2], axis=0), MM)
    brow, bcol = _iota2((n, n), 0), _iota2((n, n), 1)
    same = jnp.logical_not(jnp.logical_xor(brow >= c, bcol >= c))
    tr, tc = jnp.where(brow >= c, brow - c, brow), jnp.where(bcol >= c, bcol - c, bcol)
    strict = same & ((tr > tc) if lower else (tr < tc))
    incl = same & ((tr >= tc) if lower else (tr <= tc))
    t = _unit_tri_inverse(jnp.where(strict, -m[:n, :n], 0.0), c)
    ak = jnp.where(strict, m[:n, n:], 0.0)
    rb = jnp.where(incl, m[n:, :n], 0.0)
    rk = jnp.where(incl, m[n:, n:], 0.0)
    akv_rkv = _dot(jnp.concatenate([ak, rk], axis=0), v2, MM)
    tw = _dot(t, jnp.concatenate([a2, akv_rkv[:n]], axis=1), MM)
    p = _dot(jnp.concatenate([tw[:, :LANE], r2], axis=0), h, MM)
    u = p[:n] + tw[:, LANE:]
    y2 = p[n:] + _dot(rb, u, MM) + akv_rkv[n:]
    bk = jnp.concatenate([_stack_heads(b * e_rem, head0), _stack_heads(kdir * e_rem, head0)], axis=0)
    ones = jnp.ones((c, LANE), BF16)
    g = sum(_dot_tn(part, ones) for part in _split3(logw))
    h_new = jnp.exp(g) * h + _dot_tn(bk, jnp.concatenate([u, v2], axis=0), MM)
    return y2[:c] + y2[c:], h_new


def _rwkv_kernel(*refs, lc, ll):
    ctx_refs, lat_refs = refs[0:6], refs[6:12]
    pch_ref, plo_ref, w0a0_ref, w2_ref, a2_ref, g2_ref, o_ref = refs[12:19]
    r_s, v_s, a_s, lw_s, kd_s, b_s, gate_s, bonus_s, y_s, pad_s, st_s = refs[19:]
    tile = min(256, lc, ll)
    pch = pch_ref[...]
    k_k, k_a, r_k, lnx_g, lnx_b = (pch[i:i + 1, :] for i in range(5))
    mus = [pch[5:6, :], pch[6:7, :], pch[7:8, :]] + [plo_ref[i:i + 1, :] for i in range(3)]
    w0a0 = w0a0_ref[...]
    rr, cc = _iota2((LANE, LANE), 0), _iota2((LANE, LANE), 1)
    seg = jnp.where(jnp.logical_xor(rr >= N_B, cc >= N_B), 0.0, 1.0).astype(F32)

    def prep(src_refs, off, n, is_lat):
        for j in range(6):
            _fill_padded(pad_s.at[j], src_refs[j][0], n)
        for t0 in range(0, n, tile):
            mixed = []
            for j in range(6):
                x = pad_s[j, pl.ds(SUBLANE + t0, tile), :]
                nb = pad_s[j, pl.ds(SUBLANE - 1 + t0, tile), :] + pad_s[j, pl.ds(SUBLANE + 1 + t0, tile), :]
                mixed.append(x + (0.5 * nb - x) * mus[j])
            r, k, v, wl, al, gl = mixed
            wl = jnp.tanh(wl)
            kk = k * k_k
            kk = kk * lax.rsqrt(_dot_small_int_rhs(kk * kk, seg) + NORM_EPS)
            ksum = jnp.zeros_like(k)
            for d in range(2):
                w_log = -_softplus(-(w0a0[d:d + 1, :] + _dot(wl, w2_ref[d], MM))) - 0.5
                iclr = _sigmoid(w0a0[2 + d:3 + d, :] + _dot(al, a2_ref[d], MM))
                kdir = k * (1.0 + (iclr - 1.0) * k_a)
                ksum = ksum + kdir
                lw_s[d, pl.ds(off + t0, tile), :] = -jnp.exp(w_log)
                kd_s[d, pl.ds(off + t0, tile), :] = kdir
                b_s[d, pl.ds(off + t0, tile), :] = kk * iclr
            r_s[pl.ds(off + t0, tile), :] = r
            v_s[pl.ds(off + t0, tile), :] = v
            a_s[pl.ds(off + t0, tile), :] = -kk
            if is_lat:
                gate_s[pl.ds(t0, tile), :] = _dot(_sigmoid(gl), g2_ref[...], MM)
                bonus_s[pl.ds(t0, tile), :] = _dot_small_int_rhs(r * ksum * r_k, seg) * v

    prep(ctx_refs, 0, lc, False)
    prep(lat_refs, lc, ll, True)

    y_s[...] = jnp.zeros(y_s.shape, F32)
    st_s[...] = jnp.zeros(st_s.shape, F32)
    nc_ctx = lc // CHUNK
    nc_tot = (lc + ll) // CHUNK

    def step(s, carry):
        cb = jnp.where(s < nc_ctx, nc_ctx - 1 - s, nc_tot - 1 - (s - nc_ctx))
        for d, c in ((0, s), (1, cb)):
            rows = pl.ds(pl.multiple_of(c * CHUNK, CHUNK), CHUNK)
            y, h_new = _rwkv_chunk(r_s[rows, :], v_s[rows, :], a_s[rows, :], lw_s[d, rows, :], kd_s[d, rows, :],
                                   b_s[d, rows, :], st_s[d], lower=(d == 0))
            st_s[d] = h_new
            y_s[rows, :] = y_s[rows, :] + y
        return carry

    lax.fori_loop(0, nc_tot, step, 0)

    inv_n = 1.0 / N_B
    for t0 in range(0, ll, tile):
        yf = y_s[pl.ds(lc + t0, tile), :]
        cen = yf - _dot_small_int_rhs(yf, seg) * inv_n
        var = _dot_small_int_rhs(cen * cen, seg) * inv_n
        y = cen * lax.rsqrt(var + LNX_EPS) * lnx_g + lnx_b
        o_ref[0, pl.ds(t0, tile), :] = (y + bonus_s[pl.ds(t0, tile), :]) * gate_s[pl.ds(t0, tile), :]


def rwkv_branch(pb_ctx, pb_lat, pch, plo, w0a0, w2pad, a2pad, g2):
    b, lc, _ = pb_ctx.shape
    ll = pb_lat.shape[1]
    lt = lc + ll
    pairs = C_B // LANE

    def col(blk, n, per_pair):
        if per_pair:
            return pl.BlockSpec((1, n, LANE), lambda i, p: (i, 0, blk * pairs + p))
        return pl.BlockSpec((1, n, LANE), lambda i, p: (i, 0, 3 * pairs + blk))

    def cols(n):
        return [col(0, n, True), col(1, n, True), col(2, n, True), col(0, n, False), col(1, n, False),
                col(2, n, False)]

    in_specs = cols(lc) + cols(ll) + [
        pl.BlockSpec((SUBLANE, LANE), lambda i, p: (0, p)),
        pl.BlockSpec((SUBLANE, LANE), lambda i, p: (0, 0)),
        pl.BlockSpec((SUBLANE, LANE), lambda i, p: (0, p)),
        pl.BlockSpec((2, LANE, LANE), lambda i, p: (0, 0, p)),
        pl.BlockSpec((2, LANE, LANE), lambda i, p: (0, 0, p)),
        pl.BlockSpec((LANE, LANE), lambda i, p: (0, p))]
    seq = pltpu.VMEM((lt, LANE), F32)
    seq2 = pltpu.VMEM((2, lt, LANE), F32)
    lat = pltpu.VMEM((ll, LANE), F32)
    return pl.pallas_call(
        functools.partial(_rwkv_kernel, lc=lc, ll=ll),
        grid=(b, pairs),
        in_specs=in_specs,
        out_specs=pl.BlockSpec((1, ll, LANE), lambda i, p: (i, 0, p)),
        out_shape=jax.ShapeDtypeStruct((b, ll, C_B), F32),
        scratch_shapes=[seq, seq, seq, seq2, seq2, seq2, lat, lat, seq,
                        pltpu.VMEM((6, max(lc, ll) + 2 * SUBLANE, LANE), F32),
                        pltpu.VMEM((2, LANE, LANE), F32)],
        compiler_params=_cparams("parallel", "parallel"),
        name="rwkv_branch",
    )(*([pb_ctx] * 6), *([pb_lat] * 6), pch, plo, w0a0, w2pad, a2pad, g2)


def _merge_kernel(x_ref, pg_ref, oa_ref, ob_ref, m2_ref, woa_ref, wob_ref, wout_ref, o_ref):
    d = x_ref.shape[1]
    ya = _dot(oa_ref[...].astype(BF16), woa_ref[...])
    yb = _dot(ob_ref[...].astype(BF16), wob_ref[...])
    y = _sigmoid(pg_ref[:, 0:d]) * ya + _sigmoid(pg_ref[:, d:2 * d]) * yb
    o_ref[...] = x_ref[...] + m2_ref[0] * _dot(y.astype(BF16), wout_ref[...])


def merge_residual(x2d, pg, oa, ob, m2, w_o_a, w_o_b, w_out, rows_per_mod, tm=512):
    r, d = x2d.shape

    def rows(n):
        return pl.BlockSpec((tm, n), lambda i: (i, 0))

    def full(w):
        return pl.BlockSpec(w.shape, lambda i: (0, 0))

    tiles_per_mod = rows_per_mod // tm
    return pl.pallas_call(
        _merge_kernel,
        grid=(r // tm,),
        in_specs=[rows(d), rows(2 * d), rows(oa.shape[1]), rows(ob.shape[1]),
                  pl.BlockSpec((1, 1, d), lambda i: (i // tiles_per_mod, 0, 0)),
                  full(w_o_a), full(w_o_b), full(w_out)],
        out_specs=rows(d),
        out_shape=jax.ShapeDtypeStruct((r, d), F32),
        compiler_params=_cparams("parallel"),
        name="merge_residual",
    )(x2d, pg, oa, ob, m2, w_o_a, w_o_b, w_out)


ROUTER_GROUP_LANE0 = N_EXPERTS


def _route_kernel(h_ref, g_ref, sh_ref, sc_ref, wr_ref, br_ref, t_ref, cw_ref):
    t = _rms_modulate(h_ref[...], g_ref[...], sh_ref[0], sc_ref[0])
    t_ref[...] = t.astype(BF16)
    lg = _dot(t, wr_ref[...], HIGHEST) + br_ref[...]
    lane = _iota2(lg.shape, 1)
    lane_f = lane.astype(F32)
    neg = jnp.float32(-jnp.inf)
    big = jnp.float32(2 * LANE)
    is_grp = (lane >= ROUTER_GROUP_LANE0) & (lane < ROUTER_GROUP_LANE0 + N_GROUPS)
    lgg = jnp.where(is_grp, lg, neg)
    mg = jnp.max(lgg, axis=-1, keepdims=True)
    p_grp = 1.0 / jnp.sum(jnp.where(is_grp, jnp.exp(lgg - mg), 0.0), axis=-1, keepdims=True)
    g_sel = jnp.min(jnp.where(lgg == mg, lane_f, big), axis=-1, keepdims=True) - ROUTER_GROUP_LANE0
    grp_of_lane = lax.shift_right_logical(lane, EXPERTS_PER_GROUP.bit_length() - 1).astype(F32)
    in_grp = (lane < N_EXPERTS) & (grp_of_lane == g_sel)
    l1 = jnp.where(in_grp, lg, neg)
    top1 = jnp.max(l1, axis=-1, keepdims=True)
    idx1 = jnp.min(jnp.where(l1 == top1, lane_f, big), axis=-1, keepdims=True)
    l2 = jnp.where(in_grp & (lane_f != idx1), lg, neg)
    top2 = jnp.max(l2, axis=-1, keepdims=True)
    idx2 = jnp.min(jnp.where(l2 == top2, lane_f, big), axis=-1, keepdims=True)
    e2 = jnp.exp(top2 - top1)
    w1 = p_grp / (1.0 + e2)
    cw_ref[...] = jnp.where(lane_f == idx1, w1, jnp.where(lane_f == idx2, w1 * e2, 0.0))


def route(h2d, g, shift, scale, w_router, b_router, rows_per_mod, tm=512):
    r, d = h2d.shape
    tiles_per_mod = rows_per_mod // tm
    mod_spec = pl.BlockSpec((1, 1, d), lambda i: (i // tiles_per_mod, 0, 0))
    return pl.pallas_call(
        _route_kernel,
        grid=(r // tm,),
        in_specs=[pl.BlockSpec((tm, d), lambda i: (i, 0)), pl.BlockSpec((1, d), lambda i: (0, 0)),
                  mod_spec, mod_spec,
                  pl.BlockSpec((d, LANE), lambda i: (0, 0)), pl.BlockSpec((1, LANE), lambda i: (0, 0))],
        out_specs=[pl.BlockSpec((tm, d), lambda i: (i, 0)), pl.BlockSpec((tm, LANE), lambda i: (i, 0))],
        out_shape=[jax.ShapeDtypeStruct((r, d), BF16), jax.ShapeDtypeStruct((r, LANE), F32)],
        compiler_params=_cparams("parallel"),
        name="moe_route",
    )(h2d, g.reshape(1, d), shift, scale, w_router, b_router)


def _experts_kernel(t_ref, cw_ref, h_ref, m5_ref, fg_ref, wg_ref, wu_ref, wd_ref, o_ref, acc_ref):
    e = pl.program_id(1)

    @pl.when(e == 0)
    def _():
        acc_ref[...] = jnp.zeros(acc_ref.shape, F32)

    t = t_ref[...]
    cw = cw_ref[...]
    w = jnp.sum(jnp.where(_iota2(cw.shape, 1) == e, cw, 0.0), axis=-1, keepdims=True)
    hid = _silu(_dot(t, wg_ref[0])) * _dot(t, wu_ref[0]) * w
    acc_ref[...] += _dot(hid.astype(BF16), wd_ref[0])

    @pl.when(e == pl.num_programs(1) - 1)
    def _():
        h2 = h_ref[...] + m5_ref[0] * acc_ref[...]
        ms = jnp.mean(h2 * h2, axis=-1, keepdims=True)
        o_ref[...] = h2 * lax.rsqrt(ms + NORM_EPS) * fg_ref[...]


def experts_residual_norm(t, cw, h2d, m5, final_g, w_gate, w_up, w_down, rows_per_mod, tm=1024):
    r, d = h2d.shape
    ne, _, f = w_gate.shape
    tiles_per_mod = rows_per_mod // tm
    return pl.pallas_call(
        _experts_kernel,
        grid=(r // tm, ne),
        in_specs=[pl.BlockSpec((tm, d), lambda i, e: (i, 0)), pl.BlockSpec((tm, LANE), lambda i, e: (i, 0)),
                  pl.BlockSpec((tm, d), lambda i, e: (i, 0)),
                  pl.BlockSpec((1, 1, d), lambda i, e: (i // tiles_per_mod, 0, 0)),
                  pl.BlockSpec((1, d), lambda i, e: (0, 0)),
                  pl.BlockSpec((1, d, f), lambda i, e: (e, 0, 0)), pl.BlockSpec((1, d, f), lambda i, e: (e, 0, 0)),
                  pl.BlockSpec((1, f, d), lambda i, e: (e, 0, 0))],
        out_specs=pl.BlockSpec((tm, d), lambda i, e: (i, 0)),
        out_shape=jax.ShapeDtypeStruct((r, d), F32),
        scratch_shapes=[pltpu.VMEM((tm, d), F32)],
        compiler_params=_cparams("parallel", "arbitrary"),
        name="moe_experts",
    )(t, cw, h2d, m5, final_g.reshape(1, d), w_gate, w_up, w_down)


def _to_col_major(x, rows):
    b, l, c = x.shape
    return x.reshape(b, rows, GRID_W, c).transpose(0, 2, 1, 3).reshape(b, l, c)


def _to_row_major(x, rows):
    b, l, c = x.shape
    return x.reshape(b, GRID_W, rows, c).transpose(0, 2, 1, 3).reshape(b, l, c)


def _pad_rows(a, n):
    return jnp.pad(a, ((0, n - a.shape[0]),) + ((0, 0),) * (a.ndim - 1))


def _gdn_gates(pa, a_log, dt_bias):
    b, l, _ = pa.shape
    ab = pa[..., 4 * C_A:4 * C_A + 4 * H_A]
    a = ab[..., :2 * H_A].reshape(b, l, 2, H_A)
    bt = ab[..., 2 * H_A:].reshape(b, l, 2, H_A)
    g = -jnp.exp(a_log) * jax.nn.softplus(a + dt_bias)
    beta = jax.nn.sigmoid(bt)
    return jnp.concatenate([g, beta], axis=2).transpose(0, 3, 1, 2)


def kernel(x, c, ctx, c_ctx, ada_w, ada_b, norm_mix_g, norm_ffn_g, w_in, gdn_conv, gdn_a_log, gdn_dt_bias,
           gdn_onorm_g, rwkv_mu, rwkv_w0, rwkv_w2, rwkv_a0, rwkv_a2, rwkv_g2, rwkv_k_k, rwkv_k_a, rwkv_r_k,
           rwkv_lnx_g, rwkv_lnx_b, w_o_a, w_o_b, w_out, router_grp, router_grp_b, router_exp, router_exp_b,
           moe_w_gate, moe_w_up, moe_w_down, final_norm_g):
    bsz, seq, d = x.shape
    lc = ctx.shape[1]
    rows = seq // GRID_W
    a_cols = 4 * C_A + 4 * H_A
    b_cols = 3 * C_B + 2 * LORA_W + 2 * LORA_A + LORA_G

    cc = _pad_rows(jnp.concatenate([c, c_ctx[None]], axis=0), 2 * SUBLANE)
    mod = ada_modulation(cc, ada_w[0], ada_b[0])
    m_lat = [mod[:bsz, i * d:(i + 1) * d].reshape(bsz, 1, d) for i in range(6)]
    m_ctx = [mod[bsz:bsz + 1, i * d:(i + 1) * d].reshape(1, 1, d) for i in range(2)]

    w = w_in[0]
    w_a = jnp.pad(w[:, :a_cols], ((0, 0), (0, 4 * C_A + LANE - a_cols))).astype(BF16)
    w_b = w[:, a_cols:a_cols + b_cols].astype(BF16)
    w_g = w[:, a_cols + b_cols:].astype(BF16)

    x2d = x.reshape(bsz * seq, d)
    pa_lat, pg_lat = norm_modulate_project(x2d, norm_mix_g[0], m_lat[0], m_lat[1], [w_a, w_g], seq)
    (pb_lat,) = norm_modulate_project(_to_col_major(x, rows).reshape(bsz * seq, d), norm_mix_g[0],
                                      m_lat[0], m_lat[1], [w_b], seq)
    pa_ctx, pb_ctx = norm_modulate_project(ctx.reshape(bsz * lc, d), norm_mix_g[0], m_ctx[0], m_ctx[1],
                                           [w_a, w_b], bsz * lc)
    pa_lat = pa_lat.reshape(bsz, seq, -1)
    pb_lat = pb_lat.reshape(bsz, seq, -1)
    pa_ctx = pa_ctx.reshape(bsz, lc, -1)
    pb_ctx = pb_ctx.reshape(bsz, lc, -1)

    gates = jnp.concatenate([_gdn_gates(pa_ctx, gdn_a_log[0], gdn_dt_bias[0]),
                             _gdn_gates(pa_lat, gdn_a_log[0], gdn_dt_bias[0])], axis=2)
    nc = (lc + seq) // CHUNK
    gcol = gates.reshape(bsz, H_A, nc, CHUNK, 4)
    grow = gcol.transpose(0, 1, 2, 4, 3)
    oa = gdn_branch(pa_ctx, pa_lat, _pad_rows(gdn_conv[0], SUBLANE), gcol, grow, gdn_onorm_g[0])

    mu = rwkv_mu[0]
    pch = jnp.stack([rwkv_k_k[0], rwkv_k_a[0], rwkv_r_k[0].reshape(C_B), rwkv_lnx_g[0], rwkv_lnx_b[0],
                     mu[:C_B], mu[C_B:2 * C_B], mu[2 * C_B:3 * C_B]])
    plo = _pad_rows(mu[3 * C_B:].reshape(3, LANE), SUBLANE)
    w0a0 = _pad_rows(jnp.concatenate([rwkv_w0[0], rwkv_a0[0]], axis=0), SUBLANE)
    zw = jnp.zeros((LORA_W, C_B), F32)
    w2pad = jnp.stack([jnp.concatenate([rwkv_w2[0, 0], zw]), jnp.concatenate([zw, rwkv_w2[0, 1]])])
    a2pad = jnp.stack([jnp.concatenate([rwkv_a2[0, 0], zw]), jnp.concatenate([zw, rwkv_a2[0, 1]])])
    ob = rwkv_branch(pb_ctx, pb_lat, pch, plo, w0a0, w2pad, a2pad, rwkv_g2[0])
    ob = _to_row_major(ob, rows)

    h1 = merge_residual(x2d, pg_lat, oa.reshape(bsz * seq, C_A), ob.reshape(bsz * seq, C_B), m_lat[2],
                        w_o_a[0].astype(BF16), w_o_b[0].astype(BF16), w_out[0].astype(BF16), seq)

    w_router = jnp.pad(jnp.concatenate([router_exp[0], router_grp[0]], axis=1),
                       ((0, 0), (0, LANE - N_EXPERTS - N_GROUPS)))
    b_router = jnp.pad(jnp.concatenate([router_exp_b[0], router_grp_b[0]]),
                       (0, LANE - N_EXPERTS - N_GROUPS)).reshape(1, LANE)
    t, cw = route(h1, norm_ffn_g[0], m_lat[3], m_lat[4], w_router, b_router, seq)
    out = experts_residual_norm(t, cw, h1, m_lat[5], final_norm_g,
                                moe_w_gate[0].reshape(N_EXPERTS, d, D_EXPERT).astype(BF16),
                                moe_w_up[0].reshape(N_EXPERTS, d, D_EXPERT).astype(BF16),
                                moe_w_down[0].reshape(N_EXPERTS, D_EXPERT, d).astype(BF16), seq)
    return out.reshape(bsz, seq, d)
```

```python
import functools

import jax
import jax.numpy as jnp
from jax import lax
from jax.experimental import pallas as pl
from jax.experimental.pallas import tpu as pltpu

F32 = jnp.float32
BF16 = jnp.bfloat16
HIGHEST = lax.Precision.HIGHEST

GRID_W = 64
H_A = 4
DK_A = 128
C_A = H_A * DK_A
SHORT_CONV = 5
CHUNK = 64
H_B = 8
N_B = 64
C_B = H_B * N_B
LORA_W = 64
LORA_A = 64
LORA_G = 128
N_GROUPS = 4
EXPERTS_PER_GROUP = 8
N_EXPERTS = N_GROUPS * EXPERTS_PER_GROUP
D_EXPERT = 256
NORM_EPS = 1e-6
LNX_EPS = 1e-5 * N_B

LANE = 128
SUBLANE = 8
VMEM_LIMIT = 56 * 1024 * 1024


def _cparams(*sem):
    return pltpu.CompilerParams(dimension_semantics=sem, vmem_limit_bytes=VMEM_LIMIT)


SINGLE = "bf16 operands, one MXU pass, f32 accumulation"
MM = SINGLE


def _operands(a, b, precision):
    if precision is SINGLE:
        return a.astype(BF16), b.astype(BF16), None
    return a, b, precision


def _dot(a, b, precision=None):
    a, b, precision = _operands(a, b, precision)
    return jnp.dot(a, b, preferred_element_type=F32, precision=precision)


def _dot_nt(a, b, precision=None):
    a, b, precision = _operands(a, b, precision)
    return lax.dot_general(a, b, (((1,), (1,)), ((), ())), preferred_element_type=F32, precision=precision)


def _dot_tn(a, b, precision=None):
    a, b, precision = _operands(a, b, precision)
    return lax.dot_general(a, b, (((0,), (0,)), ((), ())), preferred_element_type=F32, precision=precision)


def _split3(x):
    hi = x.astype(BF16)
    r1 = x - hi.astype(F32)
    mid = r1.astype(BF16)
    lo = (r1 - mid.astype(F32)).astype(BF16)
    return hi, mid, lo


def _dot_small_int_lhs(m, x):
    mb = m.astype(BF16)
    hi, mid, lo = _split3(x)
    return _dot(mb, hi) + _dot(mb, mid) + _dot(mb, lo)


def _dot_small_int_rhs(x, m):
    mb = m.astype(BF16)
    hi, mid, lo = _split3(x)
    return _dot(hi, mb) + _dot(mid, mb) + _dot(lo, mb)


def _sigmoid(x):
    return jax.nn.sigmoid(x)


def _silu(x):
    return x * jax.nn.sigmoid(x)


def _softplus(x):
    return jnp.maximum(x, 0.0) + jnp.log1p(jnp.exp(-jnp.abs(x)))


def _ada_kernel(c_ref, w_ref, b_ref, o_ref):
    s = _silu(c_ref[...])
    o_ref[...] = _dot(s, w_ref[...], HIGHEST) + b_ref[...]


def ada_modulation(cc, ada_w, ada_b):
    r, d = cc.shape
    n = ada_w.shape[1]
    tn = 1536
    return pl.pallas_call(
        _ada_kernel,
        grid=(n // tn,),
        in_specs=[pl.BlockSpec((r, d), lambda j: (0, 0)),
                  pl.BlockSpec((d, tn), lambda j: (0, j)),
                  pl.BlockSpec((1, tn), lambda j: (0, j))],
        out_specs=pl.BlockSpec((r, tn), lambda j: (0, j)),
        out_shape=jax.ShapeDtypeStruct((r, n), F32),
        compiler_params=_cparams("arbitrary"),
        name="ada_modulation",
    )(cc, ada_w, ada_b.reshape(1, n))


def _rms_modulate(x, g, shift, scale):
    ms = jnp.mean(x * x, axis=-1, keepdims=True)
    y = x * lax.rsqrt(ms + NORM_EPS) * g
    return y * (1.0 + scale) + shift


def _inproj_kernel(x_ref, g_ref, sh_ref, sc_ref, *rest, n_out):
    w_refs, o_refs = rest[:n_out], rest[n_out:]
    u = _rms_modulate(x_ref[...], g_ref[...], sh_ref[0], sc_ref[0]).astype(BF16)
    for w_ref, o_ref in zip(w_refs, o_refs):
        o_ref[...] = _dot(u, w_ref[...])


def norm_modulate_project(x2d, g, shift, scale, weights, rows_per_mod, tm=256):
    r, d = x2d.shape
    tiles_per_mod = rows_per_mod // tm
    mod_spec = pl.BlockSpec((1, 1, d), lambda i: (i // tiles_per_mod, 0, 0))
    in_specs = [pl.BlockSpec((tm, d), lambda i: (i, 0)),
                pl.BlockSpec((1, d), lambda i: (0, 0)), mod_spec, mod_spec]
    in_specs += [pl.BlockSpec(w.shape, lambda i: (0, 0)) for w in weights]
    return pl.pallas_call(
        functools.partial(_inproj_kernel, n_out=len(weights)),
        grid=(r // tm,),
        in_specs=in_specs,
        out_specs=[pl.BlockSpec((tm, w.shape[1]), lambda i: (i, 0)) for w in weights],
        out_shape=[jax.ShapeDtypeStruct((r, w.shape[1]), F32) for w in weights],
        compiler_params=_cparams("parallel"),
        name="norm_modulate_project",
    )(x2d, g.reshape(1, d), shift, scale, *weights)


def _iota2(shape, dim):
    return lax.broadcasted_iota(jnp.int32, shape, dim)


def _unit_tri_inverse(lm, nil):
    n = lm.shape[0]
    eye = jnp.where(_iota2((n, n), 0) == _iota2((n, n), 1), 1.0, 0.0).astype(F32)
    x = eye - lm
    p = _dot(lm, lm, MM)
    yield
    k = 2
    while 2 * k < nil:
        xp = _dot(jnp.concatenate([x, p], axis=0), p, MM)
        yield
        x = x + xp[:n]
        p = xp[n:]
        k *= 2
    res = x + _dot(x, p, MM)
    yield
    return res


def _interleave(chains):
    results = [None] * len(chains)
    live = list(range(len(chains)))
    while live:
        for i in list(live):
            try:
                next(chains[i])
            except StopIteration as done:
                results[i] = done.value
                live.remove(i)
    return results


def _fill_padded(pad_ref, x, n):
    zeros = jnp.zeros((SUBLANE, LANE), F32)
    pad_ref[0:SUBLANE, :] = zeros
    pad_ref[SUBLANE:SUBLANE + n, :] = x
    pad_ref[SUBLANE + n:2 * SUBLANE + n, :] = zeros


def _pair_masks(c):
    n = 2 * c
    brow, bcol = _iota2((n, n), 0), _iota2((n, n), 1)
    same = jnp.logical_not(jnp.logical_xor(brow >= c, bcol >= c))
    tr, tc = jnp.where(brow >= c, brow - c, brow), jnp.where(bcol >= c, bcol - c, bcol)
    fwd = brow < c
    incl = same & ((fwd & (tr >= tc)) | (jnp.logical_not(fwd) & (tr <= tc)))
    strict = same & ((fwd & (tr > tc)) | (jnp.logical_not(fwd) & (tr < tc)))
    return incl, strict


def _gdn_chunk_local(q, k, v, g4, r4):
    c = q.shape[0]
    n = 2 * c
    row, col = _iota2((c, c), 0), _iota2((c, c), 1)
    gc2 = jnp.concatenate(
        [jnp.sum(jnp.where(row >= col, r4[0:1, :], 0.0), axis=1, keepdims=True),
         jnp.sum(jnp.where(row <= col, r4[1:2, :], 0.0), axis=1, keepdims=True)], axis=0)
    tot_f = jnp.sum(r4[0:1, :], axis=1, keepdims=True)
    tot_b = jnp.sum(r4[1:2, :], axis=1, keepdims=True)
    tot2 = jnp.concatenate([jnp.broadcast_to(tot_f, (c, 1)), jnp.broadcast_to(tot_b, (c, 1))], axis=0)
    ri, cj = _iota2((c, n), 0), _iota2((c, n), 1)
    lane_f = cj < c
    cjm = jnp.where(lane_f, cj, cj - c)
    keep = (lane_f & (ri <= cjm)) | (jnp.logical_not(lane_f) & (ri >= cjm))
    gc_row2 = jnp.sum(jnp.where(keep, jnp.where(lane_f, g4[:, 0:1], g4[:, 1:2]), 0.0), axis=0, keepdims=True)
    incl, strict = _pair_masks(c)
    decay = jnp.where(incl, jnp.exp(jnp.where(incl, gc2 - gc_row2, 0.0)), 0.0)
    beta2 = jnp.concatenate([g4[:, 2:3], g4[:, 3:4]], axis=0)
    kk2 = jnp.concatenate([k, k], axis=0)
    qq2 = jnp.concatenate([q, q], axis=0)
    kb2 = kk2 * beta2
    m = _dot_nt(jnp.concatenate([kb2, qq2], axis=0), kk2, MM)
    yield
    t = yield from _unit_tri_inverse(jnp.where(strict, m[:n] * decay, 0.0), c)
    egc2 = jnp.exp(gc2)
    sol = _dot(t, jnp.concatenate([jnp.concatenate([v, v], axis=0) * beta2, kb2 * egc2], axis=1), MM)
    yield
    qk2 = jnp.where(incl, m[n:] * decay, 0.0)
    kd2 = kk2 * jnp.exp(tot2 - gc2)
    return sol[:, :LANE], sol[:, LANE:], qq2 * egc2, kd2.T, qk2, jnp.exp(tot_f), jnp.exp(tot_b)


def _gdn_kernel(qc_ref, kc_ref, vc_ref, ql_ref, kl_ref, vl_ref, zl_ref, cwq_ref, cwk_ref, cwv_ref,
                gcol_ref, grow_ref, og_ref, o_ref, q_s, k_s, v_s, oacc_s, pad_s, st_s,
                u_s, wq_s, kdt_s, qk_s, ge_s, *, lc, ll):
    tile = min(256, lc, ll)

    def prep(x_ref, cw_ref, dst, off, n, mode):
        _fill_padded(pad_s, x_ref[0], n)
        cw = cw_ref[...]
        for t0 in range(0, n, tile):
            acc = cw[0:1, :] * pad_s[pl.ds(SUBLANE - 2 + t0, tile), :]
            for j in range(1, SHORT_CONV):
                acc = acc + cw[j:j + 1, :] * pad_s[pl.ds(SUBLANE - 2 + j + t0, tile), :]
            y = _silu(acc)
            if mode != "v":
                y = y * lax.rsqrt(jnp.sum(y * y, axis=-1, keepdims=True) + NORM_EPS)
            if mode == "q":
                y = y * (DK_A ** -0.5)
            dst[pl.ds(off + t0, tile), :] = y

    prep(qc_ref, cwq_ref, q_s, 0, lc, "q")
    prep(kc_ref, cwk_ref, k_s, 0, lc, "k")
    prep(vc_ref, cwv_ref, v_s, 0, lc, "v")
    prep(ql_ref, cwq_ref, q_s, lc, ll, "q")
    prep(kl_ref, cwk_ref, k_s, lc, ll, "k")
    prep(vl_ref, cwv_ref, v_s, lc, ll, "v")

    oacc_s[...] = jnp.zeros(oacc_s.shape, F32)
    st_s[...] = jnp.zeros(st_s.shape, F32)
    nc_ctx = lc // CHUNK
    nc_tot = (lc + ll) // CHUNK

    c = CHUNK

    width = max(w for w in (4, 3, 2, 1) if nc_tot % w == 0)

    def local(it, carry):
        cis = [it * width + j for j in range(width)]
        chains = []
        for ci in cis:
            rows = pl.ds(pl.multiple_of(ci * c, c), c)
            chains.append(_gdn_chunk_local(q_s[rows, :], k_s[rows, :], v_s[rows, :],
                                           gcol_ref[0, 0, ci], grow_ref[0, 0, ci]))
        for ci, (u2, w2, qd2, kdt2, qk2, ge_f, ge_b) in zip(cis, _interleave(chains)):
            u_s[ci] = u2
            wq_s[ci, 0] = jnp.concatenate([w2[:c], qd2[:c]], axis=0).astype(BF16)
            wq_s[ci, 1] = jnp.concatenate([w2[c:], qd2[c:]], axis=0).astype(BF16)
            kdt_s[ci] = kdt2.astype(BF16)
            qk_s[ci] = qk2.astype(BF16)
            ge_s[ci, 0:1, :] = jnp.broadcast_to(ge_f, (1, LANE))
            ge_s[ci, 1:2, :] = jnp.broadcast_to(ge_b, (1, LANE))
        return carry

    lax.fori_loop(0, nc_tot // width, local, 0)

    fwd_rows = _iota2((2 * c, LANE), 0) < c

    def step(s, carry):
        cb = jnp.where(s < nc_ctx, nc_ctx - 1 - s, nc_tot - 1 - (s - nc_ctx))
        s_f, s_b = st_s[0], st_s[1]
        p_f = _dot(wq_s[s, 0], s_f.astype(BF16))
        p_b = _dot(wq_s[cb, 1], s_b.astype(BF16))
        v_new = jnp.concatenate([u_s[s, 0:c, :] - p_f[:c], u_s[cb, c:2 * c, :] - p_b[:c]], axis=0)
        qk = jnp.concatenate([qk_s[s, 0:c, :], qk_s[cb, c:2 * c, :]], axis=0)
        o2 = jnp.concatenate([p_f[c:], p_b[c:]], axis=0) + _dot(qk, v_new.astype(BF16))
        st_s[0] = s_f * ge_s[s, 0:1, :] + _dot(kdt_s[s], jnp.where(fwd_rows, v_new, 0.0).astype(BF16))
        st_s[1] = s_b * ge_s[cb, 1:2, :] + _dot(kdt_s[cb], jnp.where(fwd_rows, 0.0, v_new).astype(BF16))
        rows_f = pl.ds(pl.multiple_of(s * c, c), c)
        rows_b = pl.ds(pl.multiple_of(cb * c, c), c)
        oacc_s[rows_f, :] = oacc_s[rows_f, :] + o2[:c]
        oacc_s[rows_b, :] = oacc_s[rows_b, :] + o2[c:]
        return carry

    lax.fori_loop(0, nc_tot, step, 0)

    og = og_ref[...]
    for t0 in range(0, ll, tile):
        o = oacc_s[pl.ds(lc + t0, tile), :]
        o = o * lax.rsqrt(jnp.mean(o * o, axis=-1, keepdims=True) + NORM_EPS) * og
        o_ref[0, pl.ds(t0, tile), :] = o * _silu(zl_ref[0, pl.ds(t0, tile), :])


def gdn_branch(pa_ctx, pa_lat, conv_w, gcol, grow, onorm_g):
    b, lc, _ = pa_ctx.shape
    ll = pa_lat.shape[1]
    nc = (lc + ll) // CHUNK

    def col(blk, n):
        return pl.BlockSpec((1, n, LANE), lambda i, h: (i, 0, blk * H_A + h))

    def cw(blk):
        return pl.BlockSpec((SUBLANE, LANE), lambda i, h: (0, blk * H_A + h))

    in_specs = [col(0, lc), col(1, lc), col(2, lc), col(0, ll), col(1, ll), col(2, ll), col(3, ll),
                cw(0), cw(1), cw(2),
                pl.BlockSpec((1, 1, nc, CHUNK, 4), lambda i, h: (i, h, 0, 0, 0)),
                pl.BlockSpec((1, 1, nc, 4, CHUNK), lambda i, h: (i, h, 0, 0, 0)),
                pl.BlockSpec((1, LANE), lambda i, h: (0, 0))]
    lt = lc + ll
    return pl.pallas_call(
        functools.partial(_gdn_kernel, lc=lc, ll=ll),
        grid=(b, H_A),
        in_specs=in_specs,
        out_specs=pl.BlockSpec((1, ll, LANE), lambda i, h: (i, 0, h)),
        out_shape=jax.ShapeDtypeStruct((b, ll, C_A), F32),
        scratch_shapes=[pltpu.VMEM((lt, LANE), F32)] * 4
        + [pltpu.VMEM((max(lc, ll) + 2 * SUBLANE, LANE), F32), pltpu.VMEM((2, DK_A, DK_A), F32),
           pltpu.VMEM((nc, 2 * CHUNK, LANE), F32), pltpu.VMEM((nc, 2, 2 * CHUNK, LANE), BF16),
           pltpu.VMEM((nc, DK_A, 2 * CHUNK), BF16), pltpu.VMEM((nc, 2 * CHUNK, 2 * CHUNK), BF16),
           pltpu.VMEM((nc, SUBLANE, LANE), F32)],
        compiler_params=_cparams("parallel", "parallel"),
        name="gdn_branch",
    )(pa_ctx, pa_ctx, pa_ctx, pa_lat, pa_lat, pa_lat, pa_lat, conv_w, conv_w, conv_w, gcol, grow,
      onorm_g.reshape(1, LANE))


def _stack_heads(x, head0):
    return jnp.concatenate([jnp.where(head0, x, 0.0), jnp.where(head0, 0.0, x)], axis=0)


def _rwkv_chunk_local(r, v, a, logw, kdir, b, lower):
    c = r.shape[0]
    n = 2 * c
    row, col = _iota2((c, c), 0), _iota2((c, c), 1)
    tri = jnp.where(row >= col if lower else row <= col, 1.0, 0.0).astype(F32)
    lcum = _dot_small_int_lhs(tri, logw)
    tot = jnp.sum(logw, axis=0, keepdims=True)
    e_in = jnp.exp(lcum)
    e_ex = jnp.exp(lcum - logw)
    e_neg = jnp.exp(-lcum)
    e_rem = jnp.exp(tot - lcum)
    head0 = _iota2((c, LANE), 1) < N_B
    a2 = _stack_heads(a * e_ex, head0)
    r2 = _stack_heads(r * e_in, head0)
    b2 = _stack_heads(b * e_neg, head0)
    k2 = _stack_heads(kdir * e_neg, head0)
    v2 = _stack_heads(v, head0)
    m = _dot_nt(jnp.concatenate([a2, r2], axis=0), jnp.concatenate([b2, k2], axis=0), MM)
    yield
    brow, bcol = _iota2((n, n), 0), _iota2((n, n), 1)
    same = jnp.logical_not(jnp.logical_xor(brow >= c, bcol >= c))
    tr, tc = jnp.where(brow >= c, brow - c, brow), jnp.where(bcol >= c, bcol - c, bcol)
    strict = same & ((tr > tc) if lower else (tr < tc))
    incl = same & ((tr >= tc) if lower else (tr <= tc))
    ak = jnp.where(strict, m[:n, n:], 0.0)
    rb = jnp.where(incl, m[n:, :n], 0.0)
    rk = jnp.where(incl, m[n:, n:], 0.0)
    akv_rkv = _dot(jnp.concatenate([ak, rk], axis=0), v2, MM)
    kv0 = _dot_tn(_stack_heads(kdir * e_rem, head0), v2, MM)
    t = yield from _unit_tri_inverse(jnp.where(strict, -m[:n, :n], 0.0), c)
    tw = _dot(t, jnp.concatenate([a2, akv_rkv[:n]], axis=1), MM)
    yield
    x = jnp.concatenate([tw[:, :LANE], r2], axis=0)
    eg = jnp.broadcast_to(jnp.exp(tot), (LANE, LANE)).T
    return x, rb, _stack_heads(b * e_rem, head0).T, tw[:, LANE:], akv_rkv[n:], kv0, eg


def _rwkv_kernel(*refs, lc, ll):
    ctx_refs, lat_refs = refs[0:6], refs[6:12]
    pch_ref, plo_ref, w0a0_ref, w2_ref, a2_ref, g2_ref, o_ref = refs[12:19]
    r_s, v_s, a_s, lw_s, kd_s, b_s, gate_s, bonus_s, y_s, pad_s, st_s, x_s, rb_s, bt_s, f_s = refs[19:]
    tile = min(256, lc, ll)
    pch = pch_ref[...]
    k_k, k_a, r_k, lnx_g, lnx_b = (pch[i:i + 1, :] for i in range(5))
    mus = [pch[5:6, :], pch[6:7, :], pch[7:8, :]] + [plo_ref[i:i + 1, :] for i in range(3)]
    w0a0 = w0a0_ref[...]
    rr, cc = _iota2((LANE, LANE), 0), _iota2((LANE, LANE), 1)
    seg = jnp.where(jnp.logical_xor(rr >= N_B, cc >= N_B), 0.0, 1.0).astype(F32)

    def prep(src_refs, off, n, is_lat):
        for j in range(6):
            _fill_padded(pad_s.at[j], src_refs[j][0], n)
        for t0 in range(0, n, tile):
            mixed = []
            for j in range(6):
                x = pad_s[j, pl.ds(SUBLANE + t0, tile), :]
                nb = pad_s[j, pl.ds(SUBLANE - 1 + t0, tile), :] + pad_s[j, pl.ds(SUBLANE + 1 + t0, tile), :]
                mixed.append(x + (0.5 * nb - x) * mus[j])
            r, k, v, wl, al, gl = mixed
            wl = jnp.tanh(wl)
            kk = k * k_k
            kk = kk * lax.rsqrt(_dot_small_int_rhs(kk * kk, seg) + NORM_EPS)
            ksum = jnp.zeros_like(k)
            for d in range(2):
                w_log = -_softplus(-(w0a0[d:d + 1, :] + _dot(wl, w2_ref[d], MM))) - 0.5
                iclr = _sigmoid(w0a0[2 + d:3 + d, :] + _dot(al, a2_ref[d], MM))
                kdir = k * (1.0 + (iclr - 1.0) * k_a)
                ksum = ksum + kdir
                lw_s[d, pl.ds(off + t0, tile), :] = -jnp.exp(w_log)
                kd_s[d, pl.ds(off + t0, tile), :] = kdir
                b_s[d, pl.ds(off + t0, tile), :] = kk * iclr
            r_s[pl.ds(off + t0, tile), :] = r
            v_s[pl.ds(off + t0, tile), :] = v
            a_s[pl.ds(off + t0, tile), :] = -kk
            if is_lat:
                gate_s[pl.ds(t0, tile), :] = _dot(_sigmoid(gl), g2_ref[...], MM)
                bonus_s[pl.ds(t0, tile), :] = _dot_small_int_rhs(r * ksum * r_k, seg) * v

    prep(ctx_refs, 0, lc, False)
    prep(lat_refs, lc, ll, True)

    y_s[...] = jnp.zeros(y_s.shape, F32)
    st_s[...] = jnp.zeros(st_s.shape, F32)
    nc_ctx = lc // CHUNK
    nc_tot = (lc + ll) // CHUNK

    group = x_s.shape[0]
    n2 = 2 * CHUNK

    def chunk_of(s, d):
        cb = jnp.where(s < nc_ctx, nc_ctx - 1 - s, nc_tot - 1 - (s - nc_ctx))
        return pl.ds(pl.multiple_of((cb if d else s) * CHUNK, CHUNK), CHUNK)

    width = 2 if group % 2 == 0 else 1

    def run_group(gi, carry):
        def local(it, c2):
            slots = [(it * width + j, d) for j in range(width) for d in range(2)]
            chains = []
            for i, d in slots:
                rows = chunk_of(gi * group + i, d)
                chains.append(_rwkv_chunk_local(r_s[rows, :], v_s[rows, :], a_s[rows, :], lw_s[d, rows, :],
                                                kd_s[d, rows, :], b_s[d, rows, :], lower=(d == 0)))
            for (i, d), (x, rb, bt, u0, rkv, kv0, eg) in zip(slots, _interleave(chains)):
                x_s[i, d] = x.astype(BF16)
                rb_s[i, d] = rb.astype(BF16)
                bt_s[i, d] = bt.astype(BF16)
                f_s[i, d, 0] = u0
                f_s[i, d, 1] = rkv
                f_s[i, d, 2] = kv0
                f_s[i, d, 3] = eg
            return c2

        lax.fori_loop(0, group // width, local, 0)

        def step(i, c2):
            hs = [st_s[d] for d in range(2)]
            ps = [_dot(x_s[i, d], hs[d].astype(BF16)) for d in range(2)]
            ubs = [(ps[d][:n2] + f_s[i, d, 0]).astype(BF16) for d in range(2)]
            for d in range(2):
                st_s[d] = f_s[i, d, 3] * hs[d] + f_s[i, d, 2] + _dot(bt_s[i, d], ubs[d])
            for d in range(2):
                rows = chunk_of(gi * group + i, d)
                y2 = ps[d][n2:] + _dot(rb_s[i, d], ubs[d]) + f_s[i, d, 1]
                y_s[rows, :] = y_s[rows, :] + (y2[:CHUNK] + y2[CHUNK:])
            return c2

        lax.fori_loop(0, group, step, 0)
        return carry

    lax.fori_loop(0, nc_tot // group, run_group, 0)

    inv_n = 1.0 / N_B
    for t0 in range(0, ll, tile):
        yf = y_s[pl.ds(lc + t0, tile), :]
        cen = yf - _dot_small_int_rhs(yf, seg) * inv_n
        var = _dot_small_int_rhs(cen * cen, seg) * inv_n
        y = cen * lax.rsqrt(var + LNX_EPS) * lnx_g + lnx_b
        o_ref[0, pl.ds(t0, tile), :] = (y + bonus_s[pl.ds(t0, tile), :]) * gate_s[pl.ds(t0, tile), :]


def rwkv_branch(pb_ctx, pb_lat, pch, plo, w0a0, w2pad, a2pad, g2):
    b, lc, _ = pb_ctx.shape
    ll = pb_lat.shape[1]
    lt = lc + ll
    pairs = C_B // LANE

    def col(blk, n, per_pair):
        if per_pair:
            return pl.BlockSpec((1, n, LANE), lambda i, p: (i, 0, blk * pairs + p))
        return pl.BlockSpec((1, n, LANE), lambda i, p: (i, 0, 3 * pairs + blk))

    def cols(n):
        return [col(0, n, True), col(1, n, True), col(2, n, True), col(0, n, False), col(1, n, False),
                col(2, n, False)]

    in_specs = cols(lc) + cols(ll) + [
        pl.BlockSpec((SUBLANE, LANE), lambda i, p: (0, p)),
        pl.BlockSpec((SUBLANE, LANE), lambda i, p: (0, 0)),
        pl.BlockSpec((SUBLANE, LANE), lambda i, p: (0, p)),
        pl.BlockSpec((2, LANE, LANE), lambda i, p: (0, 0, p)),
        pl.BlockSpec((2, LANE, LANE), lambda i, p: (0, 0, p)),
        pl.BlockSpec((LANE, LANE), lambda i, p: (0, p))]
    seq = pltpu.VMEM((lt, LANE), F32)
    seq2 = pltpu.VMEM((2, lt, LANE), F32)
    lat = pltpu.VMEM((ll, LANE), F32)
    nc = lt // CHUNK
    group = max(g for g in (6, 4, 3, 2, 1) if nc % g == 0)
    n2 = 2 * CHUNK
    return pl.pallas_call(
        functools.partial(_rwkv_kernel, lc=lc, ll=ll),
        grid=(b, pairs),
        in_specs=in_specs,
        out_specs=pl.BlockSpec((1, ll, LANE), lambda i, p: (i, 0, p)),
        out_shape=jax.ShapeDtypeStruct((b, ll, C_B), F32),
        scratch_shapes=[seq, seq, seq, seq2, seq2, seq2, lat, lat, seq,
                        pltpu.VMEM((6, max(lc, ll) + 2 * SUBLANE, LANE), F32),
                        pltpu.VMEM((2, LANE, LANE), F32),
                        pltpu.VMEM((group, 2, 2 * n2, LANE), BF16), pltpu.VMEM((group, 2, n2, n2), BF16),
                        pltpu.VMEM((group, 2, LANE, n2), BF16), pltpu.VMEM((group, 2, 4, n2, LANE), F32)],
        compiler_params=_cparams("parallel", "parallel"),
        name="rwkv_branch",
    )(*([pb_ctx] * 6), *([pb_lat] * 6), pch, plo, w0a0, w2pad, a2pad, g2)


def _merge_kernel(x_ref, pg_ref, oa_ref, ob_ref, m2_ref, woa_ref, wob_ref, wout_ref, o_ref):
    d = x_ref.shape[1]
    ya = _dot(oa_ref[...].astype(BF16), woa_ref[...])
    yb = _dot(ob_ref[...].astype(BF16), wob_ref[...])
    y = _sigmoid(pg_ref[:, 0:d]) * ya + _sigmoid(pg_ref[:, d:2 * d]) * yb
    o_ref[...] = x_ref[...] + m2_ref[0] * _dot(y.astype(BF16), wout_ref[...])


def merge_residual(x2d, pg, oa, ob, m2, w_o_a, w_o_b, w_out, rows_per_mod, tm=512):
    r, d = x2d.shape

    def rows(n):
        return pl.BlockSpec((tm, n), lambda i: (i, 0))

    def full(w):
        return pl.BlockSpec(w.shape, lambda i: (0, 0))

    tiles_per_mod = rows_per_mod // tm
    return pl.pallas_call(
        _merge_kernel,
        grid=(r // tm,),
        in_specs=[rows(d), rows(2 * d), rows(oa.shape[1]), rows(ob.shape[1]),
                  pl.BlockSpec((1, 1, d), lambda i: (i // tiles_per_mod, 0, 0)),
                  full(w_o_a), full(w_o_b), full(w_out)],
        out_specs=rows(d),
        out_shape=jax.ShapeDtypeStruct((r, d), F32),
        compiler_params=_cparams("parallel"),
        name="merge_residual",
    )(x2d, pg, oa, ob, m2, w_o_a, w_o_b, w_out)


ROUTER_GROUP_LANE0 = N_EXPERTS


def _route_kernel(h_ref, g_ref, sh_ref, sc_ref, wr_ref, br_ref, t_ref, cw_ref):
    t = _rms_modulate(h_ref[...], g_ref[...], sh_ref[0], sc_ref[0])
    t_ref[...] = t.astype(BF16)
    lg = _dot(t, wr_ref[...], HIGHEST) + br_ref[...]
    lane = _iota2(lg.shape, 1)
    lane_f = lane.astype(F32)
    neg = jnp.float32(-jnp.inf)
    big = jnp.float32(2 * LANE)
    is_grp = (lane >= ROUTER_GROUP_LANE0) & (lane < ROUTER_GROUP_LANE0 + N_GROUPS)
    lgg = jnp.where(is_grp, lg, neg)
    mg = jnp.max(lgg, axis=-1, keepdims=True)
    p_grp = 1.0 / jnp.sum(jnp.where(is_grp, jnp.exp(lgg - mg), 0.0), axis=-1, keepdims=True)
    g_sel = jnp.min(jnp.where(lgg == mg, lane_f, big), axis=-1, keepdims=True) - ROUTER_GROUP_LANE0
    grp_of_lane = lax.shift_right_logical(lane, EXPERTS_PER_GROUP.bit_length() - 1).astype(F32)
    in_grp = (lane < N_EXPERTS) & (grp_of_lane == g_sel)
    l1 = jnp.where(in_grp, lg, neg)
    top1 = jnp.max(l1, axis=-1, keepdims=True)
    idx1 = jnp.min(jnp.where(l1 == top1, lane_f, big), axis=-1, keepdims=True)
    l2 = jnp.where(in_grp & (lane_f != idx1), lg, neg)
    top2 = jnp.max(l2, axis=-1, keepdims=True)
    idx2 = jnp.min(jnp.where(l2 == top2, lane_f, big), axis=-1, keepdims=True)
    e2 = jnp.exp(top2 - top1)
    w1 = p_grp / (1.0 + e2)
    cw_ref[...] = jnp.where(lane_f == idx1, w1, jnp.where(lane_f == idx2, w1 * e2, 0.0))


def route(h2d, g, shift, scale, w_router, b_router, rows_per_mod, tm=512):
    r, d = h2d.shape
    tiles_per_mod = rows_per_mod // tm
    mod_spec = pl.BlockSpec((1, 1, d), lambda i: (i // tiles_per_mod, 0, 0))
    return pl.pallas_call(
        _route_kernel,
        grid=(r // tm,),
        in_specs=[pl.BlockSpec((tm, d), lambda i: (i, 0)), pl.BlockSpec((1, d), lambda i: (0, 0)),
                  mod_spec, mod_spec,
                  pl.BlockSpec((d, LANE), lambda i: (0, 0)), pl.BlockSpec((1, LANE), lambda i: (0, 0))],
        out_specs=[pl.BlockSpec((tm, d), lambda i: (i, 0)), pl.BlockSpec((tm, LANE), lambda i: (i, 0))],
        out_shape=[jax.ShapeDtypeStruct((r, d), BF16), jax.ShapeDtypeStruct((r, LANE), F32)],
        compiler_params=_cparams("parallel"),
        name="moe_route",
    )(h2d, g.reshape(1, d), shift, scale, w_router, b_router)


def _experts_kernel(t_ref, cw_ref, h_ref, m5_ref, fg_ref, wg_ref, wu_ref, wd_ref, o_ref, acc_ref):
    e = pl.program_id(1)

    @pl.when(e == 0)
    def _():
        acc_ref[...] = jnp.zeros(acc_ref.shape, F32)

    t = t_ref[...]
    cw = cw_ref[...]
    w = jnp.sum(jnp.where(_iota2(cw.shape, 1) == e, cw, 0.0), axis=-1, keepdims=True)
    hid = _silu(_dot(t, wg_ref[0])) * _dot(t, wu_ref[0]) * w
    acc_ref[...] += _dot(hid.astype(BF16), wd_ref[0])

    @pl.when(e == pl.num_programs(1) - 1)
    def _():
        h2 = h_ref[...] + m5_ref[0] * acc_ref[...]
        ms = jnp.mean(h2 * h2, axis=-1, keepdims=True)
        o_ref[...] = h2 * lax.rsqrt(ms + NORM_EPS) * fg_ref[...]


def experts_residual_norm(t, cw, h2d, m5, final_g, w_gate, w_up, w_down, rows_per_mod, tm=1024):
    r, d = h2d.shape
    ne, _, f = w_gate.shape
    tiles_per_mod = rows_per_mod // tm
    return pl.pallas_call(
        _experts_kernel,
        grid=(r // tm, ne),
        in_specs=[pl.BlockSpec((tm, d), lambda i, e: (i, 0)), pl.BlockSpec((tm, LANE), lambda i, e: (i, 0)),
                  pl.BlockSpec((tm, d), lambda i, e: (i, 0)),
                  pl.BlockSpec((1, 1, d), lambda i, e: (i // tiles_per_mod, 0, 0)),
                  pl.BlockSpec((1, d), lambda i, e: (0, 0)),
                  pl.BlockSpec((1, d, f), lambda i, e: (e, 0, 0)), pl.BlockSpec((1, d, f), lambda i, e: (e, 0, 0)),
                  pl.BlockSpec((1, f, d), lambda i, e: (e, 0, 0))],
        out_specs=pl.BlockSpec((tm, d), lambda i, e: (i, 0)),
        out_shape=jax.ShapeDtypeStruct((r, d), F32),
        scratch_shapes=[pltpu.VMEM((tm, d), F32)],
        compiler_params=_cparams("parallel", "arbitrary"),
        name="moe_experts",
    )(t, cw, h2d, m5, final_g.reshape(1, d), w_gate, w_up, w_down)


def _to_col_major(x, rows):
    b, l, c = x.shape
    return x.reshape(b, rows, GRID_W, c).transpose(0, 2, 1, 3).reshape(b, l, c)


def _to_row_major(x, rows):
    b, l, c = x.shape
    return x.reshape(b, GRID_W, rows, c).transpose(0, 2, 1, 3).reshape(b, l, c)


def _pad_rows(a, n):
    return jnp.pad(a, ((0, n - a.shape[0]),) + ((0, 0),) * (a.ndim - 1))


def _gdn_gates(pa, a_log, dt_bias):
    b, l, _ = pa.shape
    ab = pa[..., 4 * C_A:4 * C_A + 4 * H_A]
    a = ab[..., :2 * H_A].reshape(b, l, 2, H_A)
    bt = ab[..., 2 * H_A:].reshape(b, l, 2, H_A)
    g = -jnp.exp(a_log) * jax.nn.softplus(a + dt_bias)
    beta = jax.nn.sigmoid(bt)
    return jnp.concatenate([g, beta], axis=2).transpose(0, 3, 1, 2)


def kernel(x, c, ctx, c_ctx, ada_w, ada_b, norm_mix_g, norm_ffn_g, w_in, gdn_conv, gdn_a_log, gdn_dt_bias,
           gdn_onorm_g, rwkv_mu, rwkv_w0, rwkv_w2, rwkv_a0, rwkv_a2, rwkv_g2, rwkv_k_k, rwkv_k_a, rwkv_r_k,
           rwkv_lnx_g, rwkv_lnx_b, w_o_a, w_o_b, w_out, router_grp, router_grp_b, router_exp, router_exp_b,
           moe_w_gate, moe_w_up, moe_w_down, final_norm_g):
    bsz, seq, d = x.shape
    lc = ctx.shape[1]
    rows = seq // GRID_W
    a_cols = 4 * C_A + 4 * H_A
    b_cols = 3 * C_B + 2 * LORA_W + 2 * LORA_A + LORA_G

    cc = _pad_rows(jnp.concatenate([c, c_ctx[None]], axis=0), 2 * SUBLANE)
    mod = ada_modulation(cc, ada_w[0], ada_b[0])
    m_lat = [mod[:bsz, i * d:(i + 1) * d].reshape(bsz, 1, d) for i in range(6)]
    m_ctx = [mod[bsz:bsz + 1, i * d:(i + 1) * d].reshape(1, 1, d) for i in range(2)]

    w = w_in[0]
    w_a = jnp.pad(w[:, :a_cols], ((0, 0), (0, 4 * C_A + LANE - a_cols))).astype(BF16)
    w_b = w[:, a_cols:a_cols + b_cols].astype(BF16)
    w_g = w[:, a_cols + b_cols:].astype(BF16)

    x2d = x.reshape(bsz * seq, d)
    pa_lat, pg_lat = norm_modulate_project(x2d, norm_mix_g[0], m_lat[0], m_lat[1], [w_a, w_g], seq)
    (pb_lat,) = norm_modulate_project(_to_col_major(x, rows).reshape(bsz * seq, d), norm_mix_g[0],
                                      m_lat[0], m_lat[1], [w_b], seq)
    pa_ctx, pb_ctx = norm_modulate_project(ctx.reshape(bsz * lc, d), norm_mix_g[0], m_ctx[0], m_ctx[1],
                                           [w_a, w_b], bsz * lc)
    pa_lat = pa_lat.reshape(bsz, seq, -1)
    pb_lat = pb_lat.reshape(bsz, seq, -1)
    pa_ctx = pa_ctx.reshape(bsz, lc, -1)
    pb_ctx = pb_ctx.reshape(bsz, lc, -1)

    gates = jnp.concatenate([_gdn_gates(pa_ctx, gdn_a_log[0], gdn_dt_bias[0]),
                             _gdn_gates(pa_lat, gdn_a_log[0], gdn_dt_bias[0])], axis=2)
    nc = (lc + seq) // CHUNK
    gcol = gates.reshape(bsz, H_A, nc, CHUNK, 4)
    grow = gcol.transpose(0, 1, 2, 4, 3)
    oa = gdn_branch(pa_ctx, pa_lat, _pad_rows(gdn_conv[0], SUBLANE), gcol, grow, gdn_onorm_g[0])

    mu = rwkv_mu[0]
    pch = jnp.stack([rwkv_k_k[0], rwkv_k_a[0], rwkv_r_k[0].reshape(C_B), rwkv_lnx_g[0], rwkv_lnx_b[0],
                     mu[:C_B], mu[C_B:2 * C_B], mu[2 * C_B:3 * C_B]])
    plo = _pad_rows(mu[3 * C_B:].reshape(3, LANE), SUBLANE)
    w0a0 = _pad_rows(jnp.concatenate([rwkv_w0[0], rwkv_a0[0]], axis=0), SUBLANE)
    zw = jnp.zeros((LORA_W, C_B), F32)
    w2pad = jnp.stack([jnp.concatenate([rwkv_w2[0, 0], zw]), jnp.concatenate([zw, rwkv_w2[0, 1]])])
    a2pad = jnp.stack([jnp.concatenate([rwkv_a2[0, 0], zw]), jnp.concatenate([zw, rwkv_a2[0, 1]])])
    ob = rwkv_branch(pb_ctx, pb_lat, pch, plo, w0a0, w2pad, a2pad, rwkv_g2[0])
    ob = _to_row_major(ob, rows)

    h1 = merge_residual(x2d, pg_lat, oa.reshape(bsz * seq, C_A), ob.reshape(bsz * seq, C_B), m_lat[2],
                        w_o_a[0].astype(BF16), w_o_b[0].astype(BF16), w_out[0].astype(BF16), seq)

    w_router = jnp.pad(jnp.concatenate([router_exp[0], router_grp[0]], axis=1),
                       ((0, 0), (0, LANE - N_EXPERTS - N_GROUPS)))
    b_router = jnp.pad(jnp.concatenate([router_exp_b[0], router_grp_b[0]]),
                       (0, LANE - N_EXPERTS - N_GROUPS)).reshape(1, LANE)
    t, cw = route(h1, norm_ffn_g[0], m_lat[3], m_lat[4], w_router, b_router, seq)
    out = experts_residual_norm(t, cw, h1, m_lat[5], final_norm_g,
                                moe_w_gate[0].reshape(N_EXPERTS, d, D_EXPERT).astype(BF16),
                                moe_w_up[0].reshape(N_EXPERTS, d, D_EXPERT).astype(BF16),
                                moe_w_down[0].reshape(N_EXPERTS, D_EXPERT, d).astype(BF16), seq)
    return out.reshape(bsz, seq, d)
```

```python
import functools

import jax
import jax.numpy as jnp
from jax import lax
from jax.experimental import pallas as pl
from jax.experimental.pallas import tpu as pltpu

F32 = jnp.float32
BF16 = jnp.bfloat16
HIGHEST = lax.Precision.HIGHEST

GRID_W = 64
H_A = 4
DK_A = 128
C_A = H_A * DK_A
SHORT_CONV = 5
CHUNK = 64
H_B = 8
N_B = 64
C_B = H_B * N_B
LORA_W = 64
LORA_A = 64
LORA_G = 128
N_GROUPS = 4
EXPERTS_PER_GROUP = 8
N_EXPERTS = N_GROUPS * EXPERTS_PER_GROUP
D_EXPERT = 256
NORM_EPS = 1e-6
LNX_EPS = 1e-5 * N_B

LANE = 128
SUBLANE = 8
VMEM_LIMIT = 56 * 1024 * 1024


def _cparams(*sem):
    return pltpu.CompilerParams(dimension_semantics=sem, vmem_limit_bytes=VMEM_LIMIT)


SINGLE = "bf16 operands, one MXU pass, f32 accumulation"
MM = SINGLE


def _operands(a, b, precision):
    if precision is SINGLE:
        return a.astype(BF16), b.astype(BF16), None
    return a, b, precision


def _dot(a, b, precision=None):
    a, b, precision = _operands(a, b, precision)
    return jnp.dot(a, b, preferred_element_type=F32, precision=precision)


def _dot_nt(a, b, precision=None):
    a, b, precision = _operands(a, b, precision)
    return lax.dot_general(a, b, (((1,), (1,)), ((), ())), preferred_element_type=F32, precision=precision)


def _dot_tn(a, b, precision=None):
    a, b, precision = _operands(a, b, precision)
    return lax.dot_general(a, b, (((0,), (0,)), ((), ())), preferred_element_type=F32, precision=precision)


def _split3(x):
    hi = x.astype(BF16)
    r1 = x - hi.astype(F32)
    mid = r1.astype(BF16)
    lo = (r1 - mid.astype(F32)).astype(BF16)
    return hi, mid, lo


def _dot_small_int_lhs(m, x):
    mb = m.astype(BF16)
    hi, mid, lo = _split3(x)
    return _dot(mb, hi) + _dot(mb, mid) + _dot(mb, lo)


def _dot_small_int_rhs(x, m):
    mb = m.astype(BF16)
    hi, mid, lo = _split3(x)
    return _dot(hi, mb) + _dot(mid, mb) + _dot(lo, mb)


def _sigmoid(x):
    return jax.nn.sigmoid(x)


def _silu(x):
    return x * jax.nn.sigmoid(x)


def _softplus(x):
    return jnp.maximum(x, 0.0) + jnp.log1p(jnp.exp(-jnp.abs(x)))


def _ada_kernel(c_ref, w_ref, b_ref, o_ref):
    s = _silu(c_ref[...])
    o_ref[...] = _dot(s, w_ref[...], HIGHEST) + b_ref[...]


def ada_modulation(cc, ada_w, ada_b):
    r, d = cc.shape
    n = ada_w.shape[1]
    tn = 1536
    return pl.pallas_call(
        _ada_kernel,
        grid=(n // tn,),
        in_specs=[pl.BlockSpec((r, d), lambda j: (0, 0)),
                  pl.BlockSpec((d, tn), lambda j: (0, j)),
                  pl.BlockSpec((1, tn), lambda j: (0, j))],
        out_specs=pl.BlockSpec((r, tn), lambda j: (0, j)),
        out_shape=jax.ShapeDtypeStruct((r, n), F32),
        compiler_params=_cparams("arbitrary"),
        name="ada_modulation",
    )(cc, ada_w, ada_b.reshape(1, n))


def _rms_modulate(x, g, shift, scale):
    ms = jnp.mean(x * x, axis=-1, keepdims=True)
    y = x * lax.rsqrt(ms + NORM_EPS) * g
    return y * (1.0 + scale) + shift


def _inproj_kernel(x_ref, g_ref, sh_ref, sc_ref, *rest, n_out):
    w_refs, o_refs = rest[:n_out], rest[n_out:]
    u = _rms_modulate(x_ref[...], g_ref[...], sh_ref[0], sc_ref[0]).astype(BF16)
    for w_ref, o_ref in zip(w_refs, o_refs):
        o_ref[...] = _dot(u, w_ref[...])


def norm_modulate_project(x2d, g, shift, scale, weights, rows_per_mod, tm=256):
    r, d = x2d.shape
    tiles_per_mod = rows_per_mod // tm
    mod_spec = pl.BlockSpec((1, 1, d), lambda i: (i // tiles_per_mod, 0, 0))
    in_specs = [pl.BlockSpec((tm, d), lambda i: (i, 0)),
                pl.BlockSpec((1, d), lambda i: (0, 0)), mod_spec, mod_spec]
    in_specs += [pl.BlockSpec(w.shape, lambda i: (0, 0)) for w in weights]
    return pl.pallas_call(
        functools.partial(_inproj_kernel, n_out=len(weights)),
        grid=(r // tm,),
        in_specs=in_specs,
        out_specs=[pl.BlockSpec((tm, w.shape[1]), lambda i: (i, 0)) for w in weights],
        out_shape=[jax.ShapeDtypeStruct((r, w.shape[1]), F32) for w in weights],
        compiler_params=_cparams("parallel"),
        name="norm_modulate_project",
    )(x2d, g.reshape(1, d), shift, scale, *weights)


def _iota2(shape, dim):
    return lax.broadcasted_iota(jnp.int32, shape, dim)


def _unit_tri_inverse(lm, nil):
    n = lm.shape[0]
    eye = jnp.where(_iota2((n, n), 0) == _iota2((n, n), 1), 1.0, 0.0).astype(F32)
    x = eye - lm
    p = _dot(lm, lm, MM)
    yield
    k = 2
    while 2 * k < nil:
        xp = _dot(jnp.concatenate([x, p], axis=0), p, MM)
        yield
        x = x + xp[:n]
        p = xp[n:]
        k *= 2
    res = x + _dot(x, p, MM)
    yield
    return res


def _interleave(chains):
    results = [None] * len(chains)
    live = list(range(len(chains)))
    while live:
        for i in list(live):
            try:
                next(chains[i])
            except StopIteration as done:
                results[i] = done.value
                live.remove(i)
    return results


def _fill_padded(pad_ref, x, n):
    zeros = jnp.zeros((SUBLANE, LANE), F32)
    pad_ref[0:SUBLANE, :] = zeros
    pad_ref[SUBLANE:SUBLANE + n, :] = x
    pad_ref[SUBLANE + n:2 * SUBLANE + n, :] = zeros


def _pair_masks(c):
    n = 2 * c
    brow, bcol = _iota2((n, n), 0), _iota2((n, n), 1)
    same = jnp.logical_not(jnp.logical_xor(brow >= c, bcol >= c))
    tr, tc = jnp.where(brow >= c, brow - c, brow), jnp.where(bcol >= c, bcol - c, bcol)
    fwd = brow < c
    incl = same & ((fwd & (tr >= tc)) | (jnp.logical_not(fwd) & (tr <= tc)))
    strict = same & ((fwd & (tr > tc)) | (jnp.logical_not(fwd) & (tr < tc)))
    return incl, strict


def _gdn_chunk_local(q, k, v, g4, r4):
    c = q.shape[0]
    n = 2 * c
    row, col = _iota2((c, c), 0), _iota2((c, c), 1)
    gc2 = jnp.concatenate(
        [jnp.sum(jnp.where(row >= col, r4[0:1, :], 0.0), axis=1, keepdims=True),
         jnp.sum(jnp.where(row <= col, r4[1:2, :], 0.0), axis=1, keepdims=True)], axis=0)
    tot_f = jnp.sum(r4[0:1, :], axis=1, keepdims=True)
    tot_b = jnp.sum(r4[1:2, :], axis=1, keepdims=True)
    tot2 = jnp.concatenate([jnp.broadcast_to(tot_f, (c, 1)), jnp.broadcast_to(tot_b, (c, 1))], axis=0)
    ri, cj = _iota2((c, n), 0), _iota2((c, n), 1)
    lane_f = cj < c
    cjm = jnp.where(lane_f, cj, cj - c)
    keep = (lane_f & (ri <= cjm)) | (jnp.logical_not(lane_f) & (ri >= cjm))
    gc_row2 = jnp.sum(jnp.where(keep, jnp.where(lane_f, g4[:, 0:1], g4[:, 1:2]), 0.0), axis=0, keepdims=True)
    incl, strict = _pair_masks(c)
    decay = jnp.where(incl, jnp.exp(jnp.where(incl, gc2 - gc_row2, 0.0)), 0.0)
    beta2 = jnp.concatenate([g4[:, 2:3], g4[:, 3:4]], axis=0)
    kk2 = jnp.concatenate([k, k], axis=0)
    qq2 = jnp.concatenate([q, q], axis=0)
    kb2 = kk2 * beta2
    m = _dot_nt(jnp.concatenate([kb2, qq2], axis=0), kk2, MM)
    yield
    t = yield from _unit_tri_inverse(jnp.where(strict, m[:n] * decay, 0.0), c)
    egc2 = jnp.exp(gc2)
    sol = _dot(t, jnp.concatenate([jnp.concatenate([v, v], axis=0) * beta2, kb2 * egc2], axis=1), MM)
    yield
    qk2 = jnp.where(incl, m[n:] * decay, 0.0)
    kdt2 = (kk2 * jnp.exp(tot2 - gc2)).T
    fwd_rows = _iota2(sol.shape, 0) < c
    ks_f = _dot(kdt2, jnp.where(fwd_rows, sol, 0.0), MM)
    ks_b = _dot(kdt2, jnp.where(fwd_rows, 0.0, sol), MM)
    yield
    return sol[:, :LANE], sol[:, LANE:], qq2 * egc2, qk2, ks_f, ks_b, jnp.exp(tot_f), jnp.exp(tot_b)


def _gdn_kernel(qc_ref, kc_ref, vc_ref, ql_ref, kl_ref, vl_ref, zl_ref, cwq_ref, cwk_ref, cwv_ref,
                gcol_ref, grow_ref, og_ref, o_ref, q_s, k_s, v_s, oacc_s, pad_s, st_s,
                u_s, wq_s, k0_s, k1_s, qk_s, ge_s, p_s, *, lc, ll):
    tile = min(256, lc, ll)

    def prep(x_ref, cw_ref, dst, off, n, mode):
        _fill_padded(pad_s, x_ref[0], n)
        cw = cw_ref[...]
        for t0 in range(0, n, tile):
            acc = cw[0:1, :] * pad_s[pl.ds(SUBLANE - 2 + t0, tile), :]
            for j in range(1, SHORT_CONV):
                acc = acc + cw[j:j + 1, :] * pad_s[pl.ds(SUBLANE - 2 + j + t0, tile), :]
            y = _silu(acc)
            if mode != "v":
                y = y * lax.rsqrt(jnp.sum(y * y, axis=-1, keepdims=True) + NORM_EPS)
            if mode == "q":
                y = y * (DK_A ** -0.5)
            dst[pl.ds(off + t0, tile), :] = y

    prep(qc_ref, cwq_ref, q_s, 0, lc, "q")
    prep(kc_ref, cwk_ref, k_s, 0, lc, "k")
    prep(vc_ref, cwv_ref, v_s, 0, lc, "v")
    prep(ql_ref, cwq_ref, q_s, lc, ll, "q")
    prep(kl_ref, cwk_ref, k_s, lc, ll, "k")
    prep(vl_ref, cwv_ref, v_s, lc, ll, "v")

    oacc_s[...] = jnp.zeros(oacc_s.shape, F32)
    st_s[...] = jnp.zeros(st_s.shape, F32)
    nc_ctx = lc // CHUNK
    nc_tot = (lc + ll) // CHUNK

    c = CHUNK

    width = max(w for w in (6, 4, 3, 2, 1) if nc_tot % w == 0)

    def local(it, carry):
        cis = [it * width + j for j in range(width)]
        chains = []
        for ci in cis:
            rows = pl.ds(pl.multiple_of(ci * c, c), c)
            chains.append(_gdn_chunk_local(q_s[rows, :], k_s[rows, :], v_s[rows, :],
                                           gcol_ref[0, 0, ci], grow_ref[0, 0, ci]))
        for ci, (u2, w2, qd2, qk2, ks_f, ks_b, ge_f, ge_b) in zip(cis, _interleave(chains)):
            u_s[ci] = u2
            wq_s[ci, 0] = jnp.concatenate([w2[:c], qd2[:c]], axis=0).astype(BF16)
            wq_s[ci, 1] = jnp.concatenate([w2[c:], qd2[c:]], axis=0).astype(BF16)
            for d, ks in enumerate((ks_f, ks_b)):
                k0_s[ci, d] = ks[:, :LANE]
                k1_s[ci, d] = ks[:, LANE:].astype(BF16)
            qk_s[ci] = qk2.astype(BF16)
            ge_s[ci, 0:1, :] = jnp.broadcast_to(ge_f, (1, LANE))
            ge_s[ci, 1:2, :] = jnp.broadcast_to(ge_b, (1, LANE))
        return carry

    lax.fori_loop(0, nc_tot // width, local, 0)

    def bwd_chunk(s):
        return jnp.where(s < nc_ctx, nc_ctx - 1 - s, nc_tot - 1 - (s - nc_ctx))

    def outputs(s, keep):
        cb = bwd_chunk(s)
        p_f, p_b = p_s[0], p_s[1]
        v_new = jnp.concatenate([u_s[s, 0:c, :] - p_f[:c], u_s[cb, c:2 * c, :] - p_b[:c]], axis=0)
        qk = jnp.concatenate([qk_s[s, 0:c, :], qk_s[cb, c:2 * c, :]], axis=0)
        o2 = jnp.concatenate([p_f[c:], p_b[c:]], axis=0) + _dot(qk, v_new.astype(BF16))
        if keep is not None:
            o2 = jnp.where(keep, o2, 0.0)
        rows_f = pl.ds(pl.multiple_of(s * c, c), c)
        rows_b = pl.ds(pl.multiple_of(cb * c, c), c)
        oacc_s[rows_f, :] = oacc_s[rows_f, :] + o2[:c]
        oacc_s[rows_b, :] = oacc_s[rows_b, :] + o2[c:]

    def step(s, carry):
        outputs(jnp.maximum(s - 1, 0), s > 0)
        cb = bwd_chunk(s)
        s_f, s_b = st_s[0], st_s[1]
        sb_f, sb_b = s_f.astype(BF16), s_b.astype(BF16)
        p_s[0] = _dot(wq_s[s, 0], sb_f)
        p_s[1] = _dot(wq_s[cb, 1], sb_b)
        st_s[0] = s_f * ge_s[s, 0:1, :] + k0_s[s, 0] - _dot(k1_s[s, 0], sb_f)
        st_s[1] = s_b * ge_s[cb, 1:2, :] + k0_s[cb, 1] - _dot(k1_s[cb, 1], sb_b)
        return carry

    p_s[...] = jnp.zeros(p_s.shape, F32)
    lax.fori_loop(0, nc_tot, step, 0)
    outputs(nc_tot - 1, None)

    og = og_ref[...]
    for t0 in range(0, ll, tile):
        o = oacc_s[pl.ds(lc + t0, tile), :]
        o = o * lax.rsqrt(jnp.mean(o * o, axis=-1, keepdims=True) + NORM_EPS) * og
        o_ref[0, pl.ds(t0, tile), :] = o * _silu(zl_ref[0, pl.ds(t0, tile), :])


def gdn_branch(pa_ctx, pa_lat, conv_w, gcol, grow, onorm_g):
    b, lc, _ = pa_ctx.shape
    ll = pa_lat.shape[1]
    nc = (lc + ll) // CHUNK

    def col(blk, n):
        return pl.BlockSpec((1, n, LANE), lambda i, h: (i, 0, blk * H_A + h))

    def cw(blk):
        return pl.BlockSpec((SUBLANE, LANE), lambda i, h: (0, blk * H_A + h))

    in_specs = [col(0, lc), col(1, lc), col(2, lc), col(0, ll), col(1, ll), col(2, ll), col(3, ll),
                cw(0), cw(1), cw(2),
                pl.BlockSpec((1, 1, nc, CHUNK, 4), lambda i, h: (i, h, 0, 0, 0)),
                pl.BlockSpec((1, 1, nc, 4, CHUNK), lambda i, h: (i, h, 0, 0, 0)),
                pl.BlockSpec((1, LANE), lambda i, h: (0, 0))]
    lt = lc + ll
    return pl.pallas_call(
        functools.partial(_gdn_kernel, lc=lc, ll=ll),
        grid=(b, H_A),
        in_specs=in_specs,
        out_specs=pl.BlockSpec((1, ll, LANE), lambda i, h: (i, 0, h)),
        out_shape=jax.ShapeDtypeStruct((b, ll, C_A), F32),
        scratch_shapes=[pltpu.VMEM((lt, LANE), F32)] * 4
        + [pltpu.VMEM((max(lc, ll) + 2 * SUBLANE, LANE), F32), pltpu.VMEM((2, DK_A, DK_A), F32),
           pltpu.VMEM((nc, 2 * CHUNK, LANE), F32), pltpu.VMEM((nc, 2, 2 * CHUNK, LANE), BF16),
           pltpu.VMEM((nc, 2, DK_A, DK_A), F32), pltpu.VMEM((nc, 2, DK_A, DK_A), BF16),
           pltpu.VMEM((nc, 2 * CHUNK, 2 * CHUNK), BF16),
           pltpu.VMEM((nc, SUBLANE, LANE), F32), pltpu.VMEM((2, 2 * CHUNK, LANE), F32)],
        compiler_params=_cparams("parallel", "parallel"),
        name="gdn_branch",
    )(pa_ctx, pa_ctx, pa_ctx, pa_lat, pa_lat, pa_lat, pa_lat, conv_w, conv_w, conv_w, gcol, grow,
      onorm_g.reshape(1, LANE))


def _stack_heads(x, head0):
    return jnp.concatenate([jnp.where(head0, x, 0.0), jnp.where(head0, 0.0, x)], axis=0)


def _rwkv_chunk_local(r, v, a, logw, kdir, b, lower):
    c = r.shape[0]
    n = 2 * c
    row, col = _iota2((c, c), 0), _iota2((c, c), 1)
    tri = jnp.where(row >= col if lower else row <= col, 1.0, 0.0).astype(F32)
    lcum = _dot_small_int_lhs(tri, logw)
    tot = jnp.sum(logw, axis=0, keepdims=True)
    e_in = jnp.exp(lcum)
    e_ex = jnp.exp(lcum - logw)
    e_neg = jnp.exp(-lcum)
    e_rem = jnp.exp(tot - lcum)
    head0 = _iota2((c, LANE), 1) < N_B
    a2 = _stack_heads(a * e_ex, head0)
    r2 = _stack_heads(r * e_in, head0)
    b2 = _stack_heads(b * e_neg, head0)
    k2 = _stack_heads(kdir * e_neg, head0)
    v2 = _stack_heads(v, head0)
    m = _dot_nt(jnp.concatenate([a2, r2], axis=0), jnp.concatenate([b2, k2], axis=0), MM)
    yield
    brow, bcol = _iota2((n, n), 0), _iota2((n, n), 1)
    same = jnp.logical_not(jnp.logical_xor(brow >= c, bcol >= c))
    tr, tc = jnp.where(brow >= c, brow - c, brow), jnp.where(bcol >= c, bcol - c, bcol)
    strict = same & ((tr > tc) if lower else (tr < tc))
    incl = same & ((tr >= tc) if lower else (tr <= tc))
    ak = jnp.where(strict, m[:n, n:], 0.0)
    rb = jnp.where(incl, m[n:, :n], 0.0)
    rk = jnp.where(incl, m[n:, n:], 0.0)
    akv_rkv = _dot(jnp.concatenate([ak, rk], axis=0), v2, MM)
    kv0 = _dot_tn(_stack_heads(kdir * e_rem, head0), v2, MM)
    t = yield from _unit_tri_inverse(jnp.where(strict, -m[:n, :n], 0.0), c)
    tw = _dot(t, jnp.concatenate([a2, akv_rkv[:n]], axis=1), MM)
    yield
    x = jnp.concatenate([tw[:, :LANE], r2], axis=0)
    eg = jnp.broadcast_to(jnp.exp(tot), (LANE, LANE)).T
    bu = _dot(_stack_heads(b * e_rem, head0).T, tw, MM)
    yield
    return x, rb, bu[:, :LANE], tw[:, LANE:], akv_rkv[n:], kv0 + bu[:, LANE:], eg


def _rwkv_kernel(*refs, lc, ll):
    ctx_refs, lat_refs = refs[0:6], refs[6:12]
    pch_ref, plo_ref, w0a0_ref, w2_ref, a2_ref, g2_ref, o_ref = refs[12:19]
    r_s, v_s, a_s, lw_s, kd_s, b_s, gate_s, bonus_s, y_s, pad_s, st_s, x_s, rb_s, m1_s, f_s, p_s = refs[19:]
    tile = min(256, lc, ll)
    pch = pch_ref[...]
    k_k, k_a, r_k, lnx_g, lnx_b = (pch[i:i + 1, :] for i in range(5))
    mus = [pch[5:6, :], pch[6:7, :], pch[7:8, :]] + [plo_ref[i:i + 1, :] for i in range(3)]
    w0a0 = w0a0_ref[...]
    rr, cc = _iota2((LANE, LANE), 0), _iota2((LANE, LANE), 1)
    seg = jnp.where(jnp.logical_xor(rr >= N_B, cc >= N_B), 0.0, 1.0).astype(F32)

    def prep(src_refs, off, n, is_lat):
        for j in range(6):
            _fill_padded(pad_s.at[j], src_refs[j][0], n)
        for t0 in range(0, n, tile):
            mixed = []
            for j in range(6):
                x = pad_s[j, pl.ds(SUBLANE + t0, tile), :]
                nb = pad_s[j, pl.ds(SUBLANE - 1 + t0, tile), :] + pad_s[j, pl.ds(SUBLANE + 1 + t0, tile), :]
                mixed.append(x + (0.5 * nb - x) * mus[j])
            r, k, v, wl, al, gl = mixed
            wl = jnp.tanh(wl)
            kk = k * k_k
            kk = kk * lax.rsqrt(_dot_small_int_rhs(kk * kk, seg) + NORM_EPS)
            ksum = jnp.zeros_like(k)
            for d in range(2):
                w_log = -_softplus(-(w0a0[d:d + 1, :] + _dot(wl, w2_ref[d], MM))) - 0.5
                iclr = _sigmoid(w0a0[2 + d:3 + d, :] + _dot(al, a2_ref[d], MM))
                kdir = k * (1.0 + (iclr - 1.0) * k_a)
                ksum = ksum + kdir
                lw_s[d, pl.ds(off + t0, tile), :] = -jnp.exp(w_log)
                kd_s[d, pl.ds(off + t0, tile), :] = kdir
                b_s[d, pl.ds(off + t0, tile), :] = kk * iclr
            r_s[pl.ds(off + t0, tile), :] = r
            v_s[pl.ds(off + t0, tile), :] = v
            a_s[pl.ds(off + t0, tile), :] = -kk
            if is_lat:
                gate_s[pl.ds(t0, tile), :] = _dot(_sigmoid(gl), g2_ref[...], MM)
                bonus_s[pl.ds(t0, tile), :] = _dot_small_int_rhs(r * ksum * r_k, seg) * v

    prep(ctx_refs, 0, lc, False)
    prep(lat_refs, lc, ll, True)

    y_s[...] = jnp.zeros(y_s.shape, F32)
    st_s[...] = jnp.zeros(st_s.shape, F32)
    p_s[...] = jnp.zeros(p_s.shape, F32)
    nc_ctx = lc // CHUNK
    nc_tot = (lc + ll) // CHUNK

    group = x_s.shape[0]
    n2 = 2 * CHUNK

    def chunk_of(s, d):
        cb = jnp.where(s < nc_ctx, nc_ctx - 1 - s, nc_tot - 1 - (s - nc_ctx))
        return pl.ds(pl.multiple_of((cb if d else s) * CHUNK, CHUNK), CHUNK)

    width = max(w for w in (3, 2, 1) if group % w == 0)

    def run_group(gi, carry):
        def local(it, c2):
            slots = [(it * width + j, d) for j in range(width) for d in range(2)]
            chains = []
            for i, d in slots:
                rows = chunk_of(gi * group + i, d)
                chains.append(_rwkv_chunk_local(r_s[rows, :], v_s[rows, :], a_s[rows, :], lw_s[d, rows, :],
                                                kd_s[d, rows, :], b_s[d, rows, :], lower=(d == 0)))
            for (i, d), (x, rb, m1, u0, rkv, hc, eg) in zip(slots, _interleave(chains)):
                x_s[i, d] = x.astype(BF16)
                rb_s[i, d] = rb.astype(BF16)
                m1_s[i, d] = m1.astype(BF16)
                f_s[i, d, 0] = u0
                f_s[i, d, 1] = rkv
                f_s[i, d, 2] = hc
                f_s[i, d, 3] = eg
            return c2

        lax.fori_loop(0, group // width, local, 0)

        def outputs(i, keep):
            for d in range(2):
                rows = chunk_of(gi * group + i, d)
                u = p_s[d, 0:n2, :] + f_s[i, d, 0]
                y2 = p_s[d, n2:2 * n2, :] + _dot(rb_s[i, d], u.astype(BF16)) + f_s[i, d, 1]
                y = y2[:CHUNK] + y2[CHUNK:]
                if keep is not None:
                    y = jnp.where(keep, y, 0.0)
                y_s[rows, :] = y_s[rows, :] + y

        def step(i, c2):
            outputs(jnp.maximum(i - 1, 0), i > 0)
            hs = [st_s[d] for d in range(2)]
            hbs = [h.astype(BF16) for h in hs]
            for d in range(2):
                p_s[d] = _dot(x_s[i, d], hbs[d])
            for d in range(2):
                st_s[d] = f_s[i, d, 3] * hs[d] + f_s[i, d, 2] + _dot(m1_s[i, d], hbs[d])
            return c2

        lax.fori_loop(0, group, step, 0)
        outputs(group - 1, None)
        return carry

    lax.fori_loop(0, nc_tot // group, run_group, 0)

    inv_n = 1.0 / N_B
    for t0 in range(0, ll, tile):
        yf = y_s[pl.ds(lc + t0, tile), :]
        cen = yf - _dot_small_int_rhs(yf, seg) * inv_n
        var = _dot_small_int_rhs(cen * cen, seg) * inv_n
        y = cen * lax.rsqrt(var + LNX_EPS) * lnx_g + lnx_b
        o_ref[0, pl.ds(t0, tile), :] = (y + bonus_s[pl.ds(t0, tile), :]) * gate_s[pl.ds(t0, tile), :]


def rwkv_branch(pb_ctx, pb_lat, pch, plo, w0a0, w2pad, a2pad, g2):
    b, lc, _ = pb_ctx.shape
    ll = pb_lat.shape[1]
    lt = lc + ll
    pairs = C_B // LANE

    def col(blk, n, per_pair):
        if per_pair:
            return pl.BlockSpec((1, n, LANE), lambda i, p: (i, 0, blk * pairs + p))
        return pl.BlockSpec((1, n, LANE), lambda i, p: (i, 0, 3 * pairs + blk))

    def cols(n):
        return [col(0, n, True), col(1, n, True), col(2, n, True), col(0, n, False), col(1, n, False),
                col(2, n, False)]

    in_specs = cols(lc) + cols(ll) + [
        pl.BlockSpec((SUBLANE, LANE), lambda i, p: (0, p)),
        pl.BlockSpec((SUBLANE, LANE), lambda i, p: (0, 0)),
        pl.BlockSpec((SUBLANE, LANE), lambda i, p: (0, p)),
        pl.BlockSpec((2, LANE, LANE), lambda i, p: (0, 0, p)),
        pl.BlockSpec((2, LANE, LANE), lambda i, p: (0, 0, p)),
        pl.BlockSpec((LANE, LANE), lambda i, p: (0, p))]
    seq = pltpu.VMEM((lt, LANE), F32)
    seq2 = pltpu.VMEM((2, lt, LANE), F32)
    lat = pltpu.VMEM((ll, LANE), F32)
    nc = lt // CHUNK
    group = max(g for g in (6, 4, 3, 2, 1) if nc % g == 0)
    n2 = 2 * CHUNK
    return pl.pallas_call(
        functools.partial(_rwkv_kernel, lc=lc, ll=ll),
        grid=(b, pairs),
        in_specs=in_specs,
        out_specs=pl.BlockSpec((1, ll, LANE), lambda i, p: (i, 0, p)),
        out_shape=jax.ShapeDtypeStruct((b, ll, C_B), F32),
        scratch_shapes=[seq, seq, seq, seq2, seq2, seq2, lat, lat, seq,
                        pltpu.VMEM((6, max(lc, ll) + 2 * SUBLANE, LANE), F32),
                        pltpu.VMEM((2, LANE, LANE), F32),
                        pltpu.VMEM((group, 2, 2 * n2, LANE), BF16), pltpu.VMEM((group, 2, n2, n2), BF16),
                        pltpu.VMEM((group, 2, LANE, n2), BF16), pltpu.VMEM((group, 2, 4, n2, LANE), F32),
                        pltpu.VMEM((2, 2 * n2, LANE), F32)],
        compiler_params=_cparams("parallel", "parallel"),
        name="rwkv_branch",
    )(*([pb_ctx] * 6), *([pb_lat] * 6), pch, plo, w0a0, w2pad, a2pad, g2)


def _merge_kernel(x_ref, pg_ref, oa_ref, ob_ref, m2_ref, woa_ref, wob_ref, wout_ref, o_ref):
    d = x_ref.shape[1]
    ya = _dot(oa_ref[...].astype(BF16), woa_ref[...])
    yb = _dot(ob_ref[...].astype(BF16), wob_ref[...])
    y = _sigmoid(pg_ref[:, 0:d]) * ya + _sigmoid(pg_ref[:, d:2 * d]) * yb
    o_ref[...] = x_ref[...] + m2_ref[0] * _dot(y.astype(BF16), wout_ref[...])


def merge_residual(x2d, pg, oa, ob, m2, w_o_a, w_o_b, w_out, rows_per_mod, tm=512):
    r, d = x2d.shape

    def rows(n):
        return pl.BlockSpec((tm, n), lambda i: (i, 0))

    def full(w):
        return pl.BlockSpec(w.shape, lambda i: (0, 0))

    tiles_per_mod = rows_per_mod // tm
    return pl.pallas_call(
        _merge_kernel,
        grid=(r // tm,),
        in_specs=[rows(d), rows(2 * d), rows(oa.shape[1]), rows(ob.shape[1]),
                  pl.BlockSpec((1, 1, d), lambda i: (i // tiles_per_mod, 0, 0)),
                  full(w_o_a), full(w_o_b), full(w_out)],
        out_specs=rows(d),
        out_shape=jax.ShapeDtypeStruct((r, d), F32),
        compiler_params=_cparams("parallel"),
        name="merge_residual",
    )(x2d, pg, oa, ob, m2, w_o_a, w_o_b, w_out)


ROUTER_GROUP_LANE0 = N_EXPERTS


def _route_kernel(h_ref, g_ref, sh_ref, sc_ref, wr_ref, br_ref, t_ref, cw_ref):
    t = _rms_modulate(h_ref[...], g_ref[...], sh_ref[0], sc_ref[0])
    t_ref[...] = t.astype(BF16)
    lg = _dot(t, wr_ref[...], HIGHEST) + br_ref[...]
    lane = _iota2(lg.shape, 1)
    lane_f = lane.astype(F32)
    neg = jnp.float32(-jnp.inf)
    big = jnp.float32(2 * LANE)
    is_grp = (lane >= ROUTER_GROUP_LANE0) & (lane < ROUTER_GROUP_LANE0 + N_GROUPS)
    lgg = jnp.where(is_grp, lg, neg)
    mg = jnp.max(lgg, axis=-1, keepdims=True)
    p_grp = 1.0 / jnp.sum(jnp.where(is_grp, jnp.exp(lgg - mg), 0.0), axis=-1, keepdims=True)
    g_sel = jnp.min(jnp.where(lgg == mg, lane_f, big), axis=-1, keepdims=True) - ROUTER_GROUP_LANE0
    grp_of_lane = lax.shift_right_logical(lane, EXPERTS_PER_GROUP.bit_length() - 1).astype(F32)
    in_grp = (lane < N_EXPERTS) & (grp_of_lane == g_sel)
    l1 = jnp.where(in_grp, lg, neg)
    top1 = jnp.max(l1, axis=-1, keepdims=True)
    idx1 = jnp.min(jnp.where(l1 == top1, lane_f, big), axis=-1, keepdims=True)
    l2 = jnp.where(in_grp & (lane_f != idx1), lg, neg)
    top2 = jnp.max(l2, axis=-1, keepdims=True)
    idx2 = jnp.min(jnp.where(l2 == top2, lane_f, big), axis=-1, keepdims=True)
    e2 = jnp.exp(top2 - top1)
    w1 = p_grp / (1.0 + e2)
    cw_ref[...] = jnp.where(lane_f == idx1, w1, jnp.where(lane_f == idx2, w1 * e2, 0.0))


def route(h2d, g, shift, scale, w_router, b_router, rows_per_mod, tm=512):
    r, d = h2d.shape
    tiles_per_mod = rows_per_mod // tm
    mod_spec = pl.BlockSpec((1, 1, d), lambda i: (i // tiles_per_mod, 0, 0))
    return pl.pallas_call(
        _route_kernel,
        grid=(r // tm,),
        in_specs=[pl.BlockSpec((tm, d), lambda i: (i, 0)), pl.BlockSpec((1, d), lambda i: (0, 0)),
                  mod_spec, mod_spec,
                  pl.BlockSpec((d, LANE), lambda i: (0, 0)), pl.BlockSpec((1, LANE), lambda i: (0, 0))],
        out_specs=[pl.BlockSpec((tm, d), lambda i: (i, 0)), pl.BlockSpec((tm, LANE), lambda i: (i, 0))],
        out_shape=[jax.ShapeDtypeStruct((r, d), BF16), jax.ShapeDtypeStruct((r, LANE), F32)],
        compiler_params=_cparams("parallel"),
        name="moe_route",
    )(h2d, g.reshape(1, d), shift, scale, w_router, b_router)


def _experts_kernel(t_ref, cw_ref, h_ref, m5_ref, fg_ref, wg_ref, wu_ref, wd_ref, o_ref, acc_ref):
    e = pl.program_id(1)

    @pl.when(e == 0)
    def _():
        acc_ref[...] = jnp.zeros(acc_ref.shape, F32)

    t = t_ref[...]
    cw = cw_ref[...]
    w = jnp.sum(jnp.where(_iota2(cw.shape, 1) == e, cw, 0.0), axis=-1, keepdims=True)
    hid = _silu(_dot(t, wg_ref[0])) * _dot(t, wu_ref[0]) * w
    acc_ref[...] += _dot(hid.astype(BF16), wd_ref[0])

    @pl.when(e == pl.num_programs(1) - 1)
    def _():
        h2 = h_ref[...] + m5_ref[0] * acc_ref[...]
        ms = jnp.mean(h2 * h2, axis=-1, keepdims=True)
        o_ref[...] = h2 * lax.rsqrt(ms + NORM_EPS) * fg_ref[...]


def experts_residual_norm(t, cw, h2d, m5, final_g, w_gate, w_up, w_down, rows_per_mod, tm=1024):
    r, d = h2d.shape
    ne, _, f = w_gate.shape
    tiles_per_mod = rows_per_mod // tm
    return pl.pallas_call(
        _experts_kernel,
        grid=(r // tm, ne),
        in_specs=[pl.BlockSpec((tm, d), lambda i, e: (i, 0)), pl.BlockSpec((tm, LANE), lambda i, e: (i, 0)),
                  pl.BlockSpec((tm, d), lambda i, e: (i, 0)),
                  pl.BlockSpec((1, 1, d), lambda i, e: (i // tiles_per_mod, 0, 0)),
                  pl.BlockSpec((1, d), lambda i, e: (0, 0)),
                  pl.BlockSpec((1, d, f), lambda i, e: (e, 0, 0)), pl.BlockSpec((1, d, f), lambda i, e: (e, 0, 0)),
                  pl.BlockSpec((1, f, d), lambda i, e: (e, 0, 0))],
        out_specs=pl.BlockSpec((tm, d), lambda i, e: (i, 0)),
        out_shape=jax.ShapeDtypeStruct((r, d), F32),
        scratch_shapes=[pltpu.VMEM((tm, d), F32)],
        compiler_params=_cparams("parallel", "arbitrary"),
        name="moe_experts",
    )(t, cw, h2d, m5, final_g.reshape(1, d), w_gate, w_up, w_down)


def _to_col_major(x, rows):
    b, l, c = x.shape
    return x.reshape(b, rows, GRID_W, c).transpose(0, 2, 1, 3).reshape(b, l, c)


def _to_row_major(x, rows):
    b, l, c = x.shape
    return x.reshape(b, GRID_W, rows, c).transpose(0, 2, 1, 3).reshape(b, l, c)


def _pad_rows(a, n):
    return jnp.pad(a, ((0, n - a.shape[0]),) + ((0, 0),) * (a.ndim - 1))


def _gdn_gates(pa, a_log, dt_bias):
    b, l, _ = pa.shape
    ab = pa[..., 4 * C_A:4 * C_A + 4 * H_A]
    a = ab[..., :2 * H_A].reshape(b, l, 2, H_A)
    bt = ab[..., 2 * H_A:].reshape(b, l, 2, H_A)
    g = -jnp.exp(a_log) * jax.nn.softplus(a + dt_bias)
    beta = jax.nn.sigmoid(bt)
    return jnp.concatenate([g, beta], axis=2).transpose(0, 3, 1, 2)


def kernel(x, c, ctx, c_ctx, ada_w, ada_b, norm_mix_g, norm_ffn_g, w_in, gdn_conv, gdn_a_log, gdn_dt_bias,
           gdn_onorm_g, rwkv_mu, rwkv_w0, rwkv_w2, rwkv_a0, rwkv_a2, rwkv_g2, rwkv_k_k, rwkv_k_a, rwkv_r_k,
           rwkv_lnx_g, rwkv_lnx_b, w_o_a, w_o_b, w_out, router_grp, router_grp_b, router_exp, router_exp_b,
           moe_w_gate, moe_w_up, moe_w_down, final_norm_g):
    bsz, seq, d = x.shape
    lc = ctx.shape[1]
    rows = seq // GRID_W
    a_cols = 4 * C_A + 4 * H_A
    b_cols = 3 * C_B + 2 * LORA_W + 2 * LORA_A + LORA_G

    cc = _pad_rows(jnp.concatenate([c, c_ctx[None]], axis=0), 2 * SUBLANE)
    mod = ada_modulation(cc, ada_w[0], ada_b[0])
    m_lat = [mod[:bsz, i * d:(i + 1) * d].reshape(bsz, 1, d) for i in range(6)]
    m_ctx = [mod[bsz:bsz + 1, i * d:(i + 1) * d].reshape(1, 1, d) for i in range(2)]

    w = w_in[0]
    w_a = jnp.pad(w[:, :a_cols], ((0, 0), (0, 4 * C_A + LANE - a_cols))).astype(BF16)
    w_b = w[:, a_cols:a_cols + b_cols].astype(BF16)
    w_g = w[:, a_cols + b_cols:].astype(BF16)

    x2d = x.reshape(bsz * seq, d)
    pa_lat, pg_lat = norm_modulate_project(x2d, norm_mix_g[0], m_lat[0], m_lat[1], [w_a, w_g], seq)
    (pb_lat,) = norm_modulate_project(_to_col_major(x, rows).reshape(bsz * seq, d), norm_mix_g[0],
                                      m_lat[0], m_lat[1], [w_b], seq)
    pa_ctx, pb_ctx = norm_modulate_project(ctx.reshape(bsz * lc, d), norm_mix_g[0], m_ctx[0], m_ctx[1],
                                           [w_a, w_b], bsz * lc)
    pa_lat = pa_lat.reshape(bsz, seq, -1)
    pb_lat = pb_lat.reshape(bsz, seq, -1)
    pa_ctx = pa_ctx.reshape(bsz, lc, -1)
    pb_ctx = pb_ctx.reshape(bsz, lc, -1)

    gates = jnp.concatenate([_gdn_gates(pa_ctx, gdn_a_log[0], gdn_dt_bias[0]),
                             _gdn_gates(pa_lat, gdn_a_log[0], gdn_dt_bias[0])], axis=2)
    nc = (lc + seq) // CHUNK
    gcol = gates.reshape(bsz, H_A, nc, CHUNK, 4)
    grow = gcol.transpose(0, 1, 2, 4, 3)
    oa = gdn_branch(pa_ctx, pa_lat, _pad_rows(gdn_conv[0], SUBLANE), gcol, grow, gdn_onorm_g[0])

    mu = rwkv_mu[0]
    pch = jnp.stack([rwkv_k_k[0], rwkv_k_a[0], rwkv_r_k[0].reshape(C_B), rwkv_lnx_g[0], rwkv_lnx_b[0],
                     mu[:C_B], mu[C_B:2 * C_B], mu[2 * C_B:3 * C_B]])
    plo = _pad_rows(mu[3 * C_B:].reshape(3, LANE), SUBLANE)
    w0a0 = _pad_rows(jnp.concatenate([rwkv_w0[0], rwkv_a0[0]], axis=0), SUBLANE)
    zw = jnp.zeros((LORA_W, C_B), F32)
    w2pad = jnp.stack([jnp.concatenate([rwkv_w2[0, 0], zw]), jnp.concatenate([zw, rwkv_w2[0, 1]])])
    a2pad = jnp.stack([jnp.concatenate([rwkv_a2[0, 0], zw]), jnp.concatenate([zw, rwkv_a2[0, 1]])])
    ob = rwkv_branch(pb_ctx, pb_lat, pch, plo, w0a0, w2pad, a2pad, rwkv_g2[0])
    ob = _to_row_major(ob, rows)

    h1 = merge_residual(x2d, pg_lat, oa.reshape(bsz * seq, C_A), ob.reshape(bsz * seq, C_B), m_lat[2],
                        w_o_a[0].astype(BF16), w_o_b[0].astype(BF16), w_out[0].astype(BF16), seq)

    w_router = jnp.pad(jnp.concatenate([router_exp[0], router_grp[0]], axis=1),
                       ((0, 0), (0, LANE - N_EXPERTS - N_GROUPS)))
    b_router = jnp.pad(jnp.concatenate([router_exp_b[0], router_grp_b[0]]),
                       (0, LANE - N_EXPERTS - N_GROUPS)).reshape(1, LANE)
    t, cw = route(h1, norm_ffn_g[0], m_lat[3], m_lat[4], w_router, b_router, seq)
    out = experts_residual_norm(t, cw, h1, m_lat[5], final_norm_g,
                                moe_w_gate[0].reshape(N_EXPERTS, d, D_EXPERT).astype(BF16),
                                moe_w_up[0].reshape(N_EXPERTS, d, D_EXPERT).astype(BF16),
                                moe_w_down[0].reshape(N_EXPERTS, D_EXPERT, d).astype(BF16), seq)
    return out.reshape(bsz, seq, d)
```

```python
import functools

import jax
import jax.numpy as jnp
from jax import lax
from jax.experimental import pallas as pl
from jax.experimental.pallas import tpu as pltpu

F32 = jnp.float32
BF16 = jnp.bfloat16
HIGHEST = lax.Precision.HIGHEST

GRID_W = 64
H_A = 4
DK_A = 128
C_A = H_A * DK_A
SHORT_CONV = 5
CHUNK = 64
H_B = 8
N_B = 64
C_B = H_B * N_B
LORA_W = 64
LORA_A = 64
LORA_G = 128
N_GROUPS = 4
EXPERTS_PER_GROUP = 8
N_EXPERTS = N_GROUPS * EXPERTS_PER_GROUP
D_EXPERT = 256
NORM_EPS = 1e-6
LNX_EPS = 1e-5 * N_B

LANE = 128
SUBLANE = 8
VMEM_LIMIT = 56 * 1024 * 1024


def _cparams(*sem):
    return pltpu.CompilerParams(dimension_semantics=sem, vmem_limit_bytes=VMEM_LIMIT)


SINGLE = "bf16 operands, one MXU pass, f32 accumulation"
MM = SINGLE


def _operands(a, b, precision):
    if precision is SINGLE:
        return a.astype(BF16), b.astype(BF16), None
    return a, b, precision


def _dot(a, b, precision=None):
    a, b, precision = _operands(a, b, precision)
    return jnp.dot(a, b, preferred_element_type=F32, precision=precision)


def _dot_nt(a, b, precision=None):
    a, b, precision = _operands(a, b, precision)
    return lax.dot_general(a, b, (((1,), (1,)), ((), ())), preferred_element_type=F32, precision=precision)


def _dot_tn(a, b, precision=None):
    a, b, precision = _operands(a, b, precision)
    return lax.dot_general(a, b, (((0,), (0,)), ((), ())), preferred_element_type=F32, precision=precision)


def _split3(x):
    hi = x.astype(BF16)
    r1 = x - hi.astype(F32)
    mid = r1.astype(BF16)
    lo = (r1 - mid.astype(F32)).astype(BF16)
    return hi, mid, lo


def _dot_small_int_lhs(m, x):
    mb = m.astype(BF16)
    hi, mid, lo = _split3(x)
    return _dot(mb, hi) + _dot(mb, mid) + _dot(mb, lo)


def _dot_small_int_rhs(x, m):
    mb = m.astype(BF16)
    hi, mid, lo = _split3(x)
    return _dot(hi, mb) + _dot(mid, mb) + _dot(lo, mb)


def _sigmoid(x):
    return jax.nn.sigmoid(x)


def _silu(x):
    return x * jax.nn.sigmoid(x)


def _softplus(x):
    return jnp.maximum(x, 0.0) + jnp.log1p(jnp.exp(-jnp.abs(x)))


def _ada_kernel(c_ref, w_ref, b_ref, o_ref):
    s = _silu(c_ref[...])
    o_ref[...] = _dot(s, w_ref[...], HIGHEST) + b_ref[...]


def ada_modulation(cc, ada_w, ada_b):
    r, d = cc.shape
    n = ada_w.shape[1]
    tn = 1536
    return pl.pallas_call(
        _ada_kernel,
        grid=(n // tn,),
        in_specs=[pl.BlockSpec((r, d), lambda j: (0, 0)),
                  pl.BlockSpec((d, tn), lambda j: (0, j)),
                  pl.BlockSpec((1, tn), lambda j: (0, j))],
        out_specs=pl.BlockSpec((r, tn), lambda j: (0, j)),
        out_shape=jax.ShapeDtypeStruct((r, n), F32),
        compiler_params=_cparams("arbitrary"),
        name="ada_modulation",
    )(cc, ada_w, ada_b.reshape(1, n))


def _rms_modulate(x, g, shift, scale):
    ms = jnp.mean(x * x, axis=-1, keepdims=True)
    y = x * lax.rsqrt(ms + NORM_EPS) * g
    return y * (1.0 + scale) + shift


def _inproj_kernel(x_ref, g_ref, sh_ref, sc_ref, *rest, n_out):
    w_refs, o_refs = rest[:n_out], rest[n_out:]
    u = _rms_modulate(x_ref[...], g_ref[...], sh_ref[0], sc_ref[0]).astype(BF16)
    for w_ref, o_ref in zip(w_refs, o_refs):
        o_ref[...] = _dot(u, w_ref[...])


def norm_modulate_project(x2d, g, shift, scale, weights, rows_per_mod, tm=256):
    r, d = x2d.shape
    tiles_per_mod = rows_per_mod // tm
    mod_spec = pl.BlockSpec((1, 1, d), lambda i: (i // tiles_per_mod, 0, 0))
    in_specs = [pl.BlockSpec((tm, d), lambda i: (i, 0)),
                pl.BlockSpec((1, d), lambda i: (0, 0)), mod_spec, mod_spec]
    in_specs += [pl.BlockSpec(w.shape, lambda i: (0, 0)) for w in weights]
    return pl.pallas_call(
        functools.partial(_inproj_kernel, n_out=len(weights)),
        grid=(r // tm,),
        in_specs=in_specs,
        out_specs=[pl.BlockSpec((tm, w.shape[1]), lambda i: (i, 0)) for w in weights],
        out_shape=[jax.ShapeDtypeStruct((r, w.shape[1]), F32) for w in weights],
        compiler_params=_cparams("parallel"),
        name="norm_modulate_project",
    )(x2d, g.reshape(1, d), shift, scale, *weights)


def _iota2(shape, dim):
    return lax.broadcasted_iota(jnp.int32, shape, dim)


def _unit_tri_inverse(lm, nil):
    n = lm.shape[0]
    eye = jnp.where(_iota2((n, n), 0) == _iota2((n, n), 1), 1.0, 0.0).astype(F32)
    x = eye - lm
    p = _dot(lm, lm, MM)
    yield
    k = 2
    while 2 * k < nil:
        xp = _dot(jnp.concatenate([x, p], axis=0), p, MM)
        yield
        x = x + xp[:n]
        p = xp[n:]
        k *= 2
    res = x + _dot(x, p, MM)
    yield
    return res


def _interleave(chains):
    results = [None] * len(chains)
    live = list(range(len(chains)))
    while live:
        for i in list(live):
            try:
                next(chains[i])
            except StopIteration as done:
                results[i] = done.value
                live.remove(i)
    return results


def _fill_padded(pad_ref, x, n):
    zeros = jnp.zeros((SUBLANE, LANE), F32)
    pad_ref[0:SUBLANE, :] = zeros
    pad_ref[SUBLANE:SUBLANE + n, :] = x
    pad_ref[SUBLANE + n:2 * SUBLANE + n, :] = zeros


def _pair_masks(c):
    n = 2 * c
    brow, bcol = _iota2((n, n), 0), _iota2((n, n), 1)
    same = jnp.logical_not(jnp.logical_xor(brow >= c, bcol >= c))
    tr, tc = jnp.where(brow >= c, brow - c, brow), jnp.where(bcol >= c, bcol - c, bcol)
    fwd = brow < c
    incl = same & ((fwd & (tr >= tc)) | (jnp.logical_not(fwd) & (tr <= tc)))
    strict = same & ((fwd & (tr > tc)) | (jnp.logical_not(fwd) & (tr < tc)))
    return incl, strict


def _gdn_chunk_local(q, k, v, g4, r4):
    c = q.shape[0]
    n = 2 * c
    row, col = _iota2((c, c), 0), _iota2((c, c), 1)
    gc2 = jnp.concatenate(
        [jnp.sum(jnp.where(row >= col, r4[0:1, :], 0.0), axis=1, keepdims=True),
         jnp.sum(jnp.where(row <= col, r4[1:2, :], 0.0), axis=1, keepdims=True)], axis=0)
    tot_f = jnp.sum(r4[0:1, :], axis=1, keepdims=True)
    tot_b = jnp.sum(r4[1:2, :], axis=1, keepdims=True)
    tot2 = jnp.concatenate([jnp.broadcast_to(tot_f, (c, 1)), jnp.broadcast_to(tot_b, (c, 1))], axis=0)
    ri, cj = _iota2((c, n), 0), _iota2((c, n), 1)
    lane_f = cj < c
    cjm = jnp.where(lane_f, cj, cj - c)
    keep = (lane_f & (ri <= cjm)) | (jnp.logical_not(lane_f) & (ri >= cjm))
    gc_row2 = jnp.sum(jnp.where(keep, jnp.where(lane_f, g4[:, 0:1], g4[:, 1:2]), 0.0), axis=0, keepdims=True)
    incl, strict = _pair_masks(c)
    decay = jnp.where(incl, jnp.exp(jnp.where(incl, gc2 - gc_row2, 0.0)), 0.0)
    beta2 = jnp.concatenate([g4[:, 2:3], g4[:, 3:4]], axis=0)
    kk2 = jnp.concatenate([k, k], axis=0)
    qq2 = jnp.concatenate([q, q], axis=0)
    kb2 = kk2 * beta2
    m = _dot_nt(jnp.concatenate([kb2, qq2], axis=0), kk2, MM)
    yield
    t = yield from _unit_tri_inverse(jnp.where(strict, m[:n] * decay, 0.0), c)
    egc2 = jnp.exp(gc2)
    sol = _dot(t, jnp.concatenate([jnp.concatenate([v, v], axis=0) * beta2, kb2 * egc2], axis=1), MM)
    yield
    qk2 = jnp.where(incl, m[n:] * decay, 0.0)
    kdt2 = (kk2 * jnp.exp(tot2 - gc2)).T
    fwd_rows = _iota2(sol.shape, 0) < c
    ks_f = _dot(kdt2, jnp.where(fwd_rows, sol, 0.0), MM)
    ks_b = _dot(kdt2, jnp.where(fwd_rows, 0.0, sol), MM)
    yield
    return sol[:, :LANE], sol[:, LANE:], qq2 * egc2, qk2, ks_f, ks_b, jnp.exp(tot_f), jnp.exp(tot_b)


def _gdn_kernel(qc_ref, kc_ref, vc_ref, ql_ref, kl_ref, vl_ref, zl_ref, cwq_ref, cwk_ref, cwv_ref,
                gcol_ref, grow_ref, og_ref, o_ref, q_s, k_s, v_s, oacc_s, pad_s, st_s,
                u_s, wq_s, k0_s, k1_s, qk_s, ge_s, p_s, *, lc, ll):
    tile = min(256, lc, ll)

    def prep(x_ref, cw_ref, dst, off, n, mode):
        _fill_padded(pad_s, x_ref[0], n)
        cw = cw_ref[...]
        for t0 in range(0, n, tile):
            acc = cw[0:1, :] * pad_s[pl.ds(SUBLANE - 2 + t0, tile), :]
            for j in range(1, SHORT_CONV):
                acc = acc + cw[j:j + 1, :] * pad_s[pl.ds(SUBLANE - 2 + j + t0, tile), :]
            y = _silu(acc)
            if mode != "v":
                y = y * lax.rsqrt(jnp.sum(y * y, axis=-1, keepdims=True) + NORM_EPS)
            if mode == "q":
                y = y * (DK_A ** -0.5)
            dst[pl.ds(off + t0, tile), :] = y

    prep(qc_ref, cwq_ref, q_s, 0, lc, "q")
    prep(kc_ref, cwk_ref, k_s, 0, lc, "k")
    prep(vc_ref, cwv_ref, v_s, 0, lc, "v")
    prep(ql_ref, cwq_ref, q_s, lc, ll, "q")
    prep(kl_ref, cwk_ref, k_s, lc, ll, "k")
    prep(vl_ref, cwv_ref, v_s, lc, ll, "v")

    oacc_s[...] = jnp.zeros(oacc_s.shape, F32)
    st_s[...] = jnp.zeros(st_s.shape, F32)
    nc_ctx = lc // CHUNK
    nc_tot = (lc + ll) // CHUNK

    c = CHUNK

    width = max(w for w in (12, 6, 4, 3, 2, 1) if nc_tot % w == 0)

    def local(it, carry):
        cis = [it * width + j for j in range(width)]
        chains = []
        for ci in cis:
            rows = pl.ds(pl.multiple_of(ci * c, c), c)
            chains.append(_gdn_chunk_local(q_s[rows, :], k_s[rows, :], v_s[rows, :],
                                           gcol_ref[0, 0, ci], grow_ref[0, 0, ci]))
        for ci, (u2, w2, qd2, qk2, ks_f, ks_b, ge_f, ge_b) in zip(cis, _interleave(chains)):
            u_s[ci] = u2
            wq_s[ci, 0] = jnp.concatenate([w2[:c], qd2[:c]], axis=0).astype(BF16)
            wq_s[ci, 1] = jnp.concatenate([w2[c:], qd2[c:]], axis=0).astype(BF16)
            for d, ks in enumerate((ks_f, ks_b)):
                k0_s[ci, d] = ks[:, :LANE]
                k1_s[ci, d] = ks[:, LANE:].astype(BF16)
            qk_s[ci] = qk2.astype(BF16)
            ge_s[ci, 0:1, :] = jnp.broadcast_to(ge_f, (1, LANE))
            ge_s[ci, 1:2, :] = jnp.broadcast_to(ge_b, (1, LANE))
        return carry

    lax.fori_loop(0, nc_tot // width, local, 0)

    def bwd_chunk(s):
        return jnp.where(s < nc_ctx, nc_ctx - 1 - s, nc_tot - 1 - (s - nc_ctx))

    def outputs(s, keep):
        cb = bwd_chunk(s)
        p_f, p_b = p_s[0], p_s[1]
        v_new = jnp.concatenate([u_s[s, 0:c, :] - p_f[:c], u_s[cb, c:2 * c, :] - p_b[:c]], axis=0)
        qk = jnp.concatenate([qk_s[s, 0:c, :], qk_s[cb, c:2 * c, :]], axis=0)
        o2 = jnp.concatenate([p_f[c:], p_b[c:]], axis=0) + _dot(qk, v_new.astype(BF16))
        if keep is not None:
            o2 = jnp.where(keep, o2, 0.0)
        rows_f = pl.ds(pl.multiple_of(s * c, c), c)
        rows_b = pl.ds(pl.multiple_of(cb * c, c), c)
        oacc_s[rows_f, :] = oacc_s[rows_f, :] + o2[:c]
        oacc_s[rows_b, :] = oacc_s[rows_b, :] + o2[c:]

    def step(s, carry):
        outputs(jnp.maximum(s - 1, 0), s > 0)
        cb = bwd_chunk(s)
        s_f, s_b = st_s[0], st_s[1]
        sb_f, sb_b = s_f.astype(BF16), s_b.astype(BF16)
        p_s[0] = _dot(wq_s[s, 0], sb_f)
        p_s[1] = _dot(wq_s[cb, 1], sb_b)
        st_s[0] = s_f * ge_s[s, 0:1, :] + k0_s[s, 0] - _dot(k1_s[s, 0], sb_f)
        st_s[1] = s_b * ge_s[cb, 1:2, :] + k0_s[cb, 1] - _dot(k1_s[cb, 1], sb_b)
        return carry

    p_s[...] = jnp.zeros(p_s.shape, F32)
    lax.fori_loop(0, nc_tot, step, 0)
    outputs(nc_tot - 1, None)

    og = og_ref[...]
    for t0 in range(0, ll, tile):
        o = oacc_s[pl.ds(lc + t0, tile), :]
        o = o * lax.rsqrt(jnp.mean(o * o, axis=-1, keepdims=True) + NORM_EPS) * og
        o_ref[0, pl.ds(t0, tile), :] = o * _silu(zl_ref[0, pl.ds(t0, tile), :])


def gdn_branch(pa_ctx, pa_lat, conv_w, gcol, grow, onorm_g):
    b, lc, _ = pa_ctx.shape
    ll = pa_lat.shape[1]
    nc = (lc + ll) // CHUNK

    def col(blk, n):
        return pl.BlockSpec((1, n, LANE), lambda i, h: (i, 0, blk * H_A + h))

    def cw(blk):
        return pl.BlockSpec((SUBLANE, LANE), lambda i, h: (0, blk * H_A + h))

    in_specs = [col(0, lc), col(1, lc), col(2, lc), col(0, ll), col(1, ll), col(2, ll), col(3, ll),
                cw(0), cw(1), cw(2),
                pl.BlockSpec((1, 1, nc, CHUNK, 4), lambda i, h: (i, h, 0, 0, 0)),
                pl.BlockSpec((1, 1, nc, 4, CHUNK), lambda i, h: (i, h, 0, 0, 0)),
                pl.BlockSpec((1, LANE), lambda i, h: (0, 0))]
    lt = lc + ll
    return pl.pallas_call(
        functools.partial(_gdn_kernel, lc=lc, ll=ll),
        grid=(b, H_A),
        in_specs=in_specs,
        out_specs=pl.BlockSpec((1, ll, LANE), lambda i, h: (i, 0, h)),
        out_shape=jax.ShapeDtypeStruct((b, ll, C_A), F32),
        scratch_shapes=[pltpu.VMEM((lt, LANE), F32)] * 4
        + [pltpu.VMEM((max(lc, ll) + 2 * SUBLANE, LANE), F32), pltpu.VMEM((2, DK_A, DK_A), F32),
           pltpu.VMEM((nc, 2 * CHUNK, LANE), F32), pltpu.VMEM((nc, 2, 2 * CHUNK, LANE), BF16),
           pltpu.VMEM((nc, 2, DK_A, DK_A), F32), pltpu.VMEM((nc, 2, DK_A, DK_A), BF16),
           pltpu.VMEM((nc, 2 * CHUNK, 2 * CHUNK), BF16),
           pltpu.VMEM((nc, SUBLANE, LANE), F32), pltpu.VMEM((2, 2 * CHUNK, LANE), F32)],
        compiler_params=_cparams("parallel", "parallel"),
        name="gdn_branch",
    )(pa_ctx, pa_ctx, pa_ctx, pa_lat, pa_lat, pa_lat, pa_lat, conv_w, conv_w, conv_w, gcol, grow,
      onorm_g.reshape(1, LANE))


def _stack_heads(x, head0):
    return jnp.concatenate([jnp.where(head0, x, 0.0), jnp.where(head0, 0.0, x)], axis=0)


def _rwkv_chunk_local(r, v, a, logw, kdir, b, lower):
    c = r.shape[0]
    n = 2 * c
    row, col = _iota2((c, c), 0), _iota2((c, c), 1)
    tri = jnp.where(row >= col if lower else row <= col, 1.0, 0.0).astype(F32)
    lcum = _dot_small_int_lhs(tri, logw)
    tot = jnp.sum(logw, axis=0, keepdims=True)
    e_in = jnp.exp(lcum)
    e_ex = jnp.exp(lcum - logw)
    e_neg = jnp.exp(-lcum)
    e_rem = jnp.exp(tot - lcum)
    head0 = _iota2((c, LANE), 1) < N_B
    a2 = _stack_heads(a * e_ex, head0)
    r2 = _stack_heads(r * e_in, head0)
    b2 = _stack_heads(b * e_neg, head0)
    k2 = _stack_heads(kdir * e_neg, head0)
    v2 = _stack_heads(v, head0)
    m = _dot_nt(jnp.concatenate([a2, r2], axis=0), jnp.concatenate([b2, k2], axis=0), MM)
    yield
    brow, bcol = _iota2((n, n), 0), _iota2((n, n), 1)
    same = jnp.logical_not(jnp.logical_xor(brow >= c, bcol >= c))
    tr, tc = jnp.where(brow >= c, brow - c, brow), jnp.where(bcol >= c, bcol - c, bcol)
    strict = same & ((tr > tc) if lower else (tr < tc))
    incl = same & ((tr >= tc) if lower else (tr <= tc))
    ak = jnp.where(strict, m[:n, n:], 0.0)
    rb = jnp.where(incl, m[n:, :n], 0.0)
    rk = jnp.where(incl, m[n:, n:], 0.0)
    akv_rkv = _dot(jnp.concatenate([ak, rk], axis=0), v2, MM)
    kv0 = _dot_tn(_stack_heads(kdir * e_rem, head0), v2, MM)
    t = yield from _unit_tri_inverse(jnp.where(strict, -m[:n, :n], 0.0), c)
    tw = _dot(t, jnp.concatenate([a2, akv_rkv[:n]], axis=1), MM)
    yield
    x = jnp.concatenate([tw[:, :LANE], r2], axis=0)
    eg = jnp.broadcast_to(jnp.exp(tot), (LANE, LANE)).T
    bu = _dot(_stack_heads(b * e_rem, head0).T, tw, MM)
    yield
    return x, rb, bu[:, :LANE], tw[:, LANE:], akv_rkv[n:], kv0 + bu[:, LANE:], eg


def _rwkv_kernel(*refs, lc, ll):
    ctx_refs, lat_refs = refs[0:6], refs[6:12]
    pch_ref, plo_ref, w0a0_ref, w2_ref, a2_ref, g2_ref, o_ref = refs[12:19]
    r_s, v_s, a_s, lw_s, kd_s, b_s, gate_s, bonus_s, y_s, pad_s, st_s, x_s, rb_s, m1_s, f_s, p_s = refs[19:]
    tile = min(256, lc, ll)
    pch = pch_ref[...]
    k_k, k_a, r_k, lnx_g, lnx_b = (pch[i:i + 1, :] for i in range(5))
    mus = [pch[5:6, :], pch[6:7, :], pch[7:8, :]] + [plo_ref[i:i + 1, :] for i in range(3)]
    w0a0 = w0a0_ref[...]
    rr, cc = _iota2((LANE, LANE), 0), _iota2((LANE, LANE), 1)
    seg = jnp.where(jnp.logical_xor(rr >= N_B, cc >= N_B), 0.0, 1.0).astype(F32)

    def prep(src_refs, off, n, is_lat):
        for j in range(6):
            _fill_padded(pad_s.at[j], src_refs[j][0], n)
        for t0 in range(0, n, tile):
            mixed = []
            for j in range(6):
                x = pad_s[j, pl.ds(SUBLANE + t0, tile), :]
                nb = pad_s[j, pl.ds(SUBLANE - 1 + t0, tile), :] + pad_s[j, pl.ds(SUBLANE + 1 + t0, tile), :]
                mixed.append(x + (0.5 * nb - x) * mus[j])
            r, k, v, wl, al, gl = mixed
            wl = jnp.tanh(wl)
            kk = k * k_k
            kk = kk * lax.rsqrt(_dot_small_int_rhs(kk * kk, seg) + NORM_EPS)
            ksum = jnp.zeros_like(k)
            for d in range(2):
                w_log = -_softplus(-(w0a0[d:d + 1, :] + _dot(wl, w2_ref[d], MM))) - 0.5
                iclr = _sigmoid(w0a0[2 + d:3 + d, :] + _dot(al, a2_ref[d], MM))
                kdir = k * (1.0 + (iclr - 1.0) * k_a)
                ksum = ksum + kdir
                lw_s[d, pl.ds(off + t0, tile), :] = -jnp.exp(w_log)
                kd_s[d, pl.ds(off + t0, tile), :] = kdir
                b_s[d, pl.ds(off + t0, tile), :] = kk * iclr
            r_s[pl.ds(off + t0, tile), :] = r
            v_s[pl.ds(off + t0, tile), :] = v
            a_s[pl.ds(off + t0, tile), :] = -kk
            if is_lat:
                gate_s[pl.ds(t0, tile), :] = _dot(_sigmoid(gl), g2_ref[...], MM)
                bonus_s[pl.ds(t0, tile), :] = _dot_small_int_rhs(r * ksum * r_k, seg) * v

    prep(ctx_refs, 0, lc, False)
    prep(lat_refs, lc, ll, True)

    y_s[...] = jnp.zeros(y_s.shape, F32)
    st_s[...] = jnp.zeros(st_s.shape, F32)
    p_s[...] = jnp.zeros(p_s.shape, F32)
    nc_ctx = lc // CHUNK
    nc_tot = (lc + ll) // CHUNK

    group = x_s.shape[0]
    n2 = 2 * CHUNK

    def chunk_of(s, d):
        cb = jnp.where(s < nc_ctx, nc_ctx - 1 - s, nc_tot - 1 - (s - nc_ctx))
        return pl.ds(pl.multiple_of((cb if d else s) * CHUNK, CHUNK), CHUNK)

    width = max(w for w in (6, 3, 2, 1) if group % w == 0)

    def run_group(gi, carry):
        def local(it, c2):
            slots = [(it * width + j, d) for j in range(width) for d in range(2)]
            chains = []
            for i, d in slots:
                rows = chunk_of(gi * group + i, d)
                chains.append(_rwkv_chunk_local(r_s[rows, :], v_s[rows, :], a_s[rows, :], lw_s[d, rows, :],
                                                kd_s[d, rows, :], b_s[d, rows, :], lower=(d == 0)))
            for (i, d), (x, rb, m1, u0, rkv, hc, eg) in zip(slots, _interleave(chains)):
                x_s[i, d] = x.astype(BF16)
                rb_s[i, d] = rb.astype(BF16)
                m1_s[i, d] = m1.astype(BF16)
                f_s[i, d, 0] = u0
                f_s[i, d, 1] = rkv
                f_s[i, d, 2] = hc
                f_s[i, d, 3] = eg
            return c2

        lax.fori_loop(0, group // width, local, 0)

        def outputs(i, keep):
            for d in range(2):
                rows = chunk_of(gi * group + i, d)
                u = p_s[d, 0:n2, :] + f_s[i, d, 0]
                y2 = p_s[d, n2:2 * n2, :] + _dot(rb_s[i, d], u.astype(BF16)) + f_s[i, d, 1]
                y = y2[:CHUNK] + y2[CHUNK:]
                if keep is not None:
                    y = jnp.where(keep, y, 0.0)
                y_s[rows, :] = y_s[rows, :] + y

        def step(i, c2):
            outputs(jnp.maximum(i - 1, 0), i > 0)
            hs = [st_s[d] for d in range(2)]
            hbs = [h.astype(BF16) for h in hs]
            for d in range(2):
                p_s[d] = _dot(x_s[i, d], hbs[d])
            for d in range(2):
                st_s[d] = f_s[i, d, 3] * hs[d] + f_s[i, d, 2] + _dot(m1_s[i, d], hbs[d])
            return c2

        lax.fori_loop(0, group, step, 0)
        outputs(group - 1, None)
        return carry

    lax.fori_loop(0, nc_tot // group, run_group, 0)

    inv_n = 1.0 / N_B
    for t0 in range(0, ll, tile):
        yf = y_s[pl.ds(lc + t0, tile), :]
        cen = yf - _dot_small_int_rhs(yf, seg) * inv_n
        var = _dot_small_int_rhs(cen * cen, seg) * inv_n
        y = cen * lax.rsqrt(var + LNX_EPS) * lnx_g + lnx_b
        o_ref[0, pl.ds(t0, tile), :] = (y + bonus_s[pl.ds(t0, tile), :]) * gate_s[pl.ds(t0, tile), :]


def rwkv_branch(pb_ctx, pb_lat, pch, plo, w0a0, w2pad, a2pad, g2):
    b, lc, _ = pb_ctx.shape
    ll = pb_lat.shape[1]
    lt = lc + ll
    pairs = C_B // LANE

    def col(blk, n, per_pair):
        if per_pair:
            return pl.BlockSpec((1, n, LANE), lambda i, p: (i, 0, blk * pairs + p))
        return pl.BlockSpec((1, n, LANE), lambda i, p: (i, 0, 3 * pairs + blk))

    def cols(n):
        return [col(0, n, True), col(1, n, True), col(2, n, True), col(0, n, False), col(1, n, False),
                col(2, n, False)]

    in_specs = cols(lc) + cols(ll) + [
        pl.BlockSpec((SUBLANE, LANE), lambda i, p: (0, p)),
        pl.BlockSpec((SUBLANE, LANE), lambda i, p: (0, 0)),
        pl.BlockSpec((SUBLANE, LANE), lambda i, p: (0, p)),
        pl.BlockSpec((2, LANE, LANE), lambda i, p: (0, 0, p)),
        pl.BlockSpec((2, LANE, LANE), lambda i, p: (0, 0, p)),
        pl.BlockSpec((LANE, LANE), lambda i, p: (0, p))]
    seq = pltpu.VMEM((lt, LANE), F32)
    seq2 = pltpu.VMEM((2, lt, LANE), F32)
    lat = pltpu.VMEM((ll, LANE), F32)
    nc = lt // CHUNK
    group = max(g for g in (6, 4, 3, 2, 1) if nc % g == 0)
    n2 = 2 * CHUNK
    return pl.pallas_call(
        functools.partial(_rwkv_kernel, lc=lc, ll=ll),
        grid=(b, pairs),
        in_specs=in_specs,
        out_specs=pl.BlockSpec((1, ll, LANE), lambda i, p: (i, 0, p)),
        out_shape=jax.ShapeDtypeStruct((b, ll, C_B), F32),
        scratch_shapes=[seq, seq, seq, seq2, seq2, seq2, lat, lat, seq,
                        pltpu.VMEM((6, max(lc, ll) + 2 * SUBLANE, LANE), F32),
                        pltpu.VMEM((2, LANE, LANE), F32),
                        pltpu.VMEM((group, 2, 2 * n2, LANE), BF16), pltpu.VMEM((group, 2, n2, n2), BF16),
                        pltpu.VMEM((group, 2, LANE, n2), BF16), pltpu.VMEM((group, 2, 4, n2, LANE), F32),
                        pltpu.VMEM((2, 2 * n2, LANE), F32)],
        compiler_params=_cparams("parallel", "parallel"),
        name="rwkv_branch",
    )(*([pb_ctx] * 6), *([pb_lat] * 6), pch, plo, w0a0, w2pad, a2pad, g2)


def _merge_kernel(x_ref, pg_ref, oa_ref, ob_ref, m2_ref, woa_ref, wob_ref, wout_ref, o_ref):
    d = x_ref.shape[1]
    ya = _dot(oa_ref[...].astype(BF16), woa_ref[...])
    yb = _dot(ob_ref[...].astype(BF16), wob_ref[...])
    y = _sigmoid(pg_ref[:, 0:d]) * ya + _sigmoid(pg_ref[:, d:2 * d]) * yb
    o_ref[...] = x_ref[...] + m2_ref[0] * _dot(y.astype(BF16), wout_ref[...])


def merge_residual(x2d, pg, oa, ob, m2, w_o_a, w_o_b, w_out, rows_per_mod, tm=512):
    r, d = x2d.shape

    def rows(n):
        return pl.BlockSpec((tm, n), lambda i: (i, 0))

    def full(w):
        return pl.BlockSpec(w.shape, lambda i: (0, 0))

    tiles_per_mod = rows_per_mod // tm
    return pl.pallas_call(
        _merge_kernel,
        grid=(r // tm,),
        in_specs=[rows(d), rows(2 * d), rows(oa.shape[1]), rows(ob.shape[1]),
                  pl.BlockSpec((1, 1, d), lambda i: (i // tiles_per_mod, 0, 0)),
                  full(w_o_a), full(w_o_b), full(w_out)],
        out_specs=rows(d),
        out_shape=jax.ShapeDtypeStruct((r, d), F32),
        compiler_params=_cparams("parallel"),
        name="merge_residual",
    )(x2d, pg, oa, ob, m2, w_o_a, w_o_b, w_out)


ROUTER_GROUP_LANE0 = N_EXPERTS


def _route_kernel(h_ref, g_ref, sh_ref, sc_ref, wr_ref, br_ref, t_ref, cw_ref):
    t = _rms_modulate(h_ref[...], g_ref[...], sh_ref[0], sc_ref[0])
    t_ref[...] = t.astype(BF16)
    lg = _dot(t, wr_ref[...], HIGHEST) + br_ref[...]
    lane = _iota2(lg.shape, 1)
    lane_f = lane.astype(F32)
    neg = jnp.float32(-jnp.inf)
    big = jnp.float32(2 * LANE)
    is_grp = (lane >= ROUTER_GROUP_LANE0) & (lane < ROUTER_GROUP_LANE0 + N_GROUPS)
    lgg = jnp.where(is_grp, lg, neg)
    mg = jnp.max(lgg, axis=-1, keepdims=True)
    p_grp = 1.0 / jnp.sum(jnp.where(is_grp, jnp.exp(lgg - mg), 0.0), axis=-1, keepdims=True)
    g_sel = jnp.min(jnp.where(lgg == mg, lane_f, big), axis=-1, keepdims=True) - ROUTER_GROUP_LANE0
    grp_of_lane = lax.shift_right_logical(lane, EXPERTS_PER_GROUP.bit_length() - 1).astype(F32)
    in_grp = (lane < N_EXPERTS) & (grp_of_lane == g_sel)
    l1 = jnp.where(in_grp, lg, neg)
    top1 = jnp.max(l1, axis=-1, keepdims=True)
    idx1 = jnp.min(jnp.where(l1 == top1, lane_f, big), axis=-1, keepdims=True)
    l2 = jnp.where(in_grp & (lane_f != idx1), lg, neg)
    top2 = jnp.max(l2, axis=-1, keepdims=True)
    idx2 = jnp.min(jnp.where(l2 == top2, lane_f, big), axis=-1, keepdims=True)
    e2 = jnp.exp(top2 - top1)
    w1 = p_grp / (1.0 + e2)
    cw_ref[...] = jnp.where(lane_f == idx1, w1, jnp.where(lane_f == idx2, w1 * e2, 0.0))


def route(h2d, g, shift, scale, w_router, b_router, rows_per_mod, tm=512):
    r, d = h2d.shape
    tiles_per_mod = rows_per_mod // tm
    mod_spec = pl.BlockSpec((1, 1, d), lambda i: (i // tiles_per_mod, 0, 0))
    return pl.pallas_call(
        _route_kernel,
        grid=(r // tm,),
        in_specs=[pl.BlockSpec((tm, d), lambda i: (i, 0)), pl.BlockSpec((1, d), lambda i: (0, 0)),
                  mod_spec, mod_spec,
                  pl.BlockSpec((d, LANE), lambda i: (0, 0)), pl.BlockSpec((1, LANE), lambda i: (0, 0))],
        out_specs=[pl.BlockSpec((tm, d), lambda i: (i, 0)), pl.BlockSpec((tm, LANE), lambda i: (i, 0))],
        out_shape=[jax.ShapeDtypeStruct((r, d), BF16), jax.ShapeDtypeStruct((r, LANE), F32)],
        compiler_params=_cparams("parallel"),
        name="moe_route",
    )(h2d, g.reshape(1, d), shift, scale, w_router, b_router)


def _experts_kernel(t_ref, cw_ref, h_ref, m5_ref, fg_ref, wg_ref, wu_ref, wd_ref, o_ref, acc_ref):
    e = pl.program_id(1)

    @pl.when(e == 0)
    def _():
        acc_ref[...] = jnp.zeros(acc_ref.shape, F32)

    t = t_ref[...]
    cw = cw_ref[...]
    w = jnp.sum(jnp.where(_iota2(cw.shape, 1) == e, cw, 0.0), axis=-1, keepdims=True)
    hid = _silu(_dot(t, wg_ref[0])) * _dot(t, wu_ref[0]) * w
    acc_ref[...] += _dot(hid.astype(BF16), wd_ref[0])

    @pl.when(e == pl.num_programs(1) - 1)
    def _():
        h2 = h_ref[...] + m5_ref[0] * acc_ref[...]
        ms = jnp.mean(h2 * h2, axis=-1, keepdims=True)
        o_ref[...] = h2 * lax.rsqrt(ms + NORM_EPS) * fg_ref[...]


def experts_residual_norm(t, cw, h2d, m5, final_g, w_gate, w_up, w_down, rows_per_mod, tm=1024):
    r, d = h2d.shape
    ne, _, f = w_gate.shape
    tiles_per_mod = rows_per_mod // tm
    return pl.pallas_call(
        _experts_kernel,
        grid=(r // tm, ne),
        in_specs=[pl.BlockSpec((tm, d), lambda i, e: (i, 0)), pl.BlockSpec((tm, LANE), lambda i, e: (i, 0)),
                  pl.BlockSpec((tm, d), lambda i, e: (i, 0)),
                  pl.BlockSpec((1, 1, d), lambda i, e: (i // tiles_per_mod, 0, 0)),
                  pl.BlockSpec((1, d), lambda i, e: (0, 0)),
                  pl.BlockSpec((1, d, f), lambda i, e: (e, 0, 0)), pl.BlockSpec((1, d, f), lambda i, e: (e, 0, 0)),
                  pl.BlockSpec((1, f, d), lambda i, e: (e, 0, 0))],
        out_specs=pl.BlockSpec((tm, d), lambda i, e: (i, 0)),
        out_shape=jax.ShapeDtypeStruct((r, d), F32),
        scratch_shapes=[pltpu.VMEM((tm, d), F32)],
        compiler_params=_cparams("parallel", "arbitrary"),
        name="moe_experts",
    )(t, cw, h2d, m5, final_g.reshape(1, d), w_gate, w_up, w_down)


def _to_col_major(x, rows):
    b, l, c = x.shape
    return x.reshape(b, rows, GRID_W, c).transpose(0, 2, 1, 3).reshape(b, l, c)


def _to_row_major(x, rows):
    b, l, c = x.shape
    return x.reshape(b, GRID_W, rows, c).transpose(0, 2, 1, 3).reshape(b, l, c)


def _pad_rows(a, n):
    return jnp.pad(a, ((0, n - a.shape[0]),) + ((0, 0),) * (a.ndim - 1))


def _gdn_gates(pa, a_log, dt_bias):
    b, l, _ = pa.shape
    ab = pa[..., 4 * C_A:4 * C_A + 4 * H_A]
    a = ab[..., :2 * H_A].reshape(b, l, 2, H_A)
    bt = ab[..., 2 * H_A:].reshape(b, l, 2, H_A)
    g = -jnp.exp(a_log) * jax.nn.softplus(a + dt_bias)
    beta = jax.nn.sigmoid(bt)
    return jnp.concatenate([g, beta], axis=2).transpose(0, 3, 1, 2)


def kernel(x, c, ctx, c_ctx, ada_w, ada_b, norm_mix_g, norm_ffn_g, w_in, gdn_conv, gdn_a_log, gdn_dt_bias,
           gdn_onorm_g, rwkv_mu, rwkv_w0, rwkv_w2, rwkv_a0, rwkv_a2, rwkv_g2, rwkv_k_k, rwkv_k_a, rwkv_r_k,
           rwkv_lnx_g, rwkv_lnx_b, w_o_a, w_o_b, w_out, router_grp, router_grp_b, router_exp, router_exp_b,
           moe_w_gate, moe_w_up, moe_w_down, final_norm_g):
    bsz, seq, d = x.shape
    lc = ctx.shape[1]
    rows = seq // GRID_W
    a_cols = 4 * C_A + 4 * H_A
    b_cols = 3 * C_B + 2 * LORA_W + 2 * LORA_A + LORA_G

    cc = _pad_rows(jnp.concatenate([c, c_ctx[None]], axis=0), 2 * SUBLANE)
    mod = ada_modulation(cc, ada_w[0], ada_b[0])
    m_lat = [mod[:bsz, i * d:(i + 1) * d].reshape(bsz, 1, d) for i in range(6)]
    m_ctx = [mod[bsz:bsz + 1, i * d:(i + 1) * d].reshape(1, 1, d) for i in range(2)]

    w = w_in[0]
    w_a = jnp.pad(w[:, :a_cols], ((0, 0), (0, 4 * C_A + LANE - a_cols))).astype(BF16)
    w_b = w[:, a_cols:a_cols + b_cols].astype(BF16)
    w_g = w[:, a_cols + b_cols:].astype(BF16)

    x2d = x.reshape(bsz * seq, d)
    pa_lat, pg_lat = norm_modulate_project(x2d, norm_mix_g[0], m_lat[0], m_lat[1], [w_a, w_g], seq)
    (pb_lat,) = norm_modulate_project(_to_col_major(x, rows).reshape(bsz * seq, d), norm_mix_g[0],
                                      m_lat[0], m_lat[1], [w_b], seq)
    pa_ctx, pb_ctx = norm_modulate_project(ctx.reshape(bsz * lc, d), norm_mix_g[0], m_ctx[0], m_ctx[1],
                                           [w_a, w_b], bsz * lc)
    pa_lat = pa_lat.reshape(bsz, seq, -1)
    pb_lat = pb_lat.reshape(bsz, seq, -1)
    pa_ctx = pa_ctx.reshape(bsz, lc, -1)
    pb_ctx = pb_ctx.reshape(bsz, lc, -1)

    gates = jnp.concatenate([_gdn_gates(pa_ctx, gdn_a_log[0], gdn_dt_bias[0]),
                             _gdn_gates(pa_lat, gdn_a_log[0], gdn_dt_bias[0])], axis=2)
    nc = (lc + seq) // CHUNK
    gcol = gates.reshape(bsz, H_A, nc, CHUNK, 4)
    grow = gcol.transpose(0, 1, 2, 4, 3)
    oa = gdn_branch(pa_ctx, pa_lat, _pad_rows(gdn_conv[0], SUBLANE), gcol, grow, gdn_onorm_g[0])

    mu = rwkv_mu[0]
    pch = jnp.stack([rwkv_k_k[0], rwkv_k_a[0], rwkv_r_k[0].reshape(C_B), rwkv_lnx_g[0], rwkv_lnx_b[0],
                     mu[:C_B], mu[C_B:2 * C_B], mu[2 * C_B:3 * C_B]])
    plo = _pad_rows(mu[3 * C_B:].reshape(3, LANE), SUBLANE)
    w0a0 = _pad_rows(jnp.concatenate([rwkv_w0[0], rwkv_a0[0]], axis=0), SUBLANE)
    zw = jnp.zeros((LORA_W, C_B), F32)
    w2pad = jnp.stack([jnp.concatenate([rwkv_w2[0, 0], zw]), jnp.concatenate([zw, rwkv_w2[0, 1]])])
    a2pad = jnp.stack([jnp.concatenate([rwkv_a2[0, 0], zw]), jnp.concatenate([zw, rwkv_a2[0, 1]])])
    ob = rwkv_branch(pb_ctx, pb_lat, pch, plo, w0a0, w2pad, a2pad, rwkv_g2[0])
    ob = _to_row_major(ob, rows)

    h1 = merge_residual(x2d, pg_lat, oa.reshape(bsz * seq, C_A), ob.reshape(bsz * seq, C_B), m_lat[2],
                        w_o_a[0].astype(BF16), w_o_b[0].astype(BF16), w_out[0].astype(BF16), seq)

    w_router = jnp.pad(jnp.concatenate([router_exp[0], router_grp[0]], axis=1),
                       ((0, 0), (0, LANE - N_EXPERTS - N_GROUPS)))
    b_router = jnp.pad(jnp.concatenate([router_exp_b[0], router_grp_b[0]]),
                       (0, LANE - N_EXPERTS - N_GROUPS)).reshape(1, LANE)
    t, cw = route(h1, norm_ffn_g[0], m_lat[3], m_lat[4], w_router, b_router, seq)
    out = experts_residual_norm(t, cw, h1, m_lat[5], final_norm_g,
                                moe_w_gate[0].reshape(N_EXPERTS, d, D_EXPERT).astype(BF16),
                                moe_w_up[0].reshape(N_EXPERTS, d, D_EXPERT).astype(BF16),
                                moe_w_down[0].reshape(N_EXPERTS, D_EXPERT, d).astype(BF16), seq)
    return out.reshape(bsz, seq, d)
```

```python
import functools

import jax
import jax.numpy as jnp
from jax import lax
from jax.experimental import pallas as pl
from jax.experimental.pallas import tpu as pltpu

F32 = jnp.float32
BF16 = jnp.bfloat16
HIGHEST = lax.Precision.HIGHEST

GRID_W = 64
H_A = 4
DK_A = 128
C_A = H_A * DK_A
SHORT_CONV = 5
CHUNK = 64
H_B = 8
N_B = 64
C_B = H_B * N_B
LORA_W = 64
LORA_A = 64
LORA_G = 128
N_GROUPS = 4
EXPERTS_PER_GROUP = 8
N_EXPERTS = N_GROUPS * EXPERTS_PER_GROUP
D_EXPERT = 256
NORM_EPS = 1e-6
LNX_EPS = 1e-5 * N_B

LANE = 128
SUBLANE = 8
VMEM_LIMIT = 56 * 1024 * 1024


def _cparams(*sem):
    return pltpu.CompilerParams(dimension_semantics=sem, vmem_limit_bytes=VMEM_LIMIT)


SINGLE = "bf16 operands, one MXU pass, f32 accumulation"
MM = SINGLE


def _operands(a, b, precision):
    if precision is SINGLE:
        return a.astype(BF16), b.astype(BF16), None
    return a, b, precision


def _dot(a, b, precision=None):
    a, b, precision = _operands(a, b, precision)
    return jnp.dot(a, b, preferred_element_type=F32, precision=precision)


def _dot_nt(a, b, precision=None):
    a, b, precision = _operands(a, b, precision)
    return lax.dot_general(a, b, (((1,), (1,)), ((), ())), preferred_element_type=F32, precision=precision)


def _dot_tn(a, b, precision=None):
    a, b, precision = _operands(a, b, precision)
    return lax.dot_general(a, b, (((0,), (0,)), ((), ())), preferred_element_type=F32, precision=precision)


def _split3(x):
    hi = x.astype(BF16)
    r1 = x - hi.astype(F32)
    mid = r1.astype(BF16)
    lo = (r1 - mid.astype(F32)).astype(BF16)
    return hi, mid, lo


def _dot_small_int_lhs(m, x):
    mb = m.astype(BF16)
    hi, mid, lo = _split3(x)
    return _dot(mb, hi) + _dot(mb, mid) + _dot(mb, lo)


def _dot_small_int_rhs(x, m):
    mb = m.astype(BF16)
    hi, mid, lo = _split3(x)
    return _dot(hi, mb) + _dot(mid, mb) + _dot(lo, mb)


def _sigmoid(x):
    return jax.nn.sigmoid(x)


def _silu(x):
    return x * jax.nn.sigmoid(x)


def _softplus(x):
    return jnp.maximum(x, 0.0) + jnp.log1p(jnp.exp(-jnp.abs(x)))


def _ada_kernel(c_ref, w_ref, b_ref, o_ref):
    s = _silu(c_ref[...])
    o_ref[...] = _dot(s, w_ref[...], HIGHEST) + b_ref[...]


def ada_modulation(cc, ada_w, ada_b):
    r, d = cc.shape
    n = ada_w.shape[1]
    tn = 1536
    return pl.pallas_call(
        _ada_kernel,
        grid=(n // tn,),
        in_specs=[pl.BlockSpec((r, d), lambda j: (0, 0)),
                  pl.BlockSpec((d, tn), lambda j: (0, j)),
                  pl.BlockSpec((1, tn), lambda j: (0, j))],
        out_specs=pl.BlockSpec((r, tn), lambda j: (0, j)),
        out_shape=jax.ShapeDtypeStruct((r, n), F32),
        compiler_params=_cparams("arbitrary"),
        name="ada_modulation",
    )(cc, ada_w, ada_b.reshape(1, n))


def _rms_modulate(x, g, shift, scale):
    ms = jnp.mean(x * x, axis=-1, keepdims=True)
    y = x * lax.rsqrt(ms + NORM_EPS) * g
    return y * (1.0 + scale) + shift


def _inproj_kernel(x_ref, g_ref, sh_ref, sc_ref, *rest, n_out):
    w_refs, o_refs = rest[:n_out], rest[n_out:]
    u = _rms_modulate(x_ref[...], g_ref[...], sh_ref[0], sc_ref[0]).astype(BF16)
    for w_ref, o_ref in zip(w_refs, o_refs):
        o_ref[...] = _dot(u, w_ref[...])


def norm_modulate_project(x2d, g, shift, scale, weights, rows_per_mod, tm=512):
    r, d = x2d.shape
    tm = min(tm, rows_per_mod)
    tiles_per_mod = rows_per_mod // tm
    mod_spec = pl.BlockSpec((1, 1, d), lambda i: (i // tiles_per_mod, 0, 0))
    in_specs = [pl.BlockSpec((tm, d), lambda i: (i, 0)),
                pl.BlockSpec((1, d), lambda i: (0, 0)), mod_spec, mod_spec]
    in_specs += [pl.BlockSpec(w.shape, lambda i: (0, 0)) for w in weights]
    return pl.pallas_call(
        functools.partial(_inproj_kernel, n_out=len(weights)),
        grid=(r // tm,),
        in_specs=in_specs,
        out_specs=[pl.BlockSpec((tm, w.shape[1]), lambda i: (i, 0)) for w in weights],
        out_shape=[jax.ShapeDtypeStruct((r, w.shape[1]), F32) for w in weights],
        compiler_params=_cparams("parallel"),
        name="norm_modulate_project",
    )(x2d, g.reshape(1, d), shift, scale, *weights)


def _iota2(shape, dim):
    return lax.broadcasted_iota(jnp.int32, shape, dim)


def _unit_tri_inverse(lm, nil):
    n = lm.shape[0]
    eye = jnp.where(_iota2((n, n), 0) == _iota2((n, n), 1), 1.0, 0.0).astype(F32)
    x = eye - lm
    p = _dot(lm, lm, MM)
    yield
    k = 2
    while 2 * k < nil:
        xp = _dot(jnp.concatenate([x, p], axis=0), p, MM)
        yield
        x = x + xp[:n]
        p = xp[n:]
        k *= 2
    res = x + _dot(x, p, MM)
    yield
    return res


def _interleave(chains):
    results = [None] * len(chains)
    live = list(range(len(chains)))
    while live:
        for i in list(live):
            try:
                next(chains[i])
            except StopIteration as done:
                results[i] = done.value
                live.remove(i)
    return results


def _fill_padded(pad_ref, x, n):
    zeros = jnp.zeros((SUBLANE, LANE), F32)
    pad_ref[0:SUBLANE, :] = zeros
    pad_ref[SUBLANE:SUBLANE + n, :] = x
    pad_ref[SUBLANE + n:2 * SUBLANE + n, :] = zeros


def _pair_masks(c):
    n = 2 * c
    brow, bcol = _iota2((n, n), 0), _iota2((n, n), 1)
    same = jnp.logical_not(jnp.logical_xor(brow >= c, bcol >= c))
    tr, tc = jnp.where(brow >= c, brow - c, brow), jnp.where(bcol >= c, bcol - c, bcol)
    fwd = brow < c
    incl = same & ((fwd & (tr >= tc)) | (jnp.logical_not(fwd) & (tr <= tc)))
    strict = same & ((fwd & (tr > tc)) | (jnp.logical_not(fwd) & (tr < tc)))
    return incl, strict


def _gdn_chunk_local(q, k, v, g4, r4):
    c = q.shape[0]
    n = 2 * c
    row, col = _iota2((c, c), 0), _iota2((c, c), 1)
    gc2 = jnp.concatenate(
        [jnp.sum(jnp.where(row >= col, r4[0:1, :], 0.0), axis=1, keepdims=True),
         jnp.sum(jnp.where(row <= col, r4[1:2, :], 0.0), axis=1, keepdims=True)], axis=0)
    tot_f = jnp.sum(r4[0:1, :], axis=1, keepdims=True)
    tot_b = jnp.sum(r4[1:2, :], axis=1, keepdims=True)
    tot2 = jnp.concatenate([jnp.broadcast_to(tot_f, (c, 1)), jnp.broadcast_to(tot_b, (c, 1))], axis=0)
    ri, cj = _iota2((c, n), 0), _iota2((c, n), 1)
    lane_f = cj < c
    cjm = jnp.where(lane_f, cj, cj - c)
    keep = (lane_f & (ri <= cjm)) | (jnp.logical_not(lane_f) & (ri >= cjm))
    gc_row2 = jnp.sum(jnp.where(keep, jnp.where(lane_f, g4[:, 0:1], g4[:, 1:2]), 0.0), axis=0, keepdims=True)
    incl, strict = _pair_masks(c)
    decay = jnp.where(incl, jnp.exp(jnp.where(incl, gc2 - gc_row2, 0.0)), 0.0)
    beta2 = jnp.concatenate([g4[:, 2:3], g4[:, 3:4]], axis=0)
    kk2 = jnp.concatenate([k, k], axis=0)
    qq2 = jnp.concatenate([q, q], axis=0)
    kb2 = kk2 * beta2
    m = _dot_nt(jnp.concatenate([kb2, qq2], axis=0), kk2, MM)
    yield
    t = yield from _unit_tri_inverse(jnp.where(strict, m[:n] * decay, 0.0), c)
    egc2 = jnp.exp(gc2)
    sol = _dot(t, jnp.concatenate([jnp.concatenate([v, v], axis=0) * beta2, kb2 * egc2], axis=1), MM)
    yield
    qk2 = jnp.where(incl, m[n:] * decay, 0.0)
    kdt2 = (kk2 * jnp.exp(tot2 - gc2)).T
    fwd_rows = _iota2(sol.shape, 0) < c
    ks_f = _dot(kdt2, jnp.where(fwd_rows, sol, 0.0), MM)
    ks_b = _dot(kdt2, jnp.where(fwd_rows, 0.0, sol), MM)
    yield
    return sol[:, :LANE], sol[:, LANE:], qq2 * egc2, qk2, ks_f, ks_b, jnp.exp(tot_f), jnp.exp(tot_b)


def _gdn_kernel(qc_ref, kc_ref, vc_ref, ql_ref, kl_ref, vl_ref, zl_ref, cwq_ref, cwk_ref, cwv_ref,
                gcol_ref, grow_ref, og_ref, o_ref, q_s, k_s, v_s, oacc_s, pad_s, st_s,
                u_s, wq_s, k0_s, k1_s, qk_s, ge_s, p_s, *, lc, ll):
    tile = min(256, lc, ll)

    def prep(x_ref, cw_ref, dst, off, n, mode):
        _fill_padded(pad_s, x_ref[0], n)
        cw = cw_ref[...]
        for t0 in range(0, n, tile):
            acc = cw[0:1, :] * pad_s[pl.ds(SUBLANE - 2 + t0, tile), :]
            for j in range(1, SHORT_CONV):
                acc = acc + cw[j:j + 1, :] * pad_s[pl.ds(SUBLANE - 2 + j + t0, tile), :]
            y = _silu(acc)
            if mode != "v":
                y = y * lax.rsqrt(jnp.sum(y * y, axis=-1, keepdims=True) + NORM_EPS)
            if mode == "q":
                y = y * (DK_A ** -0.5)
            dst[pl.ds(off + t0, tile), :] = y

    prep(qc_ref, cwq_ref, q_s, 0, lc, "q")
    prep(kc_ref, cwk_ref, k_s, 0, lc, "k")
    prep(vc_ref, cwv_ref, v_s, 0, lc, "v")
    prep(ql_ref, cwq_ref, q_s, lc, ll, "q")
    prep(kl_ref, cwk_ref, k_s, lc, ll, "k")
    prep(vl_ref, cwv_ref, v_s, lc, ll, "v")

    oacc_s[...] = jnp.zeros(oacc_s.shape, F32)
    st_s[...] = jnp.zeros(st_s.shape, F32)
    nc_ctx = lc // CHUNK
    nc_tot = (lc + ll) // CHUNK

    c = CHUNK

    width = max(w for w in (12, 6, 4, 3, 2, 1) if nc_tot % w == 0)

    def local(it, carry):
        cis = [it * width + j for j in range(width)]
        chains = []
        for ci in cis:
            rows = pl.ds(pl.multiple_of(ci * c, c), c)
            chains.append(_gdn_chunk_local(q_s[rows, :], k_s[rows, :], v_s[rows, :],
                                           gcol_ref[0, 0, ci], grow_ref[0, 0, ci]))
        for ci, (u2, w2, qd2, qk2, ks_f, ks_b, ge_f, ge_b) in zip(cis, _interleave(chains)):
            u_s[ci] = u2
            wq_s[ci, 0] = jnp.concatenate([w2[:c], qd2[:c]], axis=0).astype(BF16)
            wq_s[ci, 1] = jnp.concatenate([w2[c:], qd2[c:]], axis=0).astype(BF16)
            for d, ks in enumerate((ks_f, ks_b)):
                k0_s[ci, d] = ks[:, :LANE]
                k1_s[ci, d] = ks[:, LANE:].astype(BF16)
            qk_s[ci] = qk2.astype(BF16)
            ge_s[ci, 0:1, :] = jnp.broadcast_to(ge_f, (1, LANE))
            ge_s[ci, 1:2, :] = jnp.broadcast_to(ge_b, (1, LANE))
        return carry

    lax.fori_loop(0, nc_tot // width, local, 0)

    def bwd_chunk(s):
        return jnp.where(s < nc_ctx, nc_ctx - 1 - s, nc_tot - 1 - (s - nc_ctx))

    def outputs(s, keep):
        cb = bwd_chunk(s)
        p_f, p_b = p_s[0], p_s[1]
        v_new = jnp.concatenate([u_s[s, 0:c, :] - p_f[:c], u_s[cb, c:2 * c, :] - p_b[:c]], axis=0)
        qk = jnp.concatenate([qk_s[s, 0:c, :], qk_s[cb, c:2 * c, :]], axis=0)
        o2 = jnp.concatenate([p_f[c:], p_b[c:]], axis=0) + _dot(qk, v_new.astype(BF16))
        if keep is not None:
            o2 = jnp.where(keep, o2, 0.0)
        rows_f = pl.ds(pl.multiple_of(s * c, c), c)
        rows_b = pl.ds(pl.multiple_of(cb * c, c), c)
        oacc_s[rows_f, :] = oacc_s[rows_f, :] + o2[:c]
        oacc_s[rows_b, :] = oacc_s[rows_b, :] + o2[c:]

    def step(s, carry):
        outputs(jnp.maximum(s - 1, 0), s > 0)
        cb = bwd_chunk(s)
        s_f, s_b = st_s[0], st_s[1]
        sb_f, sb_b = s_f.astype(BF16), s_b.astype(BF16)
        p_s[0] = _dot(wq_s[s, 0], sb_f)
        p_s[1] = _dot(wq_s[cb, 1], sb_b)
        st_s[0] = s_f * ge_s[s, 0:1, :] + k0_s[s, 0] - _dot(k1_s[s, 0], sb_f)
        st_s[1] = s_b * ge_s[cb, 1:2, :] + k0_s[cb, 1] - _dot(k1_s[cb, 1], sb_b)
        return carry

    p_s[...] = jnp.zeros(p_s.shape, F32)
    lax.fori_loop(0, nc_tot, step, 0)
    outputs(nc_tot - 1, None)

    og = og_ref[...]
    for t0 in range(0, ll, tile):
        o = oacc_s[pl.ds(lc + t0, tile), :]
        o = o * lax.rsqrt(jnp.mean(o * o, axis=-1, keepdims=True) + NORM_EPS) * og
        o_ref[0, pl.ds(t0, tile), :] = o * _silu(zl_ref[0, pl.ds(t0, tile), :])


def gdn_branch(pa_ctx, pa_lat, conv_w, gcol, grow, onorm_g):
    b, lc, _ = pa_ctx.shape
    ll = pa_lat.shape[1]
    nc = (lc + ll) // CHUNK

    def col(blk, n):
        return pl.BlockSpec((1, n, LANE), lambda i, h: (i, 0, blk * H_A + h))

    def cw(blk):
        return pl.BlockSpec((SUBLANE, LANE), lambda i, h: (0, blk * H_A + h))

    in_specs = [col(0, lc), col(1, lc), col(2, lc), col(0, ll), col(1, ll), col(2, ll), col(3, ll),
                cw(0), cw(1), cw(2),
                pl.BlockSpec((1, 1, nc, CHUNK, 4), lambda i, h: (i, h, 0, 0, 0)),
                pl.BlockSpec((1, 1, nc, 4, CHUNK), lambda i, h: (i, h, 0, 0, 0)),
                pl.BlockSpec((1, LANE), lambda i, h: (0, 0))]
    lt = lc + ll
    return pl.pallas_call(
        functools.partial(_gdn_kernel, lc=lc, ll=ll),
        grid=(b, H_A),
        in_specs=in_specs,
        out_specs=pl.BlockSpec((1, ll, LANE), lambda i, h: (i, 0, h)),
        out_shape=jax.ShapeDtypeStruct((b, ll, C_A), F32),
        scratch_shapes=[pltpu.VMEM((lt, LANE), F32)] * 4
        + [pltpu.VMEM((max(lc, ll) + 2 * SUBLANE, LANE), F32), pltpu.VMEM((2, DK_A, DK_A), F32),
           pltpu.VMEM((nc, 2 * CHUNK, LANE), F32), pltpu.VMEM((nc, 2, 2 * CHUNK, LANE), BF16),
           pltpu.VMEM((nc, 2, DK_A, DK_A), F32), pltpu.VMEM((nc, 2, DK_A, DK_A), BF16),
           pltpu.VMEM((nc, 2 * CHUNK, 2 * CHUNK), BF16),
           pltpu.VMEM((nc, SUBLANE, LANE), F32), pltpu.VMEM((2, 2 * CHUNK, LANE), F32)],
        compiler_params=_cparams("parallel", "parallel"),
        name="gdn_branch",
    )(pa_ctx, pa_ctx, pa_ctx, pa_lat, pa_lat, pa_lat, pa_lat, conv_w, conv_w, conv_w, gcol, grow,
      onorm_g.reshape(1, LANE))


def _stack_heads(x, head0):
    return jnp.concatenate([jnp.where(head0, x, 0.0), jnp.where(head0, 0.0, x)], axis=0)


def _rwkv_chunk_local(r, v, a, logw, kdir, b, lower):
    c = r.shape[0]
    n = 2 * c
    row, col = _iota2((c, c), 0), _iota2((c, c), 1)
    tri = jnp.where(row >= col if lower else row <= col, 1.0, 0.0).astype(F32)
    lcum = _dot_small_int_lhs(tri, logw)
    tot = jnp.sum(logw, axis=0, keepdims=True)
    e_in = jnp.exp(lcum)
    e_ex = jnp.exp(lcum - logw)
    e_neg = jnp.exp(-lcum)
    e_rem = jnp.exp(tot - lcum)
    head0 = _iota2((c, LANE), 1) < N_B
    a2 = _stack_heads(a * e_ex, head0)
    r2 = _stack_heads(r * e_in, head0)
    b2 = _stack_heads(b * e_neg, head0)
    k2 = _stack_heads(kdir * e_neg, head0)
    v2 = _stack_heads(v, head0)
    m = _dot_nt(jnp.concatenate([a2, r2], axis=0), jnp.concatenate([b2, k2], axis=0), MM)
    yield
    brow, bcol = _iota2((n, n), 0), _iota2((n, n), 1)
    same = jnp.logical_not(jnp.logical_xor(brow >= c, bcol >= c))
    tr, tc = jnp.where(brow >= c, brow - c, brow), jnp.where(bcol >= c, bcol - c, bcol)
    strict = same & ((tr > tc) if lower else (tr < tc))
    incl = same & ((tr >= tc) if lower else (tr <= tc))
    ak = jnp.where(strict, m[:n, n:], 0.0)
    rb = jnp.where(incl, m[n:, :n], 0.0)
    rk = jnp.where(incl, m[n:, n:], 0.0)
    akv_rkv = _dot(jnp.concatenate([ak, rk], axis=0), v2, MM)
    kv0 = _dot_tn(_stack_heads(kdir * e_rem, head0), v2, MM)
    t = yield from _unit_tri_inverse(jnp.where(strict, -m[:n, :n], 0.0), c)
    tw = _dot(t, jnp.concatenate([a2, akv_rkv[:n]], axis=1), MM)
    yield
    x = jnp.concatenate([tw[:, :LANE], r2], axis=0)
    eg = jnp.broadcast_to(jnp.exp(tot), (LANE, LANE)).T
    bu = _dot(_stack_heads(b * e_rem, head0).T, tw, MM)
    yield
    return x, rb, bu[:, :LANE], tw[:, LANE:], akv_rkv[n:], kv0 + bu[:, LANE:], eg


def _rwkv_kernel(*refs, lc, ll):
    ctx_refs, lat_refs = refs[0:6], refs[6:12]
    pch_ref, plo_ref, w0a0_ref, w2_ref, a2_ref, g2_ref, o_ref = refs[12:19]
    r_s, v_s, a_s, lw_s, kd_s, b_s, gate_s, bonus_s, y_s, pad_s, st_s, x_s, rb_s, m1_s, f_s, p_s = refs[19:]
    tile = min(256, lc, ll)
    pch = pch_ref[...]
    k_k, k_a, r_k, lnx_g, lnx_b = (pch[i:i + 1, :] for i in range(5))
    mus = [pch[5:6, :], pch[6:7, :], pch[7:8, :]] + [plo_ref[i:i + 1, :] for i in range(3)]
    w0a0 = w0a0_ref[...]
    rr, cc = _iota2((LANE, LANE), 0), _iota2((LANE, LANE), 1)
    seg = jnp.where(jnp.logical_xor(rr >= N_B, cc >= N_B), 0.0, 1.0).astype(F32)

    def prep(src_refs, off, n, is_lat):
        for j in range(6):
            _fill_padded(pad_s.at[j], src_refs[j][0], n)
        for t0 in range(0, n, tile):
            mixed = []
            for j in range(6):
                x = pad_s[j, pl.ds(SUBLANE + t0, tile), :]
                nb = pad_s[j, pl.ds(SUBLANE - 1 + t0, tile), :] + pad_s[j, pl.ds(SUBLANE + 1 + t0, tile), :]
                mixed.append(x + (0.5 * nb - x) * mus[j])
            r, k, v, wl, al, gl = mixed
            wl = jnp.tanh(wl)
            kk = k * k_k
            kk = kk * lax.rsqrt(_dot_small_int_rhs(kk * kk, seg) + NORM_EPS)
            ksum = jnp.zeros_like(k)
            for d in range(2):
                w_log = -_softplus(-(w0a0[d:d + 1, :] + _dot(wl, w2_ref[d], MM))) - 0.5
                iclr = _sigmoid(w0a0[2 + d:3 + d, :] + _dot(al, a2_ref[d], MM))
                kdir = k * (1.0 + (iclr - 1.0) * k_a)
                ksum = ksum + kdir
                lw_s[d, pl.ds(off + t0, tile), :] = -jnp.exp(w_log)
                kd_s[d, pl.ds(off + t0, tile), :] = kdir
                b_s[d, pl.ds(off + t0, tile), :] = kk * iclr
            r_s[pl.ds(off + t0, tile), :] = r
            v_s[pl.ds(off + t0, tile), :] = v
            a_s[pl.ds(off + t0, tile), :] = -kk
            if is_lat:
                gate_s[pl.ds(t0, tile), :] = _dot(_sigmoid(gl), g2_ref[...], MM)
                bonus_s[pl.ds(t0, tile), :] = _dot_small_int_rhs(r * ksum * r_k, seg) * v

    prep(ctx_refs, 0, lc, False)
    prep(lat_refs, lc, ll, True)

    y_s[...] = jnp.zeros(y_s.shape, F32)
    st_s[...] = jnp.zeros(st_s.shape, F32)
    p_s[...] = jnp.zeros(p_s.shape, F32)
    nc_ctx = lc // CHUNK
    nc_tot = (lc + ll) // CHUNK

    group = x_s.shape[0]
    n2 = 2 * CHUNK

    def chunk_of(s, d):
        cb = jnp.where(s < nc_ctx, nc_ctx - 1 - s, nc_tot - 1 - (s - nc_ctx))
        return pl.ds(pl.multiple_of((cb if d else s) * CHUNK, CHUNK), CHUNK)

    width = max(w for w in (6, 3, 2, 1) if group % w == 0)

    def run_group(gi, carry):
        def local(it, c2):
            slots = [(it * width + j, d) for j in range(width) for d in range(2)]
            chains = []
            for i, d in slots:
                rows = chunk_of(gi * group + i, d)
                chains.append(_rwkv_chunk_local(r_s[rows, :], v_s[rows, :], a_s[rows, :], lw_s[d, rows, :],
                                                kd_s[d, rows, :], b_s[d, rows, :], lower=(d == 0)))
            for (i, d), (x, rb, m1, u0, rkv, hc, eg) in zip(slots, _interleave(chains)):
                x_s[i, d] = x.astype(BF16)
                rb_s[i, d] = rb.astype(BF16)
                m1_s[i, d] = m1.astype(BF16)
                f_s[i, d, 0] = u0
                f_s[i, d, 1] = rkv
                f_s[i, d, 2] = hc
                f_s[i, d, 3] = eg
            return c2

        lax.fori_loop(0, group // width, local, 0)

        def outputs(i, keep):
            for d in range(2):
                rows = chunk_of(gi * group + i, d)
                u = p_s[d, 0:n2, :] + f_s[i, d, 0]
                y2 = p_s[d, n2:2 * n2, :] + _dot(rb_s[i, d], u.astype(BF16)) + f_s[i, d, 1]
                y = y2[:CHUNK] + y2[CHUNK:]
                if keep is not None:
                    y = jnp.where(keep, y, 0.0)
                y_s[rows, :] = y_s[rows, :] + y

        def step(i, c2):
            outputs(jnp.maximum(i - 1, 0), i > 0)
            hs = [st_s[d] for d in range(2)]
            hbs = [h.astype(BF16) for h in hs]
            for d in range(2):
                p_s[d] = _dot(x_s[i, d], hbs[d])
            for d in range(2):
                st_s[d] = f_s[i, d, 3] * hs[d] + f_s[i, d, 2] + _dot(m1_s[i, d], hbs[d])
            return c2

        lax.fori_loop(0, group, step, 0)
        outputs(group - 1, None)
        return carry

    lax.fori_loop(0, nc_tot // group, run_group, 0)

    inv_n = 1.0 / N_B
    for t0 in range(0, ll, tile):
        yf = y_s[pl.ds(lc + t0, tile), :]
        cen = yf - _dot_small_int_rhs(yf, seg) * inv_n
        var = _dot_small_int_rhs(cen * cen, seg) * inv_n
        y = cen * lax.rsqrt(var + LNX_EPS) * lnx_g + lnx_b
        o_ref[0, pl.ds(t0, tile), :] = (y + bonus_s[pl.ds(t0, tile), :]) * gate_s[pl.ds(t0, tile), :]


def rwkv_branch(pb_ctx, pb_lat, pch, plo, w0a0, w2pad, a2pad, g2):
    b, lc, _ = pb_ctx.shape
    ll = pb_lat.shape[1]
    lt = lc + ll
    pairs = C_B // LANE

    def col(blk, n, per_pair):
        if per_pair:
            return pl.BlockSpec((1, n, LANE), lambda i, p: (i, 0, blk * pairs + p))
        return pl.BlockSpec((1, n, LANE), lambda i, p: (i, 0, 3 * pairs + blk))

    def cols(n):
        return [col(0, n, True), col(1, n, True), col(2, n, True), col(0, n, False), col(1, n, False),
                col(2, n, False)]

    in_specs = cols(lc) + cols(ll) + [
        pl.BlockSpec((SUBLANE, LANE), lambda i, p: (0, p)),
        pl.BlockSpec((SUBLANE, LANE), lambda i, p: (0, 0)),
        pl.BlockSpec((SUBLANE, LANE), lambda i, p: (0, p)),
        pl.BlockSpec((2, LANE, LANE), lambda i, p: (0, 0, p)),
        pl.BlockSpec((2, LANE, LANE), lambda i, p: (0, 0, p)),
        pl.BlockSpec((LANE, LANE), lambda i, p: (0, p))]
    seq = pltpu.VMEM((lt, LANE), F32)
    seq2 = pltpu.VMEM((2, lt, LANE), F32)
    lat = pltpu.VMEM((ll, LANE), F32)
    nc = lt // CHUNK
    group = max(g for g in (6, 4, 3, 2, 1) if nc % g == 0)
    n2 = 2 * CHUNK
    return pl.pallas_call(
        functools.partial(_rwkv_kernel, lc=lc, ll=ll),
        grid=(b, pairs),
        in_specs=in_specs,
        out_specs=pl.BlockSpec((1, ll, LANE), lambda i, p: (i, 0, p)),
        out_shape=jax.ShapeDtypeStruct((b, ll, C_B), F32),
        scratch_shapes=[seq, seq, seq, seq2, seq2, seq2, lat, lat, seq,
                        pltpu.VMEM((6, max(lc, ll) + 2 * SUBLANE, LANE), F32),
                        pltpu.VMEM((2, LANE, LANE), F32),
                        pltpu.VMEM((group, 2, 2 * n2, LANE), BF16), pltpu.VMEM((group, 2, n2, n2), BF16),
                        pltpu.VMEM((group, 2, LANE, n2), BF16), pltpu.VMEM((group, 2, 4, n2, LANE), F32),
                        pltpu.VMEM((2, 2 * n2, LANE), F32)],
        compiler_params=_cparams("parallel", "parallel"),
        name="rwkv_branch",
    )(*([pb_ctx] * 6), *([pb_lat] * 6), pch, plo, w0a0, w2pad, a2pad, g2)


def _merge_kernel(x_ref, pg_ref, oa_ref, ob_ref, m2_ref, woa_ref, wob_ref, wout_ref, o_ref):
    d = x_ref.shape[1]
    ya = _dot(oa_ref[...].astype(BF16), woa_ref[...])
    yb = _dot(ob_ref[...].astype(BF16), wob_ref[...])
    y = _sigmoid(pg_ref[:, 0:d]) * ya + _sigmoid(pg_ref[:, d:2 * d]) * yb
    o_ref[...] = x_ref[...] + m2_ref[0] * _dot(y.astype(BF16), wout_ref[...])


def merge_residual(x2d, pg, oa, ob, m2, w_o_a, w_o_b, w_out, rows_per_mod, tm=512):
    r, d = x2d.shape

    def rows(n):
        return pl.BlockSpec((tm, n), lambda i: (i, 0))

    def full(w):
        return pl.BlockSpec(w.shape, lambda i: (0, 0))

    tiles_per_mod = rows_per_mod // tm
    return pl.pallas_call(
        _merge_kernel,
        grid=(r // tm,),
        in_specs=[rows(d), rows(2 * d), rows(oa.shape[1]), rows(ob.shape[1]),
                  pl.BlockSpec((1, 1, d), lambda i: (i // tiles_per_mod, 0, 0)),
                  full(w_o_a), full(w_o_b), full(w_out)],
        out_specs=rows(d),
        out_shape=jax.ShapeDtypeStruct((r, d), F32),
        compiler_params=_cparams("parallel"),
        name="merge_residual",
    )(x2d, pg, oa, ob, m2, w_o_a, w_o_b, w_out)


ROUTER_GROUP_LANE0 = N_EXPERTS


def _route_kernel(h_ref, g_ref, sh_ref, sc_ref, wr_ref, br_ref, t_ref, cw_ref):
    t = _rms_modulate(h_ref[...], g_ref[...], sh_ref[0], sc_ref[0])
    t_ref[...] = t.astype(BF16)
    lg = _dot(t, wr_ref[...], HIGHEST) + br_ref[...]
    lane = _iota2(lg.shape, 1)
    lane_f = lane.astype(F32)
    neg = jnp.float32(-jnp.inf)
    big = jnp.float32(2 * LANE)
    is_grp = (lane >= ROUTER_GROUP_LANE0) & (lane < ROUTER_GROUP_LANE0 + N_GROUPS)
    lgg = jnp.where(is_grp, lg, neg)
    mg = jnp.max(lgg, axis=-1, keepdims=True)
    p_grp = 1.0 / jnp.sum(jnp.where(is_grp, jnp.exp(lgg - mg), 0.0), axis=-1, keepdims=True)
    g_sel = jnp.min(jnp.where(lgg == mg, lane_f, big), axis=-1, keepdims=True) - ROUTER_GROUP_LANE0
    grp_of_lane = lax.shift_right_logical(lane, EXPERTS_PER_GROUP.bit_length() - 1).astype(F32)
    in_grp = (lane < N_EXPERTS) & (grp_of_lane == g_sel)
    l1 = jnp.where(in_grp, lg, neg)
    top1 = jnp.max(l1, axis=-1, keepdims=True)
    idx1 = jnp.min(jnp.where(l1 == top1, lane_f, big), axis=-1, keepdims=True)
    l2 = jnp.where(in_grp & (lane_f != idx1), lg, neg)
    top2 = jnp.max(l2, axis=-1, keepdims=True)
    idx2 = jnp.min(jnp.where(l2 == top2, lane_f, big), axis=-1, keepdims=True)
    e2 = jnp.exp(top2 - top1)
    w1 = p_grp / (1.0 + e2)
    cw_ref[...] = jnp.where(lane_f == idx1, w1, jnp.where(lane_f == idx2, w1 * e2, 0.0))


def route(h2d, g, shift, scale, w_router, b_router, rows_per_mod, tm=512):
    r, d = h2d.shape
    tiles_per_mod = rows_per_mod // tm
    mod_spec = pl.BlockSpec((1, 1, d), lambda i: (i // tiles_per_mod, 0, 0))
    return pl.pallas_call(
        _route_kernel,
        grid=(r // tm,),
        in_specs=[pl.BlockSpec((tm, d), lambda i: (i, 0)), pl.BlockSpec((1, d), lambda i: (0, 0)),
                  mod_spec, mod_spec,
                  pl.BlockSpec((d, LANE), lambda i: (0, 0)), pl.BlockSpec((1, LANE), lambda i: (0, 0))],
        out_specs=[pl.BlockSpec((tm, d), lambda i: (i, 0)), pl.BlockSpec((tm, LANE), lambda i: (i, 0))],
        out_shape=[jax.ShapeDtypeStruct((r, d), BF16), jax.ShapeDtypeStruct((r, LANE), F32)],
        compiler_params=_cparams("parallel"),
        name="moe_route",
    )(h2d, g.reshape(1, d), shift, scale, w_router, b_router)


EXPERTS_PER_STEP = 4


def _experts_kernel(t_ref, cw_ref, h_ref, m5_ref, fg_ref, wg_ref, wu_ref, wd_ref, o_ref, acc_ref):
    s = pl.program_id(1)

    @pl.when(s == 0)
    def _():
        acc_ref[...] = jnp.zeros(acc_ref.shape, F32)

    t = t_ref[...]
    cw = cw_ref[...]
    lane = _iota2(cw.shape, 1)
    hids = []
    for j in range(EXPERTS_PER_STEP):
        w = jnp.sum(jnp.where(lane == s * EXPERTS_PER_STEP + j, cw, 0.0), axis=-1, keepdims=True)
        hids.append((_silu(_dot(t, wg_ref[j])) * _dot(t, wu_ref[j]) * w).astype(BF16))
    acc_ref[...] += _dot(jnp.concatenate(hids, axis=1), wd_ref[...])

    @pl.when(s == pl.num_programs(1) - 1)
    def _():
        h2 = h_ref[...] + m5_ref[0] * acc_ref[...]
        ms = jnp.mean(h2 * h2, axis=-1, keepdims=True)
        o_ref[...] = h2 * lax.rsqrt(ms + NORM_EPS) * fg_ref[...]


def experts_residual_norm(t, cw, h2d, m5, final_g, w_gate, w_up, w_down, rows_per_mod, tm=1024):
    r, d = h2d.shape
    ne, _, f = w_gate.shape
    eps = EXPERTS_PER_STEP
    tiles_per_mod = rows_per_mod // tm
    return pl.pallas_call(
        _experts_kernel,
        grid=(r // tm, ne // eps),
        in_specs=[pl.BlockSpec((tm, d), lambda i, e: (i, 0)), pl.BlockSpec((tm, LANE), lambda i, e: (i, 0)),
                  pl.BlockSpec((tm, d), lambda i, e: (i, 0)),
                  pl.BlockSpec((1, 1, d), lambda i, e: (i // tiles_per_mod, 0, 0)),
                  pl.BlockSpec((1, d), lambda i, e: (0, 0)),
                  pl.BlockSpec((eps, d, f), lambda i, e: (e, 0, 0)), pl.BlockSpec((eps, d, f), lambda i, e: (e, 0, 0)),
                  pl.BlockSpec((eps * f, d), lambda i, e: (e, 0))],
        out_specs=pl.BlockSpec((tm, d), lambda i, e: (i, 0)),
        out_shape=jax.ShapeDtypeStruct((r, d), F32),
        scratch_shapes=[pltpu.VMEM((tm, d), F32)],
        compiler_params=_cparams("parallel", "arbitrary"),
        name="moe_experts",
    )(t, cw, h2d, m5, final_g.reshape(1, d), w_gate, w_up, w_down)


def _to_col_major(x, rows):
    b, l, c = x.shape
    return x.reshape(b, rows, GRID_W, c).transpose(0, 2, 1, 3).reshape(b, l, c)


def _to_row_major(x, rows):
    b, l, c = x.shape
    return x.reshape(b, GRID_W, rows, c).transpose(0, 2, 1, 3).reshape(b, l, c)


def _pad_rows(a, n):
    return jnp.pad(a, ((0, n - a.shape[0]),) + ((0, 0),) * (a.ndim - 1))


def _gdn_gates(pa, a_log, dt_bias):
    b, l, _ = pa.shape
    ab = pa[..., 4 * C_A:4 * C_A + 4 * H_A]
    a = ab[..., :2 * H_A].reshape(b, l, 2, H_A)
    bt = ab[..., 2 * H_A:].reshape(b, l, 2, H_A)
    g = -jnp.exp(a_log) * jax.nn.softplus(a + dt_bias)
    beta = jax.nn.sigmoid(bt)
    return jnp.concatenate([g, beta], axis=2).transpose(0, 3, 1, 2)


def kernel(x, c, ctx, c_ctx, ada_w, ada_b, norm_mix_g, norm_ffn_g, w_in, gdn_conv, gdn_a_log, gdn_dt_bias,
           gdn_onorm_g, rwkv_mu, rwkv_w0, rwkv_w2, rwkv_a0, rwkv_a2, rwkv_g2, rwkv_k_k, rwkv_k_a, rwkv_r_k,
           rwkv_lnx_g, rwkv_lnx_b, w_o_a, w_o_b, w_out, router_grp, router_grp_b, router_exp, router_exp_b,
           moe_w_gate, moe_w_up, moe_w_down, final_norm_g):
    bsz, seq, d = x.shape
    lc = ctx.shape[1]
    rows = seq // GRID_W
    a_cols = 4 * C_A + 4 * H_A
    b_cols = 3 * C_B + 2 * LORA_W + 2 * LORA_A + LORA_G

    cc = _pad_rows(jnp.concatenate([c, c_ctx[None]], axis=0), 2 * SUBLANE)
    mod = ada_modulation(cc, ada_w[0], ada_b[0])
    m_lat = [mod[:bsz, i * d:(i + 1) * d].reshape(bsz, 1, d) for i in range(6)]
    m_ctx = [mod[bsz:bsz + 1, i * d:(i + 1) * d].reshape(1, 1, d) for i in range(2)]

    w = w_in[0]
    w_a = jnp.pad(w[:, :a_cols], ((0, 0), (0, 4 * C_A + LANE - a_cols))).astype(BF16)
    w_b = w[:, a_cols:a_cols + b_cols].astype(BF16)
    w_g = w[:, a_cols + b_cols:].astype(BF16)

    x2d = x.reshape(bsz * seq, d)
    pa_lat, pg_lat = norm_modulate_project(x2d, norm_mix_g[0], m_lat[0], m_lat[1], [w_a, w_g], seq)
    (pb_lat,) = norm_modulate_project(_to_col_major(x, rows).reshape(bsz * seq, d), norm_mix_g[0],
                                      m_lat[0], m_lat[1], [w_b], seq)
    pa_ctx, pb_ctx = norm_modulate_project(ctx.reshape(bsz * lc, d), norm_mix_g[0], m_ctx[0], m_ctx[1],
                                           [w_a, w_b], bsz * lc)
    pa_lat = pa_lat.reshape(bsz, seq, -1)
    pb_lat = pb_lat.reshape(bsz, seq, -1)
    pa_ctx = pa_ctx.reshape(bsz, lc, -1)
    pb_ctx = pb_ctx.reshape(bsz, lc, -1)

    gates = jnp.concatenate([_gdn_gates(pa_ctx, gdn_a_log[0], gdn_dt_bias[0]),
                             _gdn_gates(pa_lat, gdn_a_log[0], gdn_dt_bias[0])], axis=2)
    nc = (lc + seq) // CHUNK
    gcol = gates.reshape(bsz, H_A, nc, CHUNK, 4)
    grow = gcol.transpose(0, 1, 2, 4, 3)
    oa = gdn_branch(pa_ctx, pa_lat, _pad_rows(gdn_conv[0], SUBLANE), gcol, grow, gdn_onorm_g[0])

    mu = rwkv_mu[0]
    pch = jnp.stack([rwkv_k_k[0], rwkv_k_a[0], rwkv_r_k[0].reshape(C_B), rwkv_lnx_g[0], rwkv_lnx_b[0],
                     mu[:C_B], mu[C_B:2 * C_B], mu[2 * C_B:3 * C_B]])
    plo = _pad_rows(mu[3 * C_B:].reshape(3, LANE), SUBLANE)
    w0a0 = _pad_rows(jnp.concatenate([rwkv_w0[0], rwkv_a0[0]], axis=0), SUBLANE)
    zw = jnp.zeros((LORA_W, C_B), F32)
    w2pad = jnp.stack([jnp.concatenate([rwkv_w2[0, 0], zw]), jnp.concatenate([zw, rwkv_w2[0, 1]])])
    a2pad = jnp.stack([jnp.concatenate([rwkv_a2[0, 0], zw]), jnp.concatenate([zw, rwkv_a2[0, 1]])])
    ob = rwkv_branch(pb_ctx, pb_lat, pch, plo, w0a0, w2pad, a2pad, rwkv_g2[0])
    ob = _to_row_major(ob, rows)

    h1 = merge_residual(x2d, pg_lat, oa.reshape(bsz * seq, C_A), ob.reshape(bsz * seq, C_B), m_lat[2],
                        w_o_a[0].astype(BF16), w_o_b[0].astype(BF16), w_out[0].astype(BF16), seq)

    w_router = jnp.pad(jnp.concatenate([router_exp[0], router_grp[0]], axis=1),
                       ((0, 0), (0, LANE - N_EXPERTS - N_GROUPS)))
    b_router = jnp.pad(jnp.concatenate([router_exp_b[0], router_grp_b[0]]),
                       (0, LANE - N_EXPERTS - N_GROUPS)).reshape(1, LANE)
    t, cw = route(h1, norm_ffn_g[0], m_lat[3], m_lat[4], w_router, b_router, seq)
    out = experts_residual_norm(t, cw, h1, m_lat[5], final_norm_g,
                                moe_w_gate[0].reshape(N_EXPERTS, d, D_EXPERT).astype(BF16),
                                moe_w_up[0].reshape(N_EXPERTS, d, D_EXPERT).astype(BF16),
                                moe_w_down[0].reshape(N_EXPERTS * D_EXPERT, d).astype(BF16), seq)
    return out.reshape(bsz, seq, d)
```

```python
import functools

import jax
import jax.numpy as jnp
from jax import lax
from jax.experimental import pallas as pl
from jax.experimental.pallas import tpu as pltpu

F32 = jnp.float32
BF16 = jnp.bfloat16
HIGHEST = lax.Precision.HIGHEST

GRID_W = 64
H_A = 4
DK_A = 128
C_A = H_A * DK_A
SHORT_CONV = 5
CHUNK = 64
H_B = 8
N_B = 64
C_B = H_B * N_B
LORA_W = 64
LORA_A = 64
LORA_G = 128
N_GROUPS = 4
EXPERTS_PER_GROUP = 8
N_EXPERTS = N_GROUPS * EXPERTS_PER_GROUP
D_EXPERT = 256
NORM_EPS = 1e-6
LNX_EPS = 1e-5 * N_B

LANE = 128
SUBLANE = 8
VMEM_LIMIT = 56 * 1024 * 1024


def _cparams(*sem):
    return pltpu.CompilerParams(dimension_semantics=sem, vmem_limit_bytes=VMEM_LIMIT)


SINGLE = "bf16 operands, one MXU pass, f32 accumulation"
MM = SINGLE


def _operands(a, b, precision):
    if precision is SINGLE:
        return a.astype(BF16), b.astype(BF16), None
    return a, b, precision


def _dot(a, b, precision=None):
    a, b, precision = _operands(a, b, precision)
    return jnp.dot(a, b, preferred_element_type=F32, precision=precision)


def _dot_nt(a, b, precision=None):
    a, b, precision = _operands(a, b, precision)
    return lax.dot_general(a, b, (((1,), (1,)), ((), ())), preferred_element_type=F32, precision=precision)


def _dot_tn(a, b, precision=None):
    a, b, precision = _operands(a, b, precision)
    return lax.dot_general(a, b, (((0,), (0,)), ((), ())), preferred_element_type=F32, precision=precision)


def _split3(x):
    hi = x.astype(BF16)
    r1 = x - hi.astype(F32)
    mid = r1.astype(BF16)
    lo = (r1 - mid.astype(F32)).astype(BF16)
    return hi, mid, lo


def _dot_small_int_lhs(m, x):
    mb = m.astype(BF16)
    hi, mid, lo = _split3(x)
    return _dot(mb, hi) + _dot(mb, mid) + _dot(mb, lo)


def _dot_small_int_rhs(x, m):
    mb = m.astype(BF16)
    hi, mid, lo = _split3(x)
    return _dot(hi, mb) + _dot(mid, mb) + _dot(lo, mb)


def _sigmoid(x):
    return jax.nn.sigmoid(x)


def _silu(x):
    return x * jax.nn.sigmoid(x)


def _softplus(x):
    return jnp.maximum(x, 0.0) + jnp.log1p(jnp.exp(-jnp.abs(x)))


def _ada_kernel(c_ref, w_ref, b_ref, o_ref):
    s = _silu(c_ref[...])
    o_ref[...] = _dot(s, w_ref[...], HIGHEST) + b_ref[...]


def ada_modulation(cc, ada_w, ada_b):
    r, d = cc.shape
    n = ada_w.shape[1]
    tn = 1536
    return pl.pallas_call(
        _ada_kernel,
        grid=(n // tn,),
        in_specs=[pl.BlockSpec((r, d), lambda j: (0, 0)),
                  pl.BlockSpec((d, tn), lambda j: (0, j)),
                  pl.BlockSpec((1, tn), lambda j: (0, j))],
        out_specs=pl.BlockSpec((r, tn), lambda j: (0, j)),
        out_shape=jax.ShapeDtypeStruct((r, n), F32),
        compiler_params=_cparams("arbitrary"),
        name="ada_modulation",
    )(cc, ada_w, ada_b.reshape(1, n))


def _rms_modulate(x, g, shift, scale):
    ms = jnp.mean(x * x, axis=-1, keepdims=True)
    y = x * lax.rsqrt(ms + NORM_EPS) * g
    return y * (1.0 + scale) + shift


def _inproj_kernel(x_ref, g_ref, sh_ref, sc_ref, *rest, n_out):
    w_refs, o_refs = rest[:n_out], rest[n_out:]
    u = _rms_modulate(x_ref[...], g_ref[...], sh_ref[0], sc_ref[0]).astype(BF16)
    for w_ref, o_ref in zip(w_refs, o_refs):
        o_ref[...] = _dot(u, w_ref[...])


def norm_modulate_project(x2d, g, shift, scale, weights, rows_per_mod, tm=512):
    r, d = x2d.shape
    tm = min(tm, rows_per_mod)
    tiles_per_mod = rows_per_mod // tm
    mod_spec = pl.BlockSpec((1, 1, d), lambda i: (i // tiles_per_mod, 0, 0))
    in_specs = [pl.BlockSpec((tm, d), lambda i: (i, 0)),
                pl.BlockSpec((1, d), lambda i: (0, 0)), mod_spec, mod_spec]
    in_specs += [pl.BlockSpec(w.shape, lambda i: (0, 0)) for w in weights]
    return pl.pallas_call(
        functools.partial(_inproj_kernel, n_out=len(weights)),
        grid=(r // tm,),
        in_specs=in_specs,
        out_specs=[pl.BlockSpec((tm, w.shape[1]), lambda i: (i, 0)) for w in weights],
        out_shape=[jax.ShapeDtypeStruct((r, w.shape[1]), F32) for w in weights],
        compiler_params=_cparams("parallel"),
        name="norm_modulate_project",
    )(x2d, g.reshape(1, d), shift, scale, *weights)


def _iota2(shape, dim):
    return lax.broadcasted_iota(jnp.int32, shape, dim)


def _unit_tri_inverse(lm, nil):
    n = lm.shape[0]
    eye = jnp.where(_iota2((n, n), 0) == _iota2((n, n), 1), 1.0, 0.0).astype(F32)
    x = eye - lm
    p = _dot(lm, lm, MM)
    yield
    k = 2
    while 2 * k < nil:
        xp = _dot(jnp.concatenate([x, p], axis=0), p, MM)
        yield
        x = x + xp[:n]
        p = xp[n:]
        k *= 2
    res = x + _dot(x, p, MM)
    yield
    return res


def _interleave(chains):
    results = [None] * len(chains)
    live = list(range(len(chains)))
    while live:
        for i in list(live):
            try:
                next(chains[i])
            except StopIteration as done:
                results[i] = done.value
                live.remove(i)
    return results


def _fill_padded(pad_ref, x, n):
    zeros = jnp.zeros((SUBLANE, LANE), F32)
    pad_ref[0:SUBLANE, :] = zeros
    pad_ref[SUBLANE:SUBLANE + n, :] = x
    pad_ref[SUBLANE + n:2 * SUBLANE + n, :] = zeros


def _pair_masks(c):
    n = 2 * c
    brow, bcol = _iota2((n, n), 0), _iota2((n, n), 1)
    same = jnp.logical_not(jnp.logical_xor(brow >= c, bcol >= c))
    tr, tc = jnp.where(brow >= c, brow - c, brow), jnp.where(bcol >= c, bcol - c, bcol)
    fwd = brow < c
    incl = same & ((fwd & (tr >= tc)) | (jnp.logical_not(fwd) & (tr <= tc)))
    strict = same & ((fwd & (tr > tc)) | (jnp.logical_not(fwd) & (tr < tc)))
    return incl, strict


def _gdn_chunk_local(q, k, v, g4, r4):
    c = q.shape[0]
    n = 2 * c
    row, col = _iota2((c, c), 0), _iota2((c, c), 1)
    gc2 = jnp.concatenate(
        [jnp.sum(jnp.where(row >= col, r4[0:1, :], 0.0), axis=1, keepdims=True),
         jnp.sum(jnp.where(row <= col, r4[1:2, :], 0.0), axis=1, keepdims=True)], axis=0)
    tot_f = jnp.sum(r4[0:1, :], axis=1, keepdims=True)
    tot_b = jnp.sum(r4[1:2, :], axis=1, keepdims=True)
    tot2 = jnp.concatenate([jnp.broadcast_to(tot_f, (c, 1)), jnp.broadcast_to(tot_b, (c, 1))], axis=0)
    ri, cj = _iota2((c, n), 0), _iota2((c, n), 1)
    lane_f = cj < c
    cjm = jnp.where(lane_f, cj, cj - c)
    keep = (lane_f & (ri <= cjm)) | (jnp.logical_not(lane_f) & (ri >= cjm))
    gc_row2 = jnp.sum(jnp.where(keep, jnp.where(lane_f, g4[:, 0:1], g4[:, 1:2]), 0.0), axis=0, keepdims=True)
    incl, strict = _pair_masks(c)
    decay = jnp.where(incl, jnp.exp(jnp.where(incl, gc2 - gc_row2, 0.0)), 0.0)
    beta2 = jnp.concatenate([g4[:, 2:3], g4[:, 3:4]], axis=0)
    kk2 = jnp.concatenate([k, k], axis=0)
    qq2 = jnp.concatenate([q, q], axis=0)
    kb2 = kk2 * beta2
    m = _dot_nt(jnp.concatenate([kb2, qq2], axis=0), kk2, MM)
    yield
    t = yield from _unit_tri_inverse(jnp.where(strict, m[:n] * decay, 0.0), c)
    egc2 = jnp.exp(gc2)
    sol = _dot(t, jnp.concatenate([jnp.concatenate([v, v], axis=0) * beta2, kb2 * egc2], axis=1), MM)
    yield
    qk2 = jnp.where(incl, m[n:] * decay, 0.0)
    kdt2 = (kk2 * jnp.exp(tot2 - gc2)).T
    fwd_rows = _iota2(sol.shape, 0) < c
    ks_f = _dot(kdt2, jnp.where(fwd_rows, sol, 0.0), MM)
    ks_b = _dot(kdt2, jnp.where(fwd_rows, 0.0, sol), MM)
    yield
    return sol[:, :LANE], sol[:, LANE:], qq2 * egc2, qk2, ks_f, ks_b, jnp.exp(tot_f), jnp.exp(tot_b)


def _gdn_kernel(qc_ref, kc_ref, vc_ref, ql_ref, kl_ref, vl_ref, zl_ref, cwq_ref, cwk_ref, cwv_ref,
                gcol_ref, grow_ref, og_ref, o_ref, q_s, k_s, v_s, oacc_s, pad_s, st_s,
                u_s, wq_s, k0_s, k1_s, qk_s, ge_s, p_s, *, lc, ll):
    tile = min(256, lc, ll)

    def prep(x_ref, cw_ref, dst, off, n, mode):
        _fill_padded(pad_s, x_ref[0], n)
        cw = cw_ref[...]
        for t0 in range(0, n, tile):
            acc = cw[0:1, :] * pad_s[pl.ds(SUBLANE - 2 + t0, tile), :]
            for j in range(1, SHORT_CONV):
                acc = acc + cw[j:j + 1, :] * pad_s[pl.ds(SUBLANE - 2 + j + t0, tile), :]
            y = _silu(acc)
            if mode != "v":
                y = y * lax.rsqrt(jnp.sum(y * y, axis=-1, keepdims=True) + NORM_EPS)
            if mode == "q":
                y = y * (DK_A ** -0.5)
            dst[pl.ds(off + t0, tile), :] = y

    prep(qc_ref, cwq_ref, q_s, 0, lc, "q")
    prep(kc_ref, cwk_ref, k_s, 0, lc, "k")
    prep(vc_ref, cwv_ref, v_s, 0, lc, "v")
    prep(ql_ref, cwq_ref, q_s, lc, ll, "q")
    prep(kl_ref, cwk_ref, k_s, lc, ll, "k")
    prep(vl_ref, cwv_ref, v_s, lc, ll, "v")

    oacc_s[...] = jnp.zeros(oacc_s.shape, F32)
    st_s[...] = jnp.zeros(st_s.shape, F32)
    nc_ctx = lc // CHUNK
    nc_tot = (lc + ll) // CHUNK

    c = CHUNK

    width = max(w for w in (18, 12, 6, 4, 3, 2, 1) if nc_tot % w == 0)

    def local(it, carry):
        cis = [it * width + j for j in range(width)]
        chains = []
        for ci in cis:
            rows = pl.ds(pl.multiple_of(ci * c, c), c)
            chains.append(_gdn_chunk_local(q_s[rows, :], k_s[rows, :], v_s[rows, :],
                                           gcol_ref[0, 0, ci], grow_ref[0, 0, ci]))
        for ci, (u2, w2, qd2, qk2, ks_f, ks_b, ge_f, ge_b) in zip(cis, _interleave(chains)):
            u_s[ci] = u2
            wq_s[ci, 0] = jnp.concatenate([w2[:c], qd2[:c]], axis=0).astype(BF16)
            wq_s[ci, 1] = jnp.concatenate([w2[c:], qd2[c:]], axis=0).astype(BF16)
            for d, ks in enumerate((ks_f, ks_b)):
                k0_s[ci, d] = ks[:, :LANE]
                k1_s[ci, d] = ks[:, LANE:].astype(BF16)
            qk_s[ci] = qk2.astype(BF16)
            ge_s[ci, 0:1, :] = jnp.broadcast_to(ge_f, (1, LANE))
            ge_s[ci, 1:2, :] = jnp.broadcast_to(ge_b, (1, LANE))
        return carry

    lax.fori_loop(0, nc_tot // width, local, 0)

    def bwd_chunk(s):
        return jnp.where(s < nc_ctx, nc_ctx - 1 - s, nc_tot - 1 - (s - nc_ctx))

    def outputs(s, keep):
        cb = bwd_chunk(s)
        p_f, p_b = p_s[0], p_s[1]
        v_new = jnp.concatenate([u_s[s, 0:c, :] - p_f[:c], u_s[cb, c:2 * c, :] - p_b[:c]], axis=0)
        qk = jnp.concatenate([qk_s[s, 0:c, :], qk_s[cb, c:2 * c, :]], axis=0)
        o2 = jnp.concatenate([p_f[c:], p_b[c:]], axis=0) + _dot(qk, v_new.astype(BF16))
        if keep is not None:
            o2 = jnp.where(keep, o2, 0.0)
        rows_f = pl.ds(pl.multiple_of(s * c, c), c)
        rows_b = pl.ds(pl.multiple_of(cb * c, c), c)
        oacc_s[rows_f, :] = oacc_s[rows_f, :] + o2[:c]
        oacc_s[rows_b, :] = oacc_s[rows_b, :] + o2[c:]

    def step(s, carry):
        outputs(jnp.maximum(s - 1, 0), s > 0)
        cb = bwd_chunk(s)
        s_f, s_b = st_s[0], st_s[1]
        sb_f, sb_b = s_f.astype(BF16), s_b.astype(BF16)
        p_s[0] = _dot(wq_s[s, 0], sb_f)
        p_s[1] = _dot(wq_s[cb, 1], sb_b)
        st_s[0] = s_f * ge_s[s, 0:1, :] + k0_s[s, 0] - _dot(k1_s[s, 0], sb_f)
        st_s[1] = s_b * ge_s[cb, 1:2, :] + k0_s[cb, 1] - _dot(k1_s[cb, 1], sb_b)
        return carry

    p_s[...] = jnp.zeros(p_s.shape, F32)
    lax.fori_loop(0, nc_tot, step, 0)
    outputs(nc_tot - 1, None)

    og = og_ref[...]
    for t0 in range(0, ll, tile):
        o = oacc_s[pl.ds(lc + t0, tile), :]
        o = o * lax.rsqrt(jnp.mean(o * o, axis=-1, keepdims=True) + NORM_EPS) * og
        o_ref[0, pl.ds(t0, tile), :] = o * _silu(zl_ref[0, pl.ds(t0, tile), :])


def gdn_branch(pa_ctx, pa_lat, conv_w, gcol, grow, onorm_g):
    b, lc, _ = pa_ctx.shape
    ll = pa_lat.shape[1]
    nc = (lc + ll) // CHUNK

    def col(blk, n):
        return pl.BlockSpec((1, n, LANE), lambda i, h: (i, 0, blk * H_A + h))

    def cw(blk):
        return pl.BlockSpec((SUBLANE, LANE), lambda i, h: (0, blk * H_A + h))

    in_specs = [col(0, lc), col(1, lc), col(2, lc), col(0, ll), col(1, ll), col(2, ll), col(3, ll),
                cw(0), cw(1), cw(2),
                pl.BlockSpec((1, 1, nc, CHUNK, 4), lambda i, h: (i, h, 0, 0, 0)),
                pl.BlockSpec((1, 1, nc, 4, CHUNK), lambda i, h: (i, h, 0, 0, 0)),
                pl.BlockSpec((1, LANE), lambda i, h: (0, 0))]
    lt = lc + ll
    return pl.pallas_call(
        functools.partial(_gdn_kernel, lc=lc, ll=ll),
        grid=(b, H_A),
        in_specs=in_specs,
        out_specs=pl.BlockSpec((1, ll, LANE), lambda i, h: (i, 0, h)),
        out_shape=jax.ShapeDtypeStruct((b, ll, C_A), F32),
        scratch_shapes=[pltpu.VMEM((lt, LANE), F32)] * 4
        + [pltpu.VMEM((max(lc, ll) + 2 * SUBLANE, LANE), F32), pltpu.VMEM((2, DK_A, DK_A), F32),
           pltpu.VMEM((nc, 2 * CHUNK, LANE), F32), pltpu.VMEM((nc, 2, 2 * CHUNK, LANE), BF16),
           pltpu.VMEM((nc, 2, DK_A, DK_A), F32), pltpu.VMEM((nc, 2, DK_A, DK_A), BF16),
           pltpu.VMEM((nc, 2 * CHUNK, 2 * CHUNK), BF16),
           pltpu.VMEM((nc, SUBLANE, LANE), F32), pltpu.VMEM((2, 2 * CHUNK, LANE), F32)],
        compiler_params=_cparams("parallel", "parallel"),
        name="gdn_branch",
    )(pa_ctx, pa_ctx, pa_ctx, pa_lat, pa_lat, pa_lat, pa_lat, conv_w, conv_w, conv_w, gcol, grow,
      onorm_g.reshape(1, LANE))


def _stack_heads(x, head0):
    return jnp.concatenate([jnp.where(head0, x, 0.0), jnp.where(head0, 0.0, x)], axis=0)


def _rwkv_chunk_local(r, v, a, logw, kdir, b, lower):
    c = r.shape[0]
    n = 2 * c
    row, col = _iota2((c, c), 0), _iota2((c, c), 1)
    tri = jnp.where(row >= col if lower else row <= col, 1.0, 0.0).astype(F32)
    lcum = _dot_small_int_lhs(tri, logw)
    tot = jnp.sum(logw, axis=0, keepdims=True)
    e_in = jnp.exp(lcum)
    e_ex = jnp.exp(lcum - logw)
    e_neg = jnp.exp(-lcum)
    e_rem = jnp.exp(tot - lcum)
    head0 = _iota2((c, LANE), 1) < N_B
    a2 = _stack_heads(a * e_ex, head0)
    r2 = _stack_heads(r * e_in, head0)
    b2 = _stack_heads(b * e_neg, head0)
    k2 = _stack_heads(kdir * e_neg, head0)
    v2 = _stack_heads(v, head0)
    m = _dot_nt(jnp.concatenate([a2, r2], axis=0), jnp.concatenate([b2, k2], axis=0), MM)
    yield
    brow, bcol = _iota2((n, n), 0), _iota2((n, n), 1)
    same = jnp.logical_not(jnp.logical_xor(brow >= c, bcol >= c))
    tr, tc = jnp.where(brow >= c, brow - c, brow), jnp.where(bcol >= c, bcol - c, bcol)
    strict = same & ((tr > tc) if lower else (tr < tc))
    incl = same & ((tr >= tc) if lower else (tr <= tc))
    ak = jnp.where(strict, m[:n, n:], 0.0)
    rb = jnp.where(incl, m[n:, :n], 0.0)
    rk = jnp.where(incl, m[n:, n:], 0.0)
    akv_rkv = _dot(jnp.concatenate([ak, rk], axis=0), v2, MM)
    kv0 = _dot_tn(_stack_heads(kdir * e_rem, head0), v2, MM)
    t = yield from _unit_tri_inverse(jnp.where(strict, -m[:n, :n], 0.0), c)
    tw = _dot(t, jnp.concatenate([a2, akv_rkv[:n]], axis=1), MM)
    yield
    x = jnp.concatenate([tw[:, :LANE], r2], axis=0)
    eg = jnp.broadcast_to(jnp.exp(tot), (LANE, LANE)).T
    bu = _dot(_stack_heads(b * e_rem, head0).T, tw, MM)
    yield
    return x, rb, bu[:, :LANE], tw[:, LANE:], akv_rkv[n:], kv0 + bu[:, LANE:], eg


def _rwkv_kernel(*refs, lc, ll):
    ctx_refs, lat_refs = refs[0:6], refs[6:12]
    pch_ref, plo_ref, w0a0_ref, w2_ref, a2_ref, g2_ref, o_ref = refs[12:19]
    r_s, v_s, a_s, lw_s, kd_s, b_s, gate_s, bonus_s, y_s, pad_s, st_s, x_s, rb_s, m1_s, f_s, p_s = refs[19:]
    tile = min(256, lc, ll)
    pch = pch_ref[...]
    k_k, k_a, r_k, lnx_g, lnx_b = (pch[i:i + 1, :] for i in range(5))
    mus = [pch[5:6, :], pch[6:7, :], pch[7:8, :]] + [plo_ref[i:i + 1, :] for i in range(3)]
    w0a0 = w0a0_ref[...]
    rr, cc = _iota2((LANE, LANE), 0), _iota2((LANE, LANE), 1)
    seg = jnp.where(jnp.logical_xor(rr >= N_B, cc >= N_B), 0.0, 1.0).astype(F32)

    def prep(src_refs, off, n, is_lat):
        for j in range(6):
            _fill_padded(pad_s.at[j], src_refs[j][0], n)
        for t0 in range(0, n, tile):
            mixed = []
            for j in range(6):
                x = pad_s[j, pl.ds(SUBLANE + t0, tile), :]
                nb = pad_s[j, pl.ds(SUBLANE - 1 + t0, tile), :] + pad_s[j, pl.ds(SUBLANE + 1 + t0, tile), :]
                mixed.append(x + (0.5 * nb - x) * mus[j])
            r, k, v, wl, al, gl = mixed
            wl = jnp.tanh(wl)
            kk = k * k_k
            kk = kk * lax.rsqrt(_dot_small_int_rhs(kk * kk, seg) + NORM_EPS)
            ksum = jnp.zeros_like(k)
            for d in range(2):
                w_log = -_softplus(-(w0a0[d:d + 1, :] + _dot(wl, w2_ref[d], MM))) - 0.5
                iclr = _sigmoid(w0a0[2 + d:3 + d, :] + _dot(al, a2_ref[d], MM))
                kdir = k * (1.0 + (iclr - 1.0) * k_a)
                ksum = ksum + kdir
                lw_s[d, pl.ds(off + t0, tile), :] = -jnp.exp(w_log)
                kd_s[d, pl.ds(off + t0, tile), :] = kdir
                b_s[d, pl.ds(off + t0, tile), :] = kk * iclr
            r_s[pl.ds(off + t0, tile), :] = r
            v_s[pl.ds(off + t0, tile), :] = v
            a_s[pl.ds(off + t0, tile), :] = -kk
            if is_lat:
                gate_s[pl.ds(t0, tile), :] = _dot(_sigmoid(gl), g2_ref[...], MM)
                bonus_s[pl.ds(t0, tile), :] = _dot_small_int_rhs(r * ksum * r_k, seg) * v

    prep(ctx_refs, 0, lc, False)
    prep(lat_refs, lc, ll, True)

    y_s[...] = jnp.zeros(y_s.shape, F32)
    st_s[...] = jnp.zeros(st_s.shape, F32)
    p_s[...] = jnp.zeros(p_s.shape, F32)
    nc_ctx = lc // CHUNK
    nc_tot = (lc + ll) // CHUNK

    group = x_s.shape[0]
    n2 = 2 * CHUNK

    def chunk_of(s, d):
        cb = jnp.where(s < nc_ctx, nc_ctx - 1 - s, nc_tot - 1 - (s - nc_ctx))
        return pl.ds(pl.multiple_of((cb if d else s) * CHUNK, CHUNK), CHUNK)

    width = max(w for w in (12, 6, 3, 2, 1) if group % w == 0)

    def run_group(gi, carry):
        def local(it, c2):
            slots = [(it * width + j, d) for j in range(width) for d in range(2)]
            chains = []
            for i, d in slots:
                rows = chunk_of(gi * group + i, d)
                chains.append(_rwkv_chunk_local(r_s[rows, :], v_s[rows, :], a_s[rows, :], lw_s[d, rows, :],
                                                kd_s[d, rows, :], b_s[d, rows, :], lower=(d == 0)))
            for (i, d), (x, rb, m1, u0, rkv, hc, eg) in zip(slots, _interleave(chains)):
                x_s[i, d] = x.astype(BF16)
                rb_s[i, d] = rb.astype(BF16)
                m1_s[i, d] = m1.astype(BF16)
                f_s[i, d, 0] = u0
                f_s[i, d, 1] = rkv
                f_s[i, d, 2] = hc
                f_s[i, d, 3] = eg
            return c2

        lax.fori_loop(0, group // width, local, 0)

        def outputs(i, keep):
            for d in range(2):
                rows = chunk_of(gi * group + i, d)
                u = p_s[d, 0:n2, :] + f_s[i, d, 0]
                y2 = p_s[d, n2:2 * n2, :] + _dot(rb_s[i, d], u.astype(BF16)) + f_s[i, d, 1]
                y = y2[:CHUNK] + y2[CHUNK:]
                if keep is not None:
                    y = jnp.where(keep, y, 0.0)
                y_s[rows, :] = y_s[rows, :] + y

        def step(i, c2):
            outputs(jnp.maximum(i - 1, 0), i > 0)
            hs = [st_s[d] for d in range(2)]
            hbs = [h.astype(BF16) for h in hs]
            for d in range(2):
                p_s[d] = _dot(x_s[i, d], hbs[d])
            for d in range(2):
                st_s[d] = f_s[i, d, 3] * hs[d] + f_s[i, d, 2] + _dot(m1_s[i, d], hbs[d])
            return c2

        lax.fori_loop(0, group, step, 0)
        outputs(group - 1, None)
        return carry

    lax.fori_loop(0, nc_tot // group, run_group, 0)

    inv_n = 1.0 / N_B
    for t0 in range(0, ll, tile):
        yf = y_s[pl.ds(lc + t0, tile), :]
        cen = yf - _dot_small_int_rhs(yf, seg) * inv_n
        var = _dot_small_int_rhs(cen * cen, seg) * inv_n
        y = cen * lax.rsqrt(var + LNX_EPS) * lnx_g + lnx_b
        o_ref[0, pl.ds(t0, tile), :] = (y + bonus_s[pl.ds(t0, tile), :]) * gate_s[pl.ds(t0, tile), :]


def rwkv_branch(pb_ctx, pb_lat, pch, plo, w0a0, w2pad, a2pad, g2):
    b, lc, _ = pb_ctx.shape
    ll = pb_lat.shape[1]
    lt = lc + ll
    pairs = C_B // LANE

    def col(blk, n, per_pair):
        if per_pair:
            return pl.BlockSpec((1, n, LANE), lambda i, p: (i, 0, blk * pairs + p))
        return pl.BlockSpec((1, n, LANE), lambda i, p: (i, 0, 3 * pairs + blk))

    def cols(n):
        return [col(0, n, True), col(1, n, True), col(2, n, True), col(0, n, False), col(1, n, False),
                col(2, n, False)]

    in_specs = cols(lc) + cols(ll) + [
        pl.BlockSpec((SUBLANE, LANE), lambda i, p: (0, p)),
        pl.BlockSpec((SUBLANE, LANE), lambda i, p: (0, 0)),
        pl.BlockSpec((SUBLANE, LANE), lambda i, p: (0, p)),
        pl.BlockSpec((2, LANE, LANE), lambda i, p: (0, 0, p)),
        pl.BlockSpec((2, LANE, LANE), lambda i, p: (0, 0, p)),
        pl.BlockSpec((LANE, LANE), lambda i, p: (0, p))]
    seq = pltpu.VMEM((lt, LANE), F32)
    seq2 = pltpu.VMEM((2, lt, LANE), F32)
    lat = pltpu.VMEM((ll, LANE), F32)
    nc = lt // CHUNK
    group = max(g for g in (12, 6, 4, 3, 2, 1) if nc % g == 0)
    n2 = 2 * CHUNK
    return pl.pallas_call(
        functools.partial(_rwkv_kernel, lc=lc, ll=ll),
        grid=(b, pairs),
        in_specs=in_specs,
        out_specs=pl.BlockSpec((1, ll, LANE), lambda i, p: (i, 0, p)),
        out_shape=jax.ShapeDtypeStruct((b, ll, C_B), F32),
        scratch_shapes=[seq, seq, seq, seq2, seq2, seq2, lat, lat, seq,
                        pltpu.VMEM((6, max(lc, ll) + 2 * SUBLANE, LANE), F32),
                        pltpu.VMEM((2, LANE, LANE), F32),
                        pltpu.VMEM((group, 2, 2 * n2, LANE), BF16), pltpu.VMEM((group, 2, n2, n2), BF16),
                        pltpu.VMEM((group, 2, LANE, n2), BF16), pltpu.VMEM((group, 2, 4, n2, LANE), F32),
                        pltpu.VMEM((2, 2 * n2, LANE), F32)],
        compiler_params=_cparams("parallel", "parallel"),
        name="rwkv_branch",
    )(*([pb_ctx] * 6), *([pb_lat] * 6), pch, plo, w0a0, w2pad, a2pad, g2)


def _merge_kernel(x_ref, pg_ref, oa_ref, ob_ref, m2_ref, woa_ref, wob_ref, wout_ref, o_ref):
    d = x_ref.shape[1]
    ya = _dot(oa_ref[...].astype(BF16), woa_ref[...])
    yb = _dot(ob_ref[...].astype(BF16), wob_ref[...])
    y = _sigmoid(pg_ref[:, 0:d]) * ya + _sigmoid(pg_ref[:, d:2 * d]) * yb
    o_ref[...] = x_ref[...] + m2_ref[0] * _dot(y.astype(BF16), wout_ref[...])


def merge_residual(x2d, pg, oa, ob, m2, w_o_a, w_o_b, w_out, rows_per_mod, tm=512):
    r, d = x2d.shape

    def rows(n):
        return pl.BlockSpec((tm, n), lambda i: (i, 0))

    def full(w):
        return pl.BlockSpec(w.shape, lambda i: (0, 0))

    tiles_per_mod = rows_per_mod // tm
    return pl.pallas_call(
        _merge_kernel,
        grid=(r // tm,),
        in_specs=[rows(d), rows(2 * d), rows(oa.shape[1]), rows(ob.shape[1]),
                  pl.BlockSpec((1, 1, d), lambda i: (i // tiles_per_mod, 0, 0)),
                  full(w_o_a), full(w_o_b), full(w_out)],
        out_specs=rows(d),
        out_shape=jax.ShapeDtypeStruct((r, d), F32),
        compiler_params=_cparams("parallel"),
        name="merge_residual",
    )(x2d, pg, oa, ob, m2, w_o_a, w_o_b, w_out)


ROUTER_GROUP_LANE0 = N_EXPERTS


def _route_kernel(h_ref, g_ref, sh_ref, sc_ref, wr_ref, br_ref, t_ref, cw_ref):
    t = _rms_modulate(h_ref[...], g_ref[...], sh_ref[0], sc_ref[0])
    t_ref[...] = t.astype(BF16)
    t_hi, t_lo, _ = _split3(t)
    w_hi, w_lo, _ = _split3(wr_ref[...])
    lg = _dot(t_hi, w_hi) + (_dot(t_hi, w_lo) + _dot(t_lo, w_hi)) + br_ref[...]
    lane = _iota2(lg.shape, 1)
    lane_f = lane.astype(F32)
    neg = jnp.float32(-jnp.inf)
    big = jnp.float32(2 * LANE)
    is_grp = (lane >= ROUTER_GROUP_LANE0) & (lane < ROUTER_GROUP_LANE0 + N_GROUPS)
    lgg = jnp.where(is_grp, lg, neg)
    mg = jnp.max(lgg, axis=-1, keepdims=True)
    p_grp = 1.0 / jnp.sum(jnp.where(is_grp, jnp.exp(lgg - mg), 0.0), axis=-1, keepdims=True)
    g_sel = jnp.min(jnp.where(lgg == mg, lane_f, big), axis=-1, keepdims=True) - ROUTER_GROUP_LANE0
    grp_of_lane = lax.shift_right_logical(lane, EXPERTS_PER_GROUP.bit_length() - 1).astype(F32)
    in_grp = (lane < N_EXPERTS) & (grp_of_lane == g_sel)
    l1 = jnp.where(in_grp, lg, neg)
    top1 = jnp.max(l1, axis=-1, keepdims=True)
    idx1 = jnp.min(jnp.where(l1 == top1, lane_f, big), axis=-1, keepdims=True)
    l2 = jnp.where(in_grp & (lane_f != idx1), lg, neg)
    top2 = jnp.max(l2, axis=-1, keepdims=True)
    idx2 = jnp.min(jnp.where(l2 == top2, lane_f, big), axis=-1, keepdims=True)
    e2 = jnp.exp(top2 - top1)
    w1 = p_grp / (1.0 + e2)
    cw_ref[...] = jnp.where(lane_f == idx1, w1, jnp.where(lane_f == idx2, w1 * e2, 0.0))


def route(h2d, g, shift, scale, w_router, b_router, rows_per_mod, tm=512):
    r, d = h2d.shape
    tiles_per_mod = rows_per_mod // tm
    mod_spec = pl.BlockSpec((1, 1, d), lambda i: (i // tiles_per_mod, 0, 0))
    return pl.pallas_call(
        _route_kernel,
        grid=(r // tm,),
        in_specs=[pl.BlockSpec((tm, d), lambda i: (i, 0)), pl.BlockSpec((1, d), lambda i: (0, 0)),
                  mod_spec, mod_spec,
                  pl.BlockSpec((d, LANE), lambda i: (0, 0)), pl.BlockSpec((1, LANE), lambda i: (0, 0))],
        out_specs=[pl.BlockSpec((tm, d), lambda i: (i, 0)), pl.BlockSpec((tm, LANE), lambda i: (i, 0))],
        out_shape=[jax.ShapeDtypeStruct((r, d), BF16), jax.ShapeDtypeStruct((r, LANE), F32)],
        compiler_params=_cparams("parallel"),
        name="moe_route",
    )(h2d, g.reshape(1, d), shift, scale, w_router, b_router)


EXPERTS_PER_STEP = 4


def _experts_kernel(t_ref, cw_ref, h_ref, m5_ref, fg_ref, wg_ref, wu_ref, wd_ref, o_ref, acc_ref):
    s = pl.program_id(1)

    @pl.when(s == 0)
    def _():
        acc_ref[...] = jnp.zeros(acc_ref.shape, F32)

    t = t_ref[...]
    cw = cw_ref[...]
    lane = _iota2(cw.shape, 1)
    hids = []
    for j in range(EXPERTS_PER_STEP):
        w = jnp.sum(jnp.where(lane == s * EXPERTS_PER_STEP + j, cw, 0.0), axis=-1, keepdims=True)
        hids.append((_silu(_dot(t, wg_ref[j])) * _dot(t, wu_ref[j]) * w).astype(BF16))
    acc_ref[...] += _dot(jnp.concatenate(hids, axis=1), wd_ref[...])

    @pl.when(s == pl.num_programs(1) - 1)
    def _():
        h2 = h_ref[...] + m5_ref[0] * acc_ref[...]
        ms = jnp.mean(h2 * h2, axis=-1, keepdims=True)
        o_ref[...] = h2 * lax.rsqrt(ms + NORM_EPS) * fg_ref[...]


def experts_residual_norm(t, cw, h2d, m5, final_g, w_gate, w_up, w_down, rows_per_mod, tm=1024):
    r, d = h2d.shape
    ne, _, f = w_gate.shape
    eps = EXPERTS_PER_STEP
    tiles_per_mod = rows_per_mod // tm
    return pl.pallas_call(
        _experts_kernel,
        grid=(r // tm, ne // eps),
        in_specs=[pl.BlockSpec((tm, d), lambda i, e: (i, 0)), pl.BlockSpec((tm, LANE), lambda i, e: (i, 0)),
                  pl.BlockSpec((tm, d), lambda i, e: (i, 0)),
                  pl.BlockSpec((1, 1, d), lambda i, e: (i // tiles_per_mod, 0, 0)),
                  pl.BlockSpec((1, d), lambda i, e: (0, 0)),
                  pl.BlockSpec((eps, d, f), lambda i, e: (e, 0, 0)), pl.BlockSpec((eps, d, f), lambda i, e: (e, 0, 0)),
                  pl.BlockSpec((eps * f, d), lambda i, e: (e, 0))],
        out_specs=pl.BlockSpec((tm, d), lambda i, e: (i, 0)),
        out_shape=jax.ShapeDtypeStruct((r, d), F32),
        scratch_shapes=[pltpu.VMEM((tm, d), F32)],
        compiler_params=_cparams("parallel", "arbitrary"),
        name="moe_experts",
    )(t, cw, h2d, m5, final_g.reshape(1, d), w_gate, w_up, w_down)


def _to_col_major(x, rows):
    b, l, c = x.shape
    return x.reshape(b, rows, GRID_W, c).transpose(0, 2, 1, 3).reshape(b, l, c)


def _to_row_major(x, rows):
    b, l, c = x.shape
    return x.reshape(b, GRID_W, rows, c).transpose(0, 2, 1, 3).reshape(b, l, c)


def _pad_rows(a, n):
    return jnp.pad(a, ((0, n - a.shape[0]),) + ((0, 0),) * (a.ndim - 1))


def _gdn_gates(pa, a_log, dt_bias):
    b, l, _ = pa.shape
    ab = pa[..., 4 * C_A:4 * C_A + 4 * H_A]
    a = ab[..., :2 * H_A].reshape(b, l, 2, H_A)
    bt = ab[..., 2 * H_A:].reshape(b, l, 2, H_A)
    g = -jnp.exp(a_log) * jax.nn.softplus(a + dt_bias)
    beta = jax.nn.sigmoid(bt)
    return jnp.concatenate([g, beta], axis=2).transpose(0, 3, 1, 2)


def kernel(x, c, ctx, c_ctx, ada_w, ada_b, norm_mix_g, norm_ffn_g, w_in, gdn_conv, gdn_a_log, gdn_dt_bias,
           gdn_onorm_g, rwkv_mu, rwkv_w0, rwkv_w2, rwkv_a0, rwkv_a2, rwkv_g2, rwkv_k_k, rwkv_k_a, rwkv_r_k,
           rwkv_lnx_g, rwkv_lnx_b, w_o_a, w_o_b, w_out, router_grp, router_grp_b, router_exp, router_exp_b,
           moe_w_gate, moe_w_up, moe_w_down, final_norm_g):
    bsz, seq, d = x.shape
    lc = ctx.shape[1]
    rows = seq // GRID_W
    a_cols = 4 * C_A + 4 * H_A
    b_cols = 3 * C_B + 2 * LORA_W + 2 * LORA_A + LORA_G

    cc = _pad_rows(jnp.concatenate([c, c_ctx[None]], axis=0), 2 * SUBLANE)
    mod = ada_modulation(cc, ada_w[0], ada_b[0])
    m_lat = [mod[:bsz, i * d:(i + 1) * d].reshape(bsz, 1, d) for i in range(6)]
    m_ctx = [mod[bsz:bsz + 1, i * d:(i + 1) * d].reshape(1, 1, d) for i in range(2)]

    w = w_in[0]
    w_a = jnp.pad(w[:, :a_cols], ((0, 0), (0, 4 * C_A + LANE - a_cols))).astype(BF16)
    w_b = w[:, a_cols:a_cols + b_cols].astype(BF16)
    w_g = w[:, a_cols + b_cols:].astype(BF16)

    x2d = x.reshape(bsz * seq, d)
    pa_lat, pg_lat = norm_modulate_project(x2d, norm_mix_g[0], m_lat[0], m_lat[1], [w_a, w_g], seq)
    (pb_lat,) = norm_modulate_project(_to_col_major(x, rows).reshape(bsz * seq, d), norm_mix_g[0],
                                      m_lat[0], m_lat[1], [w_b], seq)
    pa_ctx, pb_ctx = norm_modulate_project(ctx.reshape(bsz * lc, d), norm_mix_g[0], m_ctx[0], m_ctx[1],
                                           [w_a, w_b], bsz * lc)
    pa_lat = pa_lat.reshape(bsz, seq, -1)
    pb_lat = pb_lat.reshape(bsz, seq, -1)
    pa_ctx = pa_ctx.reshape(bsz, lc, -1)
    pb_ctx = pb_ctx.reshape(bsz, lc, -1)

    gates = jnp.concatenate([_gdn_gates(pa_ctx, gdn_a_log[0], gdn_dt_bias[0]),
                             _gdn_gates(pa_lat, gdn_a_log[0], gdn_dt_bias[0])], axis=2)
    nc = (lc + seq) // CHUNK
    gcol = gates.reshape(bsz, H_A, nc, CHUNK, 4)
    grow = gcol.transpose(0, 1, 2, 4, 3)
    oa = gdn_branch(pa_ctx, pa_lat, _pad_rows(gdn_conv[0], SUBLANE), gcol, grow, gdn_onorm_g[0])

    mu = rwkv_mu[0]
    pch = jnp.stack([rwkv_k_k[0], rwkv_k_a[0], rwkv_r_k[0].reshape(C_B), rwkv_lnx_g[0], rwkv_lnx_b[0],
                     mu[:C_B], mu[C_B:2 * C_B], mu[2 * C_B:3 * C_B]])
    plo = _pad_rows(mu[3 * C_B:].reshape(3, LANE), SUBLANE)
    w0a0 = _pad_rows(jnp.concatenate([rwkv_w0[0], rwkv_a0[0]], axis=0), SUBLANE)
    zw = jnp.zeros((LORA_W, C_B), F32)
    w2pad = jnp.stack([jnp.concatenate([rwkv_w2[0, 0], zw]), jnp.concatenate([zw, rwkv_w2[0, 1]])])
    a2pad = jnp.stack([jnp.concatenate([rwkv_a2[0, 0], zw]), jnp.concatenate([zw, rwkv_a2[0, 1]])])
    ob = rwkv_branch(pb_ctx, pb_lat, pch, plo, w0a0, w2pad, a2pad, rwkv_g2[0])
    ob = _to_row_major(ob, rows)

    h1 = merge_residual(x2d, pg_lat, oa.reshape(bsz * seq, C_A), ob.reshape(bsz * seq, C_B), m_lat[2],
                        w_o_a[0].astype(BF16), w_o_b[0].astype(BF16), w_out[0].astype(BF16), seq)

    w_router = jnp.pad(jnp.concatenate([router_exp[0], router_grp[0]], axis=1),
                       ((0, 0), (0, LANE - N_EXPERTS - N_GROUPS)))
    b_router = jnp.pad(jnp.concatenate([router_exp_b[0], router_grp_b[0]]),
                       (0, LANE - N_EXPERTS - N_GROUPS)).reshape(1, LANE)
    t, cw = route(h1, norm_ffn_g[0], m_lat[3], m_lat[4], w_router, b_router, seq)
    out = experts_residual_norm(t, cw, h1, m_lat[5], final_norm_g,
                                moe_w_gate[0].reshape(N_EXPERTS, d, D_EXPERT).astype(BF16),
                                moe_w_up[0].reshape(N_EXPERTS, d, D_EXPERT).astype(BF16),
                                moe_w_down[0].reshape(N_EXPERTS * D_EXPERT, d).astype(BF16), seq)
    return out.reshape(bsz, seq, d)
```

```python
import functools

import jax
import jax.numpy as jnp
from jax import lax
from jax.experimental import pallas as pl
from jax.experimental.pallas import tpu as pltpu

F32 = jnp.float32
BF16 = jnp.bfloat16
HIGHEST = lax.Precision.HIGHEST

GRID_W = 64
H_A = 4
DK_A = 128
C_A = H_A * DK_A
SHORT_CONV = 5
CHUNK = 64
H_B = 8
N_B = 64
C_B = H_B * N_B
LORA_W = 64
LORA_A = 64
LORA_G = 128
N_GROUPS = 4
EXPERTS_PER_GROUP = 8
N_EXPERTS = N_GROUPS * EXPERTS_PER_GROUP
D_EXPERT = 256
NORM_EPS = 1e-6
LNX_EPS = 1e-5 * N_B

LANE = 128
SUBLANE = 8
VMEM_LIMIT = 56 * 1024 * 1024


def _cparams(*sem):
    return pltpu.CompilerParams(dimension_semantics=sem, vmem_limit_bytes=VMEM_LIMIT)


SINGLE = "bf16 operands, one MXU pass, f32 accumulation"
MM = SINGLE


def _operands(a, b, precision):
    if precision is SINGLE:
        return a.astype(BF16), b.astype(BF16), None
    return a, b, precision


def _dot(a, b, precision=None):
    a, b, precision = _operands(a, b, precision)
    return jnp.dot(a, b, preferred_element_type=F32, precision=precision)


def _dot_nt(a, b, precision=None):
    a, b, precision = _operands(a, b, precision)
    return lax.dot_general(a, b, (((1,), (1,)), ((), ())), preferred_element_type=F32, precision=precision)


def _dot_tn(a, b, precision=None):
    a, b, precision = _operands(a, b, precision)
    return lax.dot_general(a, b, (((0,), (0,)), ((), ())), preferred_element_type=F32, precision=precision)


def _split3(x):
    hi = x.astype(BF16)
    r1 = x - hi.astype(F32)
    mid = r1.astype(BF16)
    lo = (r1 - mid.astype(F32)).astype(BF16)
    return hi, mid, lo


def _dot_small_int_lhs(m, x):
    mb = m.astype(BF16)
    hi, mid, lo = _split3(x)
    return _dot(mb, hi) + _dot(mb, mid) + _dot(mb, lo)


def _dot_small_int_rhs(x, m):
    mb = m.astype(BF16)
    hi, mid, lo = _split3(x)
    return _dot(hi, mb) + _dot(mid, mb) + _dot(lo, mb)


def _sigmoid(x):
    return jax.nn.sigmoid(x)


def _silu(x):
    return x * jax.nn.sigmoid(x)


def _softplus(x):
    return jnp.maximum(x, 0.0) + jnp.log1p(jnp.exp(-jnp.abs(x)))


def _ada_kernel(c_ref, w_ref, b_ref, o_ref):
    s_hi, s_lo, _ = _split3(_silu(c_ref[...]))
    w_hi, w_lo, _ = _split3(w_ref[...])
    o_ref[...] = _dot(s_hi, w_hi) + (_dot(s_hi, w_lo) + _dot(s_lo, w_hi)) + b_ref[...]


def ada_modulation(cc, ada_w, ada_b):
    r, d = cc.shape
    n = ada_w.shape[1]
    tn = 1536
    return pl.pallas_call(
        _ada_kernel,
        grid=(n // tn,),
        in_specs=[pl.BlockSpec((r, d), lambda j: (0, 0)),
                  pl.BlockSpec((d, tn), lambda j: (0, j)),
                  pl.BlockSpec((1, tn), lambda j: (0, j))],
        out_specs=pl.BlockSpec((r, tn), lambda j: (0, j)),
        out_shape=jax.ShapeDtypeStruct((r, n), F32),
        compiler_params=_cparams("arbitrary"),
        name="ada_modulation",
    )(cc, ada_w, ada_b.reshape(1, n))


def _rms_modulate(x, g, shift, scale):
    ms = jnp.mean(x * x, axis=-1, keepdims=True)
    y = x * lax.rsqrt(ms + NORM_EPS) * g
    return y * (1.0 + scale) + shift


def _inproj_kernel(x_ref, g_ref, sh_ref, sc_ref, *rest, n_out):
    w_refs, o_refs = rest[:n_out], rest[n_out:]
    u = _rms_modulate(x_ref[...], g_ref[...], sh_ref[0], sc_ref[0]).astype(BF16)
    for w_ref, o_ref in zip(w_refs, o_refs):
        o_ref[...] = _dot(u, w_ref[...])


def norm_modulate_project(x2d, g, shift, scale, weights, rows_per_mod, tm=512):
    r, d = x2d.shape
    tm = min(tm, rows_per_mod)
    tiles_per_mod = rows_per_mod // tm
    mod_spec = pl.BlockSpec((1, 1, d), lambda i: (i // tiles_per_mod, 0, 0))
    in_specs = [pl.BlockSpec((tm, d), lambda i: (i, 0)),
                pl.BlockSpec((1, d), lambda i: (0, 0)), mod_spec, mod_spec]
    in_specs += [pl.BlockSpec(w.shape, lambda i: (0, 0)) for w in weights]
    return pl.pallas_call(
        functools.partial(_inproj_kernel, n_out=len(weights)),
        grid=(r // tm,),
        in_specs=in_specs,
        out_specs=[pl.BlockSpec((tm, w.shape[1]), lambda i: (i, 0)) for w in weights],
        out_shape=[jax.ShapeDtypeStruct((r, w.shape[1]), F32) for w in weights],
        compiler_params=_cparams("parallel"),
        name="norm_modulate_project",
    )(x2d, g.reshape(1, d), shift, scale, *weights)


def _iota2(shape, dim):
    return lax.broadcasted_iota(jnp.int32, shape, dim)


def _unit_tri_inverse(lm, nil):
    n = lm.shape[0]
    eye = jnp.where(_iota2((n, n), 0) == _iota2((n, n), 1), 1.0, 0.0).astype(F32)
    x = eye - lm
    p = _dot(lm, lm, MM)
    yield
    k = 2
    while 2 * k < nil:
        xp = _dot(jnp.concatenate([x, p], axis=0), p, MM)
        yield
        x = x + xp[:n]
        p = xp[n:]
        k *= 2
    res = x + _dot(x, p, MM)
    yield
    return res


def _interleave(chains):
    results = [None] * len(chains)
    live = list(range(len(chains)))
    while live:
        for i in list(live):
            try:
                next(chains[i])
            except StopIteration as done:
                results[i] = done.value
                live.remove(i)
    return results


def _fill_padded(pad_ref, x, n):
    zeros = jnp.zeros((SUBLANE, LANE), F32)
    pad_ref[0:SUBLANE, :] = zeros
    pad_ref[SUBLANE:SUBLANE + n, :] = x
    pad_ref[SUBLANE + n:2 * SUBLANE + n, :] = zeros


def _pair_masks(c):
    n = 2 * c
    brow, bcol = _iota2((n, n), 0), _iota2((n, n), 1)
    same = jnp.logical_not(jnp.logical_xor(brow >= c, bcol >= c))
    tr, tc = jnp.where(brow >= c, brow - c, brow), jnp.where(bcol >= c, bcol - c, bcol)
    fwd = brow < c
    incl = same & ((fwd & (tr >= tc)) | (jnp.logical_not(fwd) & (tr <= tc)))
    strict = same & ((fwd & (tr > tc)) | (jnp.logical_not(fwd) & (tr < tc)))
    return incl, strict


def _gdn_chunk_local(q, k, v, g4, r4):
    c = q.shape[0]
    n = 2 * c
    row, col = _iota2((c, c), 0), _iota2((c, c), 1)
    gc2 = jnp.concatenate(
        [jnp.sum(jnp.where(row >= col, r4[0:1, :], 0.0), axis=1, keepdims=True),
         jnp.sum(jnp.where(row <= col, r4[1:2, :], 0.0), axis=1, keepdims=True)], axis=0)
    tot_f = jnp.sum(r4[0:1, :], axis=1, keepdims=True)
    tot_b = jnp.sum(r4[1:2, :], axis=1, keepdims=True)
    tot2 = jnp.concatenate([jnp.broadcast_to(tot_f, (c, 1)), jnp.broadcast_to(tot_b, (c, 1))], axis=0)
    ri, cj = _iota2((c, n), 0), _iota2((c, n), 1)
    lane_f = cj < c
    cjm = jnp.where(lane_f, cj, cj - c)
    keep = (lane_f & (ri <= cjm)) | (jnp.logical_not(lane_f) & (ri >= cjm))
    gc_row2 = jnp.sum(jnp.where(keep, jnp.where(lane_f, g4[:, 0:1], g4[:, 1:2]), 0.0), axis=0, keepdims=True)
    incl, strict = _pair_masks(c)
    decay = jnp.where(incl, jnp.exp(jnp.where(incl, gc2 - gc_row2, 0.0)), 0.0)
    beta2 = jnp.concatenate([g4[:, 2:3], g4[:, 3:4]], axis=0)
    kk2 = jnp.concatenate([k, k], axis=0)
    qq2 = jnp.concatenate([q, q], axis=0)
    kb2 = kk2 * beta2
    m = _dot_nt(jnp.concatenate([kb2, qq2], axis=0), kk2, MM)
    yield
    t = yield from _unit_tri_inverse(jnp.where(strict, m[:n] * decay, 0.0), c)
    egc2 = jnp.exp(gc2)
    sol = _dot(t, jnp.concatenate([jnp.concatenate([v, v], axis=0) * beta2, kb2 * egc2], axis=1), MM)
    yield
    qk2 = jnp.where(incl, m[n:] * decay, 0.0)
    kdt2 = (kk2 * jnp.exp(tot2 - gc2)).T
    fwd_rows = _iota2(sol.shape, 0) < c
    ks_f = _dot(kdt2, jnp.where(fwd_rows, sol, 0.0), MM)
    ks_b = _dot(kdt2, jnp.where(fwd_rows, 0.0, sol), MM)
    yield
    return sol[:, :LANE], sol[:, LANE:], qq2 * egc2, qk2, ks_f, ks_b, jnp.exp(tot_f), jnp.exp(tot_b)


def _gdn_kernel(qc_ref, kc_ref, vc_ref, ql_ref, kl_ref, vl_ref, zl_ref, cwq_ref, cwk_ref, cwv_ref,
                gcol_ref, grow_ref, og_ref, o_ref, q_s, k_s, v_s, oacc_s, pad_s, st_s,
                u_s, wq_s, k0_s, k1_s, qk_s, ge_s, p_s, *, lc, ll):
    tile = min(256, lc, ll)

    def prep(x_ref, cw_ref, dst, off, n, mode):
        _fill_padded(pad_s, x_ref[0], n)
        cw = cw_ref[...]
        for t0 in range(0, n, tile):
            acc = cw[0:1, :] * pad_s[pl.ds(SUBLANE - 2 + t0, tile), :]
            for j in range(1, SHORT_CONV):
                acc = acc + cw[j:j + 1, :] * pad_s[pl.ds(SUBLANE - 2 + j + t0, tile), :]
            y = _silu(acc)
            if mode != "v":
                y = y * lax.rsqrt(jnp.sum(y * y, axis=-1, keepdims=True) + NORM_EPS)
            if mode == "q":
                y = y * (DK_A ** -0.5)
            dst[pl.ds(off + t0, tile), :] = y

    prep(qc_ref, cwq_ref, q_s, 0, lc, "q")
    prep(kc_ref, cwk_ref, k_s, 0, lc, "k")
    prep(vc_ref, cwv_ref, v_s, 0, lc, "v")
    prep(ql_ref, cwq_ref, q_s, lc, ll, "q")
    prep(kl_ref, cwk_ref, k_s, lc, ll, "k")
    prep(vl_ref, cwv_ref, v_s, lc, ll, "v")

    oacc_s[...] = jnp.zeros(oacc_s.shape, F32)
    st_s[...] = jnp.zeros(st_s.shape, F32)
    nc_ctx = lc // CHUNK
    nc_tot = (lc + ll) // CHUNK

    c = CHUNK

    width = max(w for w in (18, 12, 6, 4, 3, 2, 1) if nc_tot % w == 0)

    def local(it, carry):
        cis = [it * width + j for j in range(width)]
        chains = []
        for ci in cis:
            rows = pl.ds(pl.multiple_of(ci * c, c), c)
            chains.append(_gdn_chunk_local(q_s[rows, :], k_s[rows, :], v_s[rows, :],
                                           gcol_ref[0, 0, ci], grow_ref[0, 0, ci]))
        for ci, (u2, w2, qd2, qk2, ks_f, ks_b, ge_f, ge_b) in zip(cis, _interleave(chains)):
            u_s[ci] = u2
            wq_s[ci, 0] = jnp.concatenate([w2[:c], qd2[:c]], axis=0).astype(BF16)
            wq_s[ci, 1] = jnp.concatenate([w2[c:], qd2[c:]], axis=0).astype(BF16)
            for d, ks in enumerate((ks_f, ks_b)):
                k0_s[ci, d] = ks[:, :LANE]
                k1_s[ci, d] = ks[:, LANE:].astype(BF16)
            qk_s[ci] = qk2.astype(BF16)
            ge_s[ci, 0:1, :] = jnp.broadcast_to(ge_f, (1, LANE))
            ge_s[ci, 1:2, :] = jnp.broadcast_to(ge_b, (1, LANE))
        return carry

    lax.fori_loop(0, nc_tot // width, local, 0)

    def bwd_chunk(s):
        return jnp.where(s < nc_ctx, nc_ctx - 1 - s, nc_tot - 1 - (s - nc_ctx))

    def outputs(s, keep):
        cb = bwd_chunk(s)
        p_f, p_b = p_s[0], p_s[1]
        v_new = jnp.concatenate([u_s[s, 0:c, :] - p_f[:c], u_s[cb, c:2 * c, :] - p_b[:c]], axis=0)
        qk = jnp.concatenate([qk_s[s, 0:c, :], qk_s[cb, c:2 * c, :]], axis=0)
        o2 = jnp.concatenate([p_f[c:], p_b[c:]], axis=0) + _dot(qk, v_new.astype(BF16))
        if keep is not None:
            o2 = jnp.where(keep, o2, 0.0)
        rows_f = pl.ds(pl.multiple_of(s * c, c), c)
        rows_b = pl.ds(pl.multiple_of(cb * c, c), c)
        oacc_s[rows_f, :] = oacc_s[rows_f, :] + o2[:c]
        oacc_s[rows_b, :] = oacc_s[rows_b, :] + o2[c:]

    def step(s, carry):
        outputs(jnp.maximum(s - 1, 0), s > 0)
        cb = bwd_chunk(s)
        s_f, s_b = st_s[0], st_s[1]
        sb_f, sb_b = s_f.astype(BF16), s_b.astype(BF16)
        p_s[0] = _dot(wq_s[s, 0], sb_f)
        p_s[1] = _dot(wq_s[cb, 1], sb_b)
        st_s[0] = s_f * ge_s[s, 0:1, :] + k0_s[s, 0] - _dot(k1_s[s, 0], sb_f)
        st_s[1] = s_b * ge_s[cb, 1:2, :] + k0_s[cb, 1] - _dot(k1_s[cb, 1], sb_b)
        return carry

    p_s[...] = jnp.zeros(p_s.shape, F32)
    lax.fori_loop(0, nc_tot, step, 0, unroll=2)
    outputs(nc_tot - 1, None)

    og = og_ref[...]
    for t0 in range(0, ll, tile):
        o = oacc_s[pl.ds(lc + t0, tile), :]
        o = o * lax.rsqrt(jnp.mean(o * o, axis=-1, keepdims=True) + NORM_EPS) * og
        o_ref[0, pl.ds(t0, tile), :] = o * _silu(zl_ref[0, pl.ds(t0, tile), :])


def gdn_branch(pa_ctx, pa_lat, conv_w, gcol, grow, onorm_g):
    b, lc, _ = pa_ctx.shape
    ll = pa_lat.shape[1]
    nc = (lc + ll) // CHUNK

    def col(blk, n):
        return pl.BlockSpec((1, n, LANE), lambda i, h: (i, 0, blk * H_A + h))

    def cw(blk):
        return pl.BlockSpec((SUBLANE, LANE), lambda i, h: (0, blk * H_A + h))

    in_specs = [col(0, lc), col(1, lc), col(2, lc), col(0, ll), col(1, ll), col(2, ll), col(3, ll),
                cw(0), cw(1), cw(2),
                pl.BlockSpec((1, 1, nc, CHUNK, 4), lambda i, h: (i, h, 0, 0, 0)),
                pl.BlockSpec((1, 1, nc, 4, CHUNK), lambda i, h: (i, h, 0, 0, 0)),
                pl.BlockSpec((1, LANE), lambda i, h: (0, 0))]
    lt = lc + ll
    return pl.pallas_call(
        functools.partial(_gdn_kernel, lc=lc, ll=ll),
        grid=(b, H_A),
        in_specs=in_specs,
        out_specs=pl.BlockSpec((1, ll, LANE), lambda i, h: (i, 0, h)),
        out_shape=jax.ShapeDtypeStruct((b, ll, C_A), F32),
        scratch_shapes=[pltpu.VMEM((lt, LANE), F32)] * 4
        + [pltpu.VMEM((max(lc, ll) + 2 * SUBLANE, LANE), F32), pltpu.VMEM((2, DK_A, DK_A), F32),
           pltpu.VMEM((nc, 2 * CHUNK, LANE), F32), pltpu.VMEM((nc, 2, 2 * CHUNK, LANE), BF16),
           pltpu.VMEM((nc, 2, DK_A, DK_A), F32), pltpu.VMEM((nc, 2, DK_A, DK_A), BF16),
           pltpu.VMEM((nc, 2 * CHUNK, 2 * CHUNK), BF16),
           pltpu.VMEM((nc, SUBLANE, LANE), F32), pltpu.VMEM((2, 2 * CHUNK, LANE), F32)],
        compiler_params=_cparams("parallel", "parallel"),
        name="gdn_branch",
    )(pa_ctx, pa_ctx, pa_ctx, pa_lat, pa_lat, pa_lat, pa_lat, conv_w, conv_w, conv_w, gcol, grow,
      onorm_g.reshape(1, LANE))


def _stack_heads(x, head0):
    return jnp.concatenate([jnp.where(head0, x, 0.0), jnp.where(head0, 0.0, x)], axis=0)


def _rwkv_chunk_local(r, v, a, logw, kdir, b, lower):
    c = r.shape[0]
    n = 2 * c
    row, col = _iota2((c, c), 0), _iota2((c, c), 1)
    tri = jnp.where(row >= col if lower else row <= col, 1.0, 0.0).astype(F32)
    lcum = _dot_small_int_lhs(tri, logw)
    tot = jnp.sum(logw, axis=0, keepdims=True)
    e_in = jnp.exp(lcum)
    e_ex = jnp.exp(lcum - logw)
    e_neg = jnp.exp(-lcum)
    e_rem = jnp.exp(tot - lcum)
    head0 = _iota2((c, LANE), 1) < N_B
    a2 = _stack_heads(a * e_ex, head0)
    r2 = _stack_heads(r * e_in, head0)
    b2 = _stack_heads(b * e_neg, head0)
    k2 = _stack_heads(kdir * e_neg, head0)
    v2 = _stack_heads(v, head0)
    m = _dot_nt(jnp.concatenate([a2, r2], axis=0), jnp.concatenate([b2, k2], axis=0), MM)
    yield
    brow, bcol = _iota2((n, n), 0), _iota2((n, n), 1)
    same = jnp.logical_not(jnp.logical_xor(brow >= c, bcol >= c))
    tr, tc = jnp.where(brow >= c, brow - c, brow), jnp.where(bcol >= c, bcol - c, bcol)
    strict = same & ((tr > tc) if lower else (tr < tc))
    incl = same & ((tr >= tc) if lower else (tr <= tc))
    ak = jnp.where(strict, m[:n, n:], 0.0)
    rb = jnp.where(incl, m[n:, :n], 0.0)
    rk = jnp.where(incl, m[n:, n:], 0.0)
    akv_rkv = _dot(jnp.concatenate([ak, rk], axis=0), v2, MM)
    kv0 = _dot_tn(_stack_heads(kdir * e_rem, head0), v2, MM)
    t = yield from _unit_tri_inverse(jnp.where(strict, -m[:n, :n], 0.0), c)
    tw = _dot(t, jnp.concatenate([a2, akv_rkv[:n]], axis=1), MM)
    yield
    x = jnp.concatenate([tw[:, :LANE], r2], axis=0)
    eg = jnp.broadcast_to(jnp.exp(tot), (LANE, LANE)).T
    bu = _dot(_stack_heads(b * e_rem, head0).T, tw, MM)
    yield
    return x, rb, bu[:, :LANE], tw[:, LANE:], akv_rkv[n:], kv0 + bu[:, LANE:], eg


def _rwkv_kernel(*refs, lc, ll):
    ctx_refs, lat_refs = refs[0:6], refs[6:12]
    pch_ref, plo_ref, w0a0_ref, w2_ref, a2_ref, g2_ref, o_ref = refs[12:19]
    r_s, v_s, a_s, lw_s, kd_s, b_s, gate_s, bonus_s, y_s, pad_s, st_s, x_s, rb_s, m1_s, f_s, p_s = refs[19:]
    tile = min(256, lc, ll)
    pch = pch_ref[...]
    k_k, k_a, r_k, lnx_g, lnx_b = (pch[i:i + 1, :] for i in range(5))
    mus = [pch[5:6, :], pch[6:7, :], pch[7:8, :]] + [plo_ref[i:i + 1, :] for i in range(3)]
    w0a0 = w0a0_ref[...]
    rr, cc = _iota2((LANE, LANE), 0), _iota2((LANE, LANE), 1)
    seg = jnp.where(jnp.logical_xor(rr >= N_B, cc >= N_B), 0.0, 1.0).astype(F32)

    def prep(src_refs, off, n, is_lat):
        for j in range(6):
            _fill_padded(pad_s.at[j], src_refs[j][0], n)
        for t0 in range(0, n, tile):
            mixed = []
            for j in range(6):
                x = pad_s[j, pl.ds(SUBLANE + t0, tile), :]
                nb = pad_s[j, pl.ds(SUBLANE - 1 + t0, tile), :] + pad_s[j, pl.ds(SUBLANE + 1 + t0, tile), :]
                mixed.append(x + (0.5 * nb - x) * mus[j])
            r, k, v, wl, al, gl = mixed
            wl = jnp.tanh(wl)
            kk = k * k_k
            kk = kk * lax.rsqrt(_dot_small_int_rhs(kk * kk, seg) + NORM_EPS)
            ksum = jnp.zeros_like(k)
            for d in range(2):
                w_log = -_softplus(-(w0a0[d:d + 1, :] + _dot(wl, w2_ref[d], MM))) - 0.5
                iclr = _sigmoid(w0a0[2 + d:3 + d, :] + _dot(al, a2_ref[d], MM))
                kdir = k * (1.0 + (iclr - 1.0) * k_a)
                ksum = ksum + kdir
                lw_s[d, pl.ds(off + t0, tile), :] = -jnp.exp(w_log)
                kd_s[d, pl.ds(off + t0, tile), :] = kdir
                b_s[d, pl.ds(off + t0, tile), :] = kk * iclr
            r_s[pl.ds(off + t0, tile), :] = r
            v_s[pl.ds(off + t0, tile), :] = v
            a_s[pl.ds(off + t0, tile), :] = -kk
            if is_lat:
                gate_s[pl.ds(t0, tile), :] = _dot(_sigmoid(gl), g2_ref[...], MM)
                bonus_s[pl.ds(t0, tile), :] = _dot_small_int_rhs(r * ksum * r_k, seg) * v

    prep(ctx_refs, 0, lc, False)
    prep(lat_refs, lc, ll, True)

    y_s[...] = jnp.zeros(y_s.shape, F32)
    st_s[...] = jnp.zeros(st_s.shape, F32)
    p_s[...] = jnp.zeros(p_s.shape, F32)
    nc_ctx = lc // CHUNK
    nc_tot = (lc + ll) // CHUNK

    group = x_s.shape[0]
    n2 = 2 * CHUNK

    def chunk_of(s, d):
        cb = jnp.where(s < nc_ctx, nc_ctx - 1 - s, nc_tot - 1 - (s - nc_ctx))
        return pl.ds(pl.multiple_of((cb if d else s) * CHUNK, CHUNK), CHUNK)

    width = max(w for w in (12, 6, 3, 2, 1) if group % w == 0)

    def run_group(gi, carry):
        def local(it, c2):
            slots = [(it * width + j, d) for j in range(width) for d in range(2)]
            chains = []
            for i, d in slots:
                rows = chunk_of(gi * group + i, d)
                chains.append(_rwkv_chunk_local(r_s[rows, :], v_s[rows, :], a_s[rows, :], lw_s[d, rows, :],
                                                kd_s[d, rows, :], b_s[d, rows, :], lower=(d == 0)))
            for (i, d), (x, rb, m1, u0, rkv, hc, eg) in zip(slots, _interleave(chains)):
                x_s[i, d] = x.astype(BF16)
                rb_s[i, d] = rb.astype(BF16)
                m1_s[i, d] = m1.astype(BF16)
                f_s[i, d, 0] = u0
                f_s[i, d, 1] = rkv
                f_s[i, d, 2] = hc
                f_s[i, d, 3] = eg
            return c2

        lax.fori_loop(0, group // width, local, 0)

        def outputs(i, keep):
            for d in range(2):
                rows = chunk_of(gi * group + i, d)
                u = p_s[d, 0:n2, :] + f_s[i, d, 0]
                y2 = p_s[d, n2:2 * n2, :] + _dot(rb_s[i, d], u.astype(BF16)) + f_s[i, d, 1]
                y = y2[:CHUNK] + y2[CHUNK:]
                if keep is not None:
                    y = jnp.where(keep, y, 0.0)
                y_s[rows, :] = y_s[rows, :] + y

        def step(i, c2):
            outputs(jnp.maximum(i - 1, 0), i > 0)
            hs = [st_s[d] for d in range(2)]
            hbs = [h.astype(BF16) for h in hs]
            for d in range(2):
                p_s[d] = _dot(x_s[i, d], hbs[d])
            for d in range(2):
                st_s[d] = f_s[i, d, 3] * hs[d] + f_s[i, d, 2] + _dot(m1_s[i, d], hbs[d])
            return c2

        lax.fori_loop(0, group, step, 0, unroll=2)
        outputs(group - 1, None)
        return carry

    lax.fori_loop(0, nc_tot // group, run_group, 0)

    inv_n = 1.0 / N_B
    for t0 in range(0, ll, tile):
        yf = y_s[pl.ds(lc + t0, tile), :]
        cen = yf - _dot_small_int_rhs(yf, seg) * inv_n
        var = _dot_small_int_rhs(cen * cen, seg) * inv_n
        y = cen * lax.rsqrt(var + LNX_EPS) * lnx_g + lnx_b
        o_ref[0, pl.ds(t0, tile), :] = (y + bonus_s[pl.ds(t0, tile), :]) * gate_s[pl.ds(t0, tile), :]


def rwkv_branch(pb_ctx, pb_lat, pch, plo, w0a0, w2pad, a2pad, g2):
    b, lc, _ = pb_ctx.shape
    ll = pb_lat.shape[1]
    lt = lc + ll
    pairs = C_B // LANE

    def col(blk, n, per_pair):
        if per_pair:
            return pl.BlockSpec((1, n, LANE), lambda i, p: (i, 0, blk * pairs + p))
        return pl.BlockSpec((1, n, LANE), lambda i, p: (i, 0, 3 * pairs + blk))

    def cols(n):
        return [col(0, n, True), col(1, n, True), col(2, n, True), col(0, n, False), col(1, n, False),
                col(2, n, False)]

    in_specs = cols(lc) + cols(ll) + [
        pl.BlockSpec((SUBLANE, LANE), lambda i, p: (0, p)),
        pl.BlockSpec((SUBLANE, LANE), lambda i, p: (0, 0)),
        pl.BlockSpec((SUBLANE, LANE), lambda i, p: (0, p)),
        pl.BlockSpec((2, LANE, LANE), lambda i, p: (0, 0, p)),
        pl.BlockSpec((2, LANE, LANE), lambda i, p: (0, 0, p)),
        pl.BlockSpec((LANE, LANE), lambda i, p: (0, p))]
    seq = pltpu.VMEM((lt, LANE), F32)
    seq2 = pltpu.VMEM((2, lt, LANE), F32)
    lat = pltpu.VMEM((ll, LANE), F32)
    nc = lt // CHUNK
    group = max(g for g in (12, 6, 4, 3, 2, 1) if nc % g == 0)
    n2 = 2 * CHUNK
    return pl.pallas_call(
        functools.partial(_rwkv_kernel, lc=lc, ll=ll),
        grid=(b, pairs),
        in_specs=in_specs,
        out_specs=pl.BlockSpec((1, ll, LANE), lambda i, p: (i, 0, p)),
        out_shape=jax.ShapeDtypeStruct((b, ll, C_B), F32),
        scratch_shapes=[seq, seq, seq, seq2, seq2, seq2, lat, lat, seq,
                        pltpu.VMEM((6, max(lc, ll) + 2 * SUBLANE, LANE), F32),
                        pltpu.VMEM((2, LANE, LANE), F32),
                        pltpu.VMEM((group, 2, 2 * n2, LANE), BF16), pltpu.VMEM((group, 2, n2, n2), BF16),
                        pltpu.VMEM((group, 2, LANE, n2), BF16), pltpu.VMEM((group, 2, 4, n2, LANE), F32),
                        pltpu.VMEM((2, 2 * n2, LANE), F32)],
        compiler_params=_cparams("parallel", "parallel"),
        name="rwkv_branch",
    )(*([pb_ctx] * 6), *([pb_lat] * 6), pch, plo, w0a0, w2pad, a2pad, g2)


def _merge_kernel(x_ref, pg_ref, oa_ref, ob_ref, m2_ref, woa_ref, wob_ref, wout_ref, o_ref):
    d = x_ref.shape[1]
    ya = _dot(oa_ref[...].astype(BF16), woa_ref[...])
    yb = _dot(ob_ref[...].astype(BF16), wob_ref[...])
    y = _sigmoid(pg_ref[:, 0:d]) * ya + _sigmoid(pg_ref[:, d:2 * d]) * yb
    o_ref[...] = x_ref[...] + m2_ref[0] * _dot(y.astype(BF16), wout_ref[...])


def merge_residual(x2d, pg, oa, ob, m2, w_o_a, w_o_b, w_out, rows_per_mod, tm=512):
    r, d = x2d.shape

    def rows(n):
        return pl.BlockSpec((tm, n), lambda i: (i, 0))

    def full(w):
        return pl.BlockSpec(w.shape, lambda i: (0, 0))

    tiles_per_mod = rows_per_mod // tm
    return pl.pallas_call(
        _merge_kernel,
        grid=(r // tm,),
        in_specs=[rows(d), rows(2 * d), rows(oa.shape[1]), rows(ob.shape[1]),
                  pl.BlockSpec((1, 1, d), lambda i: (i // tiles_per_mod, 0, 0)),
                  full(w_o_a), full(w_o_b), full(w_out)],
        out_specs=rows(d),
        out_shape=jax.ShapeDtypeStruct((r, d), F32),
        compiler_params=_cparams("parallel"),
        name="merge_residual",
    )(x2d, pg, oa, ob, m2, w_o_a, w_o_b, w_out)


ROUTER_GROUP_LANE0 = N_EXPERTS


def _route_kernel(h_ref, g_ref, sh_ref, sc_ref, wr_ref, br_ref, t_ref, cw_ref):
    t = _rms_modulate(h_ref[...], g_ref[...], sh_ref[0], sc_ref[0])
    t_ref[...] = t.astype(BF16)
    t_hi, t_lo, _ = _split3(t)
    w_hi, w_lo, _ = _split3(wr_ref[...])
    lg = _dot(t_hi, w_hi) + (_dot(t_hi, w_lo) + _dot(t_lo, w_hi)) + br_ref[...]
    lane = _iota2(lg.shape, 1)
    lane_f = lane.astype(F32)
    neg = jnp.float32(-jnp.inf)
    big = jnp.float32(2 * LANE)
    is_grp = (lane >= ROUTER_GROUP_LANE0) & (lane < ROUTER_GROUP_LANE0 + N_GROUPS)
    lgg = jnp.where(is_grp, lg, neg)
    mg = jnp.max(lgg, axis=-1, keepdims=True)
    p_grp = 1.0 / jnp.sum(jnp.where(is_grp, jnp.exp(lgg - mg), 0.0), axis=-1, keepdims=True)
    g_sel = jnp.min(jnp.where(lgg == mg, lane_f, big), axis=-1, keepdims=True) - ROUTER_GROUP_LANE0
    grp_of_lane = lax.shift_right_logical(lane, EXPERTS_PER_GROUP.bit_length() - 1).astype(F32)
    in_grp = (lane < N_EXPERTS) & (grp_of_lane == g_sel)
    l1 = jnp.where(in_grp, lg, neg)
    top1 = jnp.max(l1, axis=-1, keepdims=True)
    idx1 = jnp.min(jnp.where(l1 == top1, lane_f, big), axis=-1, keepdims=True)
    l2 = jnp.where(in_grp & (lane_f != idx1), lg, neg)
    top2 = jnp.max(l2, axis=-1, keepdims=True)
    idx2 = jnp.min(jnp.where(l2 == top2, lane_f, big), axis=-1, keepdims=True)
    e2 = jnp.exp(top2 - top1)
    w1 = p_grp / (1.0 + e2)
    cw_ref[...] = jnp.where(lane_f == idx1, w1, jnp.where(lane_f == idx2, w1 * e2, 0.0))


def route(h2d, g, shift, scale, w_router, b_router, rows_per_mod, tm=512):
    r, d = h2d.shape
    tiles_per_mod = rows_per_mod // tm
    mod_spec = pl.BlockSpec((1, 1, d), lambda i: (i // tiles_per_mod, 0, 0))
    return pl.pallas_call(
        _route_kernel,
        grid=(r // tm,),
        in_specs=[pl.BlockSpec((tm, d), lambda i: (i, 0)), pl.BlockSpec((1, d), lambda i: (0, 0)),
                  mod_spec, mod_spec,
                  pl.BlockSpec((d, LANE), lambda i: (0, 0)), pl.BlockSpec((1, LANE), lambda i: (0, 0))],
        out_specs=[pl.BlockSpec((tm, d), lambda i: (i, 0)), pl.BlockSpec((tm, LANE), lambda i: (i, 0))],
        out_shape=[jax.ShapeDtypeStruct((r, d), BF16), jax.ShapeDtypeStruct((r, LANE), F32)],
        compiler_params=_cparams("parallel"),
        name="moe_route",
    )(h2d, g.reshape(1, d), shift, scale, w_router, b_router)


EXPERTS_PER_STEP = 4


def _experts_kernel(t_ref, cw_ref, h_ref, m5_ref, fg_ref, wg_ref, wu_ref, wd_ref, o_ref, acc_ref):
    s = pl.program_id(1)

    @pl.when(s == 0)
    def _():
        acc_ref[...] = jnp.zeros(acc_ref.shape, F32)

    t = t_ref[...]
    cw = cw_ref[...]
    lane = _iota2(cw.shape, 1)
    hids = []
    for j in range(EXPERTS_PER_STEP):
        w = jnp.sum(jnp.where(lane == s * EXPERTS_PER_STEP + j, cw, 0.0), axis=-1, keepdims=True)
        hids.append((_silu(_dot(t, wg_ref[j])) * _dot(t, wu_ref[j]) * w).astype(BF16))
    acc_ref[...] += _dot(jnp.concatenate(hids, axis=1), wd_ref[...])

    @pl.when(s == pl.num_programs(1) - 1)
    def _():
        h2 = h_ref[...] + m5_ref[0] * acc_ref[...]
        ms = jnp.mean(h2 * h2, axis=-1, keepdims=True)
        o_ref[...] = h2 * lax.rsqrt(ms + NORM_EPS) * fg_ref[...]


def experts_residual_norm(t, cw, h2d, m5, final_g, w_gate, w_up, w_down, rows_per_mod, tm=1024):
    r, d = h2d.shape
    ne, _, f = w_gate.shape
    eps = EXPERTS_PER_STEP
    tiles_per_mod = rows_per_mod // tm
    return pl.pallas_call(
        _experts_kernel,
        grid=(r // tm, ne // eps),
        in_specs=[pl.BlockSpec((tm, d), lambda i, e: (i, 0)), pl.BlockSpec((tm, LANE), lambda i, e: (i, 0)),
                  pl.BlockSpec((tm, d), lambda i, e: (i, 0)),
                  pl.BlockSpec((1, 1, d), lambda i, e: (i // tiles_per_mod, 0, 0)),
                  pl.BlockSpec((1, d), lambda i, e: (0, 0)),
                  pl.BlockSpec((eps, d, f), lambda i, e: (e, 0, 0)), pl.BlockSpec((eps, d, f), lambda i, e: (e, 0, 0)),
                  pl.BlockSpec((eps * f, d), lambda i, e: (e, 0))],
        out_specs=pl.BlockSpec((tm, d), lambda i, e: (i, 0)),
        out_shape=jax.ShapeDtypeStruct((r, d), F32),
        scratch_shapes=[pltpu.VMEM((tm, d), F32)],
        compiler_params=_cparams("parallel", "arbitrary"),
        name="moe_experts",
    )(t, cw, h2d, m5, final_g.reshape(1, d), w_gate, w_up, w_down)


def _to_col_major(x, rows):
    b, l, c = x.shape
    return x.reshape(b, rows, GRID_W, c).transpose(0, 2, 1, 3).reshape(b, l, c)


def _to_row_major(x, rows):
    b, l, c = x.shape
    return x.reshape(b, GRID_W, rows, c).transpose(0, 2, 1, 3).reshape(b, l, c)


def _pad_rows(a, n):
    return jnp.pad(a, ((0, n - a.shape[0]),) + ((0, 0),) * (a.ndim - 1))


def _gdn_gates(pa, a_log, dt_bias):
    b, l, _ = pa.shape
    ab = pa[..., 4 * C_A:4 * C_A + 4 * H_A]
    a = ab[..., :2 * H_A].reshape(b, l, 2, H_A)
    bt = ab[..., 2 * H_A:].reshape(b, l, 2, H_A)
    g = -jnp.exp(a_log) * jax.nn.softplus(a + dt_bias)
    beta = jax.nn.sigmoid(bt)
    return jnp.concatenate([g, beta], axis=2).transpose(0, 3, 1, 2)


def kernel(x, c, ctx, c_ctx, ada_w, ada_b, norm_mix_g, norm_ffn_g, w_in, gdn_conv, gdn_a_log, gdn_dt_bias,
           gdn_onorm_g, rwkv_mu, rwkv_w0, rwkv_w2, rwkv_a0, rwkv_a2, rwkv_g2, rwkv_k_k, rwkv_k_a, rwkv_r_k,
           rwkv_lnx_g, rwkv_lnx_b, w_o_a, w_o_b, w_out, router_grp, router_grp_b, router_exp, router_exp_b,
           moe_w_gate, moe_w_up, moe_w_down, final_norm_g):
    bsz, seq, d = x.shape
    lc = ctx.shape[1]
    rows = seq // GRID_W
    a_cols = 4 * C_A + 4 * H_A
    b_cols = 3 * C_B + 2 * LORA_W + 2 * LORA_A + LORA_G

    cc = _pad_rows(jnp.concatenate([c, c_ctx[None]], axis=0), 2 * SUBLANE)
    mod = ada_modulation(cc, ada_w[0], ada_b[0])
    m_lat = [mod[:bsz, i * d:(i + 1) * d].reshape(bsz, 1, d) for i in range(6)]
    m_ctx = [mod[bsz:bsz + 1, i * d:(i + 1) * d].reshape(1, 1, d) for i in range(2)]

    w = w_in[0]
    w_a = jnp.pad(w[:, :a_cols], ((0, 0), (0, 4 * C_A + LANE - a_cols))).astype(BF16)
    w_b = w[:, a_cols:a_cols + b_cols].astype(BF16)
    w_g = w[:, a_cols + b_cols:].astype(BF16)

    x2d = x.reshape(bsz * seq, d)
    pa_lat, pg_lat = norm_modulate_project(x2d, norm_mix_g[0], m_lat[0], m_lat[1], [w_a, w_g], seq)
    (pb_lat,) = norm_modulate_project(_to_col_major(x, rows).reshape(bsz * seq, d), norm_mix_g[0],
                                      m_lat[0], m_lat[1], [w_b], seq)
    pa_ctx, pb_ctx = norm_modulate_project(ctx.reshape(bsz * lc, d), norm_mix_g[0], m_ctx[0], m_ctx[1],
                                           [w_a, w_b], bsz * lc)
    pa_lat = pa_lat.reshape(bsz, seq, -1)
    pb_lat = pb_lat.reshape(bsz, seq, -1)
    pa_ctx = pa_ctx.reshape(bsz, lc, -1)
    pb_ctx = pb_ctx.reshape(bsz, lc, -1)

    gates = jnp.concatenate([_gdn_gates(pa_ctx, gdn_a_log[0], gdn_dt_bias[0]),
                             _gdn_gates(pa_lat, gdn_a_log[0], gdn_dt_bias[0])], axis=2)
    nc = (lc + seq) // CHUNK
    gcol = gates.reshape(bsz, H_A, nc, CHUNK, 4)
    grow = gcol.transpose(0, 1, 2, 4, 3)
    oa = gdn_branch(pa_ctx, pa_lat, _pad_rows(gdn_conv[0], SUBLANE), gcol, grow, gdn_onorm_g[0])

    mu = rwkv_mu[0]
    pch = jnp.stack([rwkv_k_k[0], rwkv_k_a[0], rwkv_r_k[0].reshape(C_B), rwkv_lnx_g[0], rwkv_lnx_b[0],
                     mu[:C_B], mu[C_B:2 * C_B], mu[2 * C_B:3 * C_B]])
    plo = _pad_rows(mu[3 * C_B:].reshape(3, LANE), SUBLANE)
    w0a0 = _pad_rows(jnp.concatenate([rwkv_w0[0], rwkv_a0[0]], axis=0), SUBLANE)
    zw = jnp.zeros((LORA_W, C_B), F32)
    w2pad = jnp.stack([jnp.concatenate([rwkv_w2[0, 0], zw]), jnp.concatenate([zw, rwkv_w2[0, 1]])])
    a2pad = jnp.stack([jnp.concatenate([rwkv_a2[0, 0], zw]), jnp.concatenate([zw, rwkv_a2[0, 1]])])
    ob = rwkv_branch(pb_ctx, pb_lat, pch, plo, w0a0, w2pad, a2pad, rwkv_g2[0])
    ob = _to_row_major(ob, rows)

    h1 = merge_residual(x2d, pg_lat, oa.reshape(bsz * seq, C_A), ob.reshape(bsz * seq, C_B), m_lat[2],
                        w_o_a[0].astype(BF16), w_o_b[0].astype(BF16), w_out[0].astype(BF16), seq)

    w_router = jnp.pad(jnp.concatenate([router_exp[0], router_grp[0]], axis=1),
                       ((0, 0), (0, LANE - N_EXPERTS - N_GROUPS)))
    b_router = jnp.pad(jnp.concatenate([router_exp_b[0], router_grp_b[0]]),
                       (0, LANE - N_EXPERTS - N_GROUPS)).reshape(1, LANE)
    t, cw = route(h1, norm_ffn_g[0], m_lat[3], m_lat[4], w_router, b_router, seq)
    out = experts_residual_norm(t, cw, h1, m_lat[5], final_norm_g,
                                moe_w_gate[0].reshape(N_EXPERTS, d, D_EXPERT).astype(BF16),
                                moe_w_up[0].reshape(N_EXPERTS, d, D_EXPERT).astype(BF16),
                                moe_w_down[0].reshape(N_EXPERTS * D_EXPERT, d).astype(BF16), seq)
    return out.reshape(bsz, seq, d)
```

```python
import functools

import jax
import jax.numpy as jnp
from jax import lax
from jax.experimental import pallas as pl
from jax.experimental.pallas import tpu as pltpu

F32 = jnp.float32
BF16 = jnp.bfloat16
HIGHEST = lax.Precision.HIGHEST

GRID_W = 64
H_A = 4
DK_A = 128
C_A = H_A * DK_A
SHORT_CONV = 5
CHUNK = 64
H_B = 8
N_B = 64
C_B = H_B * N_B
LORA_W = 64
LORA_A = 64
LORA_G = 128
N_GROUPS = 4
EXPERTS_PER_GROUP = 8
N_EXPERTS = N_GROUPS * EXPERTS_PER_GROUP
D_EXPERT = 256
NORM_EPS = 1e-6
LNX_EPS = 1e-5 * N_B

LANE = 128
SUBLANE = 8
VMEM_LIMIT = 56 * 1024 * 1024


def _cparams(*sem):
    return pltpu.CompilerParams(dimension_semantics=sem, vmem_limit_bytes=VMEM_LIMIT)


SINGLE = "bf16 operands, one MXU pass, f32 accumulation"
MM = SINGLE


def _operands(a, b, precision):
    if precision is SINGLE:
        return a.astype(BF16), b.astype(BF16), None
    return a, b, precision


def _dot(a, b, precision=None):
    a, b, precision = _operands(a, b, precision)
    return jnp.dot(a, b, preferred_element_type=F32, precision=precision)


def _dot_nt(a, b, precision=None):
    a, b, precision = _operands(a, b, precision)
    return lax.dot_general(a, b, (((1,), (1,)), ((), ())), preferred_element_type=F32, precision=precision)


def _dot_tn(a, b, precision=None):
    a, b, precision = _operands(a, b, precision)
    return lax.dot_general(a, b, (((0,), (0,)), ((), ())), preferred_element_type=F32, precision=precision)


def _split3(x):
    hi = x.astype(BF16)
    r1 = x - hi.astype(F32)
    mid = r1.astype(BF16)
    lo = (r1 - mid.astype(F32)).astype(BF16)
    return hi, mid, lo


def _dot_small_int_lhs(m, x):
    mb = m.astype(BF16)
    hi, mid, lo = _split3(x)
    return _dot(mb, hi) + _dot(mb, mid) + _dot(mb, lo)


def _dot_small_int_rhs(x, m):
    mb = m.astype(BF16)
    hi, mid, lo = _split3(x)
    return _dot(hi, mb) + _dot(mid, mb) + _dot(lo, mb)


def _sigmoid(x):
    return jax.nn.sigmoid(x)


def _silu(x):
    return x * jax.nn.sigmoid(x)


def _softplus(x):
    return jnp.maximum(x, 0.0) + jnp.log1p(jnp.exp(-jnp.abs(x)))


def _ada_kernel(c_ref, w_ref, b_ref, o_ref):
    s_hi, s_lo, _ = _split3(_silu(c_ref[...]))
    w_hi, w_lo, _ = _split3(w_ref[...])
    o_ref[...] = _dot(s_hi, w_hi) + (_dot(s_hi, w_lo) + _dot(s_lo, w_hi)) + b_ref[...]


def ada_modulation(cc, ada_w, ada_b):
    r, d = cc.shape
    n = ada_w.shape[1]
    tn = 1536
    return pl.pallas_call(
        _ada_kernel,
        grid=(n // tn,),
        in_specs=[pl.BlockSpec((r, d), lambda j: (0, 0)),
                  pl.BlockSpec((d, tn), lambda j: (0, j)),
                  pl.BlockSpec((1, tn), lambda j: (0, j))],
        out_specs=pl.BlockSpec((r, tn), lambda j: (0, j)),
        out_shape=jax.ShapeDtypeStruct((r, n), F32),
        compiler_params=_cparams("arbitrary"),
        name="ada_modulation",
    )(cc, ada_w, ada_b.reshape(1, n))


def _rms_modulate(x, g, shift, scale):
    ms = jnp.mean(x * x, axis=-1, keepdims=True)
    y = x * lax.rsqrt(ms + NORM_EPS) * g
    return y * (1.0 + scale) + shift


def _inproj_kernel(x_ref, g_ref, sh_ref, sc_ref, *rest, n_out):
    w_refs, o_refs = rest[:n_out], rest[n_out:]
    u = _rms_modulate(x_ref[...], g_ref[...], sh_ref[0], sc_ref[0]).astype(BF16)
    for w_ref, o_ref in zip(w_refs, o_refs):
        o_ref[...] = _dot(u, w_ref[...])


def norm_modulate_project(x2d, g, shift, scale, weights, rows_per_mod, tm=512):
    r, d = x2d.shape
    tm = min(tm, rows_per_mod)
    tiles_per_mod = rows_per_mod // tm
    mod_spec = pl.BlockSpec((1, 1, d), lambda i: (i // tiles_per_mod, 0, 0))
    in_specs = [pl.BlockSpec((tm, d), lambda i: (i, 0)),
                pl.BlockSpec((1, d), lambda i: (0, 0)), mod_spec, mod_spec]
    in_specs += [pl.BlockSpec(w.shape, lambda i: (0, 0)) for w in weights]
    return pl.pallas_call(
        functools.partial(_inproj_kernel, n_out=len(weights)),
        grid=(r // tm,),
        in_specs=in_specs,
        out_specs=[pl.BlockSpec((tm, w.shape[1]), lambda i: (i, 0)) for w in weights],
        out_shape=[jax.ShapeDtypeStruct((r, w.shape[1]), F32) for w in weights],
        compiler_params=_cparams("parallel"),
        name="norm_modulate_project",
    )(x2d, g.reshape(1, d), shift, scale, *weights)


def _iota2(shape, dim):
    return lax.broadcasted_iota(jnp.int32, shape, dim)


def _unit_tri_inverse(lm, nil):
    n = lm.shape[0]
    eye = jnp.where(_iota2((n, n), 0) == _iota2((n, n), 1), 1.0, 0.0).astype(F32)
    x = eye - lm
    p = _dot(lm, lm, MM)
    yield
    k = 2
    while 2 * k < nil:
        xp = _dot(jnp.concatenate([x, p], axis=0), p, MM)
        yield
        x = x + xp[:n]
        p = xp[n:]
        k *= 2
    res = x + _dot(x, p, MM)
    yield
    return res


def _interleave(chains):
    results = [None] * len(chains)
    live = list(range(len(chains)))
    while live:
        for i in list(live):
            try:
                next(chains[i])
            except StopIteration as done:
                results[i] = done.value
                live.remove(i)
    return results


def _fill_padded(pad_ref, x, n):
    zeros = jnp.zeros((SUBLANE, LANE), F32)
    pad_ref[0:SUBLANE, :] = zeros
    pad_ref[SUBLANE:SUBLANE + n, :] = x
    pad_ref[SUBLANE + n:2 * SUBLANE + n, :] = zeros


def _pair_masks(c):
    n = 2 * c
    brow, bcol = _iota2((n, n), 0), _iota2((n, n), 1)
    same = jnp.logical_not(jnp.logical_xor(brow >= c, bcol >= c))
    tr, tc = jnp.where(brow >= c, brow - c, brow), jnp.where(bcol >= c, bcol - c, bcol)
    fwd = brow < c
    incl = same & ((fwd & (tr >= tc)) | (jnp.logical_not(fwd) & (tr <= tc)))
    strict = same & ((fwd & (tr > tc)) | (jnp.logical_not(fwd) & (tr < tc)))
    return incl, strict


def _gdn_chunk_local(q, k, v, g4, r4):
    c = q.shape[0]
    n = 2 * c
    row, col = _iota2((c, c), 0), _iota2((c, c), 1)
    gc2 = jnp.concatenate(
        [jnp.sum(jnp.where(row >= col, r4[0:1, :], 0.0), axis=1, keepdims=True),
         jnp.sum(jnp.where(row <= col, r4[1:2, :], 0.0), axis=1, keepdims=True)], axis=0)
    tot_f = jnp.sum(r4[0:1, :], axis=1, keepdims=True)
    tot_b = jnp.sum(r4[1:2, :], axis=1, keepdims=True)
    tot2 = jnp.concatenate([jnp.broadcast_to(tot_f, (c, 1)), jnp.broadcast_to(tot_b, (c, 1))], axis=0)
    ri, cj = _iota2((c, n), 0), _iota2((c, n), 1)
    lane_f = cj < c
    cjm = jnp.where(lane_f, cj, cj - c)
    keep = (lane_f & (ri <= cjm)) | (jnp.logical_not(lane_f) & (ri >= cjm))
    gc_row2 = jnp.sum(jnp.where(keep, jnp.where(lane_f, g4[:, 0:1], g4[:, 1:2]), 0.0), axis=0, keepdims=True)
    incl, strict = _pair_masks(c)
    decay = jnp.where(incl, jnp.exp(jnp.where(incl, gc2 - gc_row2, 0.0)), 0.0)
    beta2 = jnp.concatenate([g4[:, 2:3], g4[:, 3:4]], axis=0)
    kk2 = jnp.concatenate([k, k], axis=0)
    qq2 = jnp.concatenate([q, q], axis=0)
    kb2 = kk2 * beta2
    m = _dot_nt(jnp.concatenate([kb2, qq2], axis=0), kk2, MM)
    yield
    t = yield from _unit_tri_inverse(jnp.where(strict, m[:n] * decay, 0.0), c)
    egc2 = jnp.exp(gc2)
    sol = _dot(t, jnp.concatenate([jnp.concatenate([v, v], axis=0) * beta2, kb2 * egc2], axis=1), MM)
    yield
    qk2 = jnp.where(incl, m[n:] * decay, 0.0)
    kdt2 = (kk2 * jnp.exp(tot2 - gc2)).T
    fwd_rows = _iota2(sol.shape, 0) < c
    ks_f = _dot(kdt2, jnp.where(fwd_rows, sol, 0.0), MM)
    ks_b = _dot(kdt2, jnp.where(fwd_rows, 0.0, sol), MM)
    yield
    return sol[:, :LANE], sol[:, LANE:], qq2 * egc2, qk2, ks_f, ks_b, jnp.exp(tot_f), jnp.exp(tot_b)


def _gdn_kernel(qc_ref, kc_ref, vc_ref, ql_ref, kl_ref, vl_ref, zl_ref, cwq_ref, cwk_ref, cwv_ref,
                gcol_ref, grow_ref, og_ref, o_ref, q_s, k_s, v_s, oacc_s, pad_s, st_s,
                u_s, wq_s, k0_s, k1_s, qk_s, ge_s, p_s, *, lc, ll):
    tile = min(256, lc, ll)

    def prep(x_ref, cw_ref, dst, off, n, mode):
        _fill_padded(pad_s, x_ref[0], n)
        cw = cw_ref[...]
        for t0 in range(0, n, tile):
            acc = cw[0:1, :] * pad_s[pl.ds(SUBLANE - 2 + t0, tile), :]
            for j in range(1, SHORT_CONV):
                acc = acc + cw[j:j + 1, :] * pad_s[pl.ds(SUBLANE - 2 + j + t0, tile), :]
            y = _silu(acc)
            if mode != "v":
                y = y * lax.rsqrt(jnp.sum(y * y, axis=-1, keepdims=True) + NORM_EPS)
            if mode == "q":
                y = y * (DK_A ** -0.5)
            dst[pl.ds(off + t0, tile), :] = y

    prep(qc_ref, cwq_ref, q_s, 0, lc, "q")
    prep(kc_ref, cwk_ref, k_s, 0, lc, "k")
    prep(vc_ref, cwv_ref, v_s, 0, lc, "v")
    prep(ql_ref, cwq_ref, q_s, lc, ll, "q")
    prep(kl_ref, cwk_ref, k_s, lc, ll, "k")
    prep(vl_ref, cwv_ref, v_s, lc, ll, "v")

    oacc_s[...] = jnp.zeros(oacc_s.shape, F32)
    st_s[...] = jnp.zeros(st_s.shape, F32)
    nc_ctx = lc // CHUNK
    nc_tot = (lc + ll) // CHUNK

    c = CHUNK

    width = max(w for w in (18, 12, 6, 4, 3, 2, 1) if nc_tot % w == 0)

    def local(it, carry):
        cis = [it * width + j for j in range(width)]
        chains = []
        for ci in cis:
            rows = pl.ds(pl.multiple_of(ci * c, c), c)
            chains.append(_gdn_chunk_local(q_s[rows, :], k_s[rows, :], v_s[rows, :],
                                           gcol_ref[0, 0, ci], grow_ref[0, 0, ci]))
        for ci, (u2, w2, qd2, qk2, ks_f, ks_b, ge_f, ge_b) in zip(cis, _interleave(chains)):
            u_s[ci] = u2
            wq_s[ci, 0] = jnp.concatenate([w2[:c], qd2[:c]], axis=0).astype(BF16)
            wq_s[ci, 1] = jnp.concatenate([w2[c:], qd2[c:]], axis=0).astype(BF16)
            for d, ks in enumerate((ks_f, ks_b)):
                k0_s[ci, d] = ks[:, :LANE]
                k1_s[ci, d] = ks[:, LANE:].astype(BF16)
            qk_s[ci] = qk2.astype(BF16)
            ge_s[ci, 0:1, :] = jnp.broadcast_to(ge_f, (1, LANE))
            ge_s[ci, 1:2, :] = jnp.broadcast_to(ge_b, (1, LANE))
        return carry

    lax.fori_loop(0, nc_tot // width, local, 0)

    def bwd_chunk(s):
        return jnp.where(s < nc_ctx, nc_ctx - 1 - s, nc_tot - 1 - (s - nc_ctx))

    def outputs(s, keep):
        cb = bwd_chunk(s)
        p_f, p_b = p_s[0], p_s[1]
        v_new = jnp.concatenate([u_s[s, 0:c, :] - p_f[:c], u_s[cb, c:2 * c, :] - p_b[:c]], axis=0)
        qk = jnp.concatenate([qk_s[s, 0:c, :], qk_s[cb, c:2 * c, :]], axis=0)
        o2 = jnp.concatenate([p_f[c:], p_b[c:]], axis=0) + _dot(qk, v_new.astype(BF16))
        if keep is not None:
            o2 = jnp.where(keep, o2, 0.0)
        rows_f = pl.ds(pl.multiple_of(s * c, c), c)
        rows_b = pl.ds(pl.multiple_of(cb * c, c), c)
        oacc_s[rows_f, :] = oacc_s[rows_f, :] + o2[:c]
        oacc_s[rows_b, :] = oacc_s[rows_b, :] + o2[c:]

    def step(s, carry):
        outputs(jnp.maximum(s - 1, 0), s > 0)
        cb = bwd_chunk(s)
        s_f, s_b = st_s[0], st_s[1]
        sb_f, sb_b = s_f.astype(BF16), s_b.astype(BF16)
        p_s[0] = _dot(wq_s[s, 0], sb_f)
        p_s[1] = _dot(wq_s[cb, 1], sb_b)
        st_s[0] = s_f * ge_s[s, 0:1, :] + k0_s[s, 0] - _dot(k1_s[s, 0], sb_f)
        st_s[1] = s_b * ge_s[cb, 1:2, :] + k0_s[cb, 1] - _dot(k1_s[cb, 1], sb_b)
        return carry

    p_s[...] = jnp.zeros(p_s.shape, F32)
    lax.fori_loop(0, nc_tot, step, 0, unroll=2)
    outputs(nc_tot - 1, None)

    og = og_ref[...]
    for t0 in range(0, ll, tile):
        o = oacc_s[pl.ds(lc + t0, tile), :]
        o = o * lax.rsqrt(jnp.mean(o * o, axis=-1, keepdims=True) + NORM_EPS) * og
        o_ref[0, pl.ds(t0, tile), :] = o * _silu(zl_ref[0, pl.ds(t0, tile), :])


def gdn_branch(pa_ctx, pa_lat, conv_w, gcol, grow, onorm_g):
    b, lc, _ = pa_ctx.shape
    ll = pa_lat.shape[1]
    nc = (lc + ll) // CHUNK

    def col(blk, n):
        return pl.BlockSpec((1, n, LANE), lambda i, h: (i, 0, blk * H_A + h))

    def cw(blk):
        return pl.BlockSpec((SUBLANE, LANE), lambda i, h: (0, blk * H_A + h))

    in_specs = [col(0, lc), col(1, lc), col(2, lc), col(0, ll), col(1, ll), col(2, ll), col(3, ll),
                cw(0), cw(1), cw(2),
                pl.BlockSpec((1, 1, nc, CHUNK, 4), lambda i, h: (i, h, 0, 0, 0)),
                pl.BlockSpec((1, 1, nc, 4, CHUNK), lambda i, h: (i, h, 0, 0, 0)),
                pl.BlockSpec((1, LANE), lambda i, h: (0, 0))]
    lt = lc + ll
    return pl.pallas_call(
        functools.partial(_gdn_kernel, lc=lc, ll=ll),
        grid=(b, H_A),
        in_specs=in_specs,
        out_specs=pl.BlockSpec((1, ll, LANE), lambda i, h: (i, 0, h)),
        out_shape=jax.ShapeDtypeStruct((b, ll, C_A), F32),
        scratch_shapes=[pltpu.VMEM((lt, LANE), F32)] * 4
        + [pltpu.VMEM((max(lc, ll) + 2 * SUBLANE, LANE), F32), pltpu.VMEM((2, DK_A, DK_A), F32),
           pltpu.VMEM((nc, 2 * CHUNK, LANE), F32), pltpu.VMEM((nc, 2, 2 * CHUNK, LANE), BF16),
           pltpu.VMEM((nc, 2, DK_A, DK_A), F32), pltpu.VMEM((nc, 2, DK_A, DK_A), BF16),
           pltpu.VMEM((nc, 2 * CHUNK, 2 * CHUNK), BF16),
           pltpu.VMEM((nc, SUBLANE, LANE), F32), pltpu.VMEM((2, 2 * CHUNK, LANE), F32)],
        compiler_params=_cparams("parallel", "parallel"),
        name="gdn_branch",
    )(pa_ctx, pa_ctx, pa_ctx, pa_lat, pa_lat, pa_lat, pa_lat, conv_w, conv_w, conv_w, gcol, grow,
      onorm_g.reshape(1, LANE))


def _stack_heads(x, head0):
    return jnp.concatenate([jnp.where(head0, x, 0.0), jnp.where(head0, 0.0, x)], axis=0)


def _rwkv_chunk_local(r, v, a, logw, kdir, b, lower):
    c = r.shape[0]
    n = 2 * c
    row, col = _iota2((c, c), 0), _iota2((c, c), 1)
    tri = jnp.where(row >= col if lower else row <= col, 1.0, 0.0).astype(F32)
    lcum = _dot_small_int_lhs(tri, logw)
    tot = jnp.sum(logw, axis=0, keepdims=True)
    e_in = jnp.exp(lcum)
    e_ex = jnp.exp(lcum - logw)
    e_neg = jnp.exp(-lcum)
    e_rem = jnp.exp(tot - lcum)
    head0 = _iota2((c, LANE), 1) < N_B
    a2 = _stack_heads(a * e_ex, head0)
    r2 = _stack_heads(r * e_in, head0)
    b2 = _stack_heads(b * e_neg, head0)
    k2 = _stack_heads(kdir * e_neg, head0)
    v2 = _stack_heads(v, head0)
    m = _dot_nt(jnp.concatenate([a2, r2], axis=0), jnp.concatenate([b2, k2], axis=0), MM)
    yield
    brow, bcol = _iota2((n, n), 0), _iota2((n, n), 1)
    same = jnp.logical_not(jnp.logical_xor(brow >= c, bcol >= c))
    tr, tc = jnp.where(brow >= c, brow - c, brow), jnp.where(bcol >= c, bcol - c, bcol)
    strict = same & ((tr > tc) if lower else (tr < tc))
    incl = same & ((tr >= tc) if lower else (tr <= tc))
    ak = jnp.where(strict, m[:n, n:], 0.0)
    rb = jnp.where(incl, m[n:, :n], 0.0)
    rk = jnp.where(incl, m[n:, n:], 0.0)
    akv_rkv = _dot(jnp.concatenate([ak, rk], axis=0), v2, MM)
    kv0 = _dot_tn(_stack_heads(kdir * e_rem, head0), v2, MM)
    t = yield from _unit_tri_inverse(jnp.where(strict, -m[:n, :n], 0.0), c)
    tw = _dot(t, jnp.concatenate([a2, akv_rkv[:n]], axis=1), MM)
    yield
    x = jnp.concatenate([tw[:, :LANE], r2], axis=0)
    eg = jnp.broadcast_to(jnp.exp(tot), (LANE, LANE)).T
    bu = _dot(_stack_heads(b * e_rem, head0).T, tw, MM)
    yield
    return x, rb, bu[:, :LANE], tw[:, LANE:], akv_rkv[n:], kv0 + bu[:, LANE:], eg


def _rwkv_kernel(*refs, lc, ll):
    ctx_refs, lat_refs = refs[0:6], refs[6:12]
    pch_ref, plo_ref, w0a0_ref, w2_ref, a2_ref, g2_ref, o_ref = refs[12:19]
    r_s, v_s, a_s, lw_s, kd_s, b_s, gate_s, bonus_s, y_s, pad_s, st_s, x_s, rb_s, m1_s, f_s, p_s = refs[19:]
    tile = min(256, lc, ll)
    pch = pch_ref[...]
    k_k, k_a, r_k, lnx_g, lnx_b = (pch[i:i + 1, :] for i in range(5))
    mus = [pch[5:6, :], pch[6:7, :], pch[7:8, :]] + [plo_ref[i:i + 1, :] for i in range(3)]
    w0a0 = w0a0_ref[...]
    rr, cc = _iota2((LANE, LANE), 0), _iota2((LANE, LANE), 1)
    seg = jnp.where(jnp.logical_xor(rr >= N_B, cc >= N_B), 0.0, 1.0).astype(F32)

    def prep(src_refs, off, n, is_lat):
        for j in range(6):
            _fill_padded(pad_s.at[j], src_refs[j][0], n)
        for t0 in range(0, n, tile):
            mixed = []
            for j in range(6):
                x = pad_s[j, pl.ds(SUBLANE + t0, tile), :]
                nb = pad_s[j, pl.ds(SUBLANE - 1 + t0, tile), :] + pad_s[j, pl.ds(SUBLANE + 1 + t0, tile), :]
                mixed.append(x + (0.5 * nb - x) * mus[j])
            r, k, v, wl, al, gl = mixed
            wl = jnp.tanh(wl)
            kk = k * k_k
            kk = kk * lax.rsqrt(_dot_small_int_rhs(kk * kk, seg) + NORM_EPS)
            ksum = jnp.zeros_like(k)
            for d in range(2):
                w_log = -_softplus(-(w0a0[d:d + 1, :] + _dot(wl, w2_ref[d], MM))) - 0.5
                iclr = _sigmoid(w0a0[2 + d:3 + d, :] + _dot(al, a2_ref[d], MM))
                kdir = k * (1.0 + (iclr - 1.0) * k_a)
                ksum = ksum + kdir
                lw_s[d, pl.ds(off + t0, tile), :] = -jnp.exp(w_log)
                kd_s[d, pl.ds(off + t0, tile), :] = kdir
                b_s[d, pl.ds(off + t0, tile), :] = kk * iclr
            r_s[pl.ds(off + t0, tile), :] = r
            v_s[pl.ds(off + t0, tile), :] = v
            a_s[pl.ds(off + t0, tile), :] = -kk
            if is_lat:
                gate_s[pl.ds(t0, tile), :] = _dot(_sigmoid(gl), g2_ref[...], MM)
                bonus_s[pl.ds(t0, tile), :] = _dot_small_int_rhs(r * ksum * r_k, seg) * v

    prep(ctx_refs, 0, lc, False)
    prep(lat_refs, lc, ll, True)

    y_s[...] = jnp.zeros(y_s.shape, F32)
    st_s[...] = jnp.zeros(st_s.shape, F32)
    p_s[...] = jnp.zeros(p_s.shape, F32)
    nc_ctx = lc // CHUNK
    nc_tot = (lc + ll) // CHUNK

    group = x_s.shape[0]
    n2 = 2 * CHUNK

    def chunk_of(s, d):
        cb = jnp.where(s < nc_ctx, nc_ctx - 1 - s, nc_tot - 1 - (s - nc_ctx))
        return pl.ds(pl.multiple_of((cb if d else s) * CHUNK, CHUNK), CHUNK)

    width = max(w for w in (12, 6, 3, 2, 1) if group % w == 0)

    def run_group(gi, carry):
        def local(it, c2):
            slots = [(it * width + j, d) for j in range(width) for d in range(2)]
            chains = []
            for i, d in slots:
                rows = chunk_of(gi * group + i, d)
                chains.append(_rwkv_chunk_local(r_s[rows, :], v_s[rows, :], a_s[rows, :], lw_s[d, rows, :],
                                                kd_s[d, rows, :], b_s[d, rows, :], lower=(d == 0)))
            for (i, d), (x, rb, m1, u0, rkv, hc, eg) in zip(slots, _interleave(chains)):
                x_s[i, d] = x.astype(BF16)
                rb_s[i, d] = rb.astype(BF16)
                m1_s[i, d] = m1.astype(BF16)
                f_s[i, d, 0] = u0
                f_s[i, d, 1] = rkv
                f_s[i, d, 2] = hc
                f_s[i, d, 3] = eg
            return c2

        lax.fori_loop(0, group // width, local, 0)

        def outputs(i, keep):
            for d in range(2):
                rows = chunk_of(gi * group + i, d)
                u = p_s[d, 0:n2, :] + f_s[i, d, 0]
                y2 = p_s[d, n2:2 * n2, :] + _dot(rb_s[i, d], u.astype(BF16)) + f_s[i, d, 1]
                y = y2[:CHUNK] + y2[CHUNK:]
                if keep is not None:
                    y = jnp.where(keep, y, 0.0)
                y_s[rows, :] = y_s[rows, :] + y

        def step(i, c2):
            outputs(jnp.maximum(i - 1, 0), i > 0)
            hs = [st_s[d] for d in range(2)]
            hbs = [h.astype(BF16) for h in hs]
            for d in range(2):
                p_s[d] = _dot(x_s[i, d], hbs[d])
            for d in range(2):
                st_s[d] = f_s[i, d, 3] * hs[d] + f_s[i, d, 2] + _dot(m1_s[i, d], hbs[d])
            return c2

        lax.fori_loop(0, group, step, 0, unroll=2)
        outputs(group - 1, None)
        return carry

    lax.fori_loop(0, nc_tot // group, run_group, 0)

    inv_n = 1.0 / N_B
    for t0 in range(0, ll, tile):
        yf = y_s[pl.ds(lc + t0, tile), :]
        cen = yf - _dot_small_int_rhs(yf, seg) * inv_n
        var = _dot_small_int_rhs(cen * cen, seg) * inv_n
        y = cen * lax.rsqrt(var + LNX_EPS) * lnx_g + lnx_b
        o_ref[0, pl.ds(t0, tile), :] = (y + bonus_s[pl.ds(t0, tile), :]) * gate_s[pl.ds(t0, tile), :]


def rwkv_branch(pb_ctx, pb_lat, pch, plo, w0a0, w2pad, a2pad, g2):
    b, lc, _ = pb_ctx.shape
    ll = pb_lat.shape[1]
    lt = lc + ll
    pairs = C_B // LANE

    def col(blk, n, per_pair):
        if per_pair:
            return pl.BlockSpec((1, n, LANE), lambda i, p: (i, 0, blk * pairs + p))
        return pl.BlockSpec((1, n, LANE), lambda i, p: (i, 0, 3 * pairs + blk))

    def cols(n):
        return [col(0, n, True), col(1, n, True), col(2, n, True), col(0, n, False), col(1, n, False),
                col(2, n, False)]

    in_specs = cols(lc) + cols(ll) + [
        pl.BlockSpec((SUBLANE, LANE), lambda i, p: (0, p)),
        pl.BlockSpec((SUBLANE, LANE), lambda i, p: (0, 0)),
        pl.BlockSpec((SUBLANE, LANE), lambda i, p: (0, p)),
        pl.BlockSpec((2, LANE, LANE), lambda i, p: (0, 0, p)),
        pl.BlockSpec((2, LANE, LANE), lambda i, p: (0, 0, p)),
        pl.BlockSpec((LANE, LANE), lambda i, p: (0, p))]
    seq = pltpu.VMEM((lt, LANE), F32)
    seq2 = pltpu.VMEM((2, lt, LANE), F32)
    lat = pltpu.VMEM((ll, LANE), F32)
    nc = lt // CHUNK
    group = max(g for g in (12, 6, 4, 3, 2, 1) if nc % g == 0)
    n2 = 2 * CHUNK
    return pl.pallas_call(
        functools.partial(_rwkv_kernel, lc=lc, ll=ll),
        grid=(b, pairs),
        in_specs=in_specs,
        out_specs=pl.BlockSpec((1, ll, LANE), lambda i, p: (i, 0, p)),
        out_shape=jax.ShapeDtypeStruct((b, ll, C_B), F32),
        scratch_shapes=[seq, seq, seq, seq2, seq2, seq2, lat, lat, seq,
                        pltpu.VMEM((6, max(lc, ll) + 2 * SUBLANE, LANE), F32),
                        pltpu.VMEM((2, LANE, LANE), F32),
                        pltpu.VMEM((group, 2, 2 * n2, LANE), BF16), pltpu.VMEM((group, 2, n2, n2), BF16),
                        pltpu.VMEM((group, 2, LANE, n2), BF16), pltpu.VMEM((group, 2, 4, n2, LANE), F32),
                        pltpu.VMEM((2, 2 * n2, LANE), F32)],
        compiler_params=_cparams("parallel", "parallel"),
        name="rwkv_branch",
    )(*([pb_ctx] * 6), *([pb_lat] * 6), pch, plo, w0a0, w2pad, a2pad, g2)


def _merge_kernel(x_ref, pg_ref, oa_ref, ob_ref, m2_ref, woa_ref, wob_ref, wout_ref, o_ref):
    d = x_ref.shape[1]
    ya = _dot(oa_ref[...].astype(BF16), woa_ref[...])
    yb = _dot(ob_ref[...].astype(BF16), wob_ref[...])
    y = _sigmoid(pg_ref[:, 0:d]) * ya + _sigmoid(pg_ref[:, d:2 * d]) * yb
    o_ref[...] = x_ref[...] + m2_ref[0] * _dot(y.astype(BF16), wout_ref[...])


def merge_residual(x2d, pg, oa, ob, m2, w_o_a, w_o_b, w_out, rows_per_mod, tm=512):
    r, d = x2d.shape

    def rows(n):
        return pl.BlockSpec((tm, n), lambda i: (i, 0))

    def full(w):
        return pl.BlockSpec(w.shape, lambda i: (0, 0))

    tiles_per_mod = rows_per_mod // tm
    return pl.pallas_call(
        _merge_kernel,
        grid=(r // tm,),
        in_specs=[rows(d), rows(2 * d), rows(oa.shape[1]), rows(ob.shape[1]),
                  pl.BlockSpec((1, 1, d), lambda i: (i // tiles_per_mod, 0, 0)),
                  full(w_o_a), full(w_o_b), full(w_out)],
        out_specs=rows(d),
        out_shape=jax.ShapeDtypeStruct((r, d), F32),
        compiler_params=_cparams("parallel"),
        name="merge_residual",
    )(x2d, pg, oa, ob, m2, w_o_a, w_o_b, w_out)


ROUTER_GROUP_LANE0 = N_EXPERTS


def _route_kernel(h_ref, g_ref, sh_ref, sc_ref, wr_ref, br_ref, t_ref, cw_ref):
    t = _rms_modulate(h_ref[...], g_ref[...], sh_ref[0], sc_ref[0])
    t_ref[...] = t.astype(BF16)
    t_hi, t_lo, _ = _split3(t)
    w_hi, w_lo, _ = _split3(wr_ref[...])
    lg = _dot(t_hi, w_hi) + (_dot(t_hi, w_lo) + _dot(t_lo, w_hi)) + br_ref[...]
    lane = _iota2(lg.shape, 1)
    lane_f = lane.astype(F32)
    neg = jnp.float32(-jnp.inf)
    big = jnp.float32(2 * LANE)
    is_grp = (lane >= ROUTER_GROUP_LANE0) & (lane < ROUTER_GROUP_LANE0 + N_GROUPS)
    lgg = jnp.where(is_grp, lg, neg)
    mg = jnp.max(lgg, axis=-1, keepdims=True)
    p_grp = 1.0 / jnp.sum(jnp.where(is_grp, jnp.exp(lgg - mg), 0.0), axis=-1, keepdims=True)
    g_sel = jnp.min(jnp.where(lgg == mg, lane_f, big), axis=-1, keepdims=True) - ROUTER_GROUP_LANE0
    grp_of_lane = lax.shift_right_logical(lane, EXPERTS_PER_GROUP.bit_length() - 1).astype(F32)
    in_grp = (lane < N_EXPERTS) & (grp_of_lane == g_sel)
    l1 = jnp.where(in_grp, lg, neg)
    top1 = jnp.max(l1, axis=-1, keepdims=True)
    idx1 = jnp.min(jnp.where(l1 == top1, lane_f, big), axis=-1, keepdims=True)
    l2 = jnp.where(in_grp & (lane_f != idx1), lg, neg)
    top2 = jnp.max(l2, axis=-1, keepdims=True)
    idx2 = jnp.min(jnp.where(l2 == top2, lane_f, big), axis=-1, keepdims=True)
    e2 = jnp.exp(top2 - top1)
    w1 = p_grp / (1.0 + e2)
    cw_ref[...] = jnp.where(lane_f == idx1, w1, jnp.where(lane_f == idx2, w1 * e2, 0.0))


def route(h2d, g, shift, scale, w_router, b_router, rows_per_mod, tm=512):
    r, d = h2d.shape
    tiles_per_mod = rows_per_mod // tm
    mod_spec = pl.BlockSpec((1, 1, d), lambda i: (i // tiles_per_mod, 0, 0))
    return pl.pallas_call(
        _route_kernel,
        grid=(r // tm,),
        in_specs=[pl.BlockSpec((tm, d), lambda i: (i, 0)), pl.BlockSpec((1, d), lambda i: (0, 0)),
                  mod_spec, mod_spec,
                  pl.BlockSpec((d, LANE), lambda i: (0, 0)), pl.BlockSpec((1, LANE), lambda i: (0, 0))],
        out_specs=[pl.BlockSpec((tm, d), lambda i: (i, 0)), pl.BlockSpec((tm, LANE), lambda i: (i, 0))],
        out_shape=[jax.ShapeDtypeStruct((r, d), BF16), jax.ShapeDtypeStruct((r, LANE), F32)],
        compiler_params=_cparams("parallel"),
        name="moe_route",
    )(h2d, g.reshape(1, d), shift, scale, w_router, b_router)


EXPERTS_PER_STEP = 8


def _experts_kernel(t_ref, cw_ref, h_ref, m5_ref, fg_ref, wg_ref, wu_ref, wd_ref, o_ref):
    acc_ref = o_ref
    s = pl.program_id(1)

    @pl.when(s == 0)
    def _():
        acc_ref[...] = jnp.zeros(acc_ref.shape, F32)

    t = t_ref[...]
    cw = cw_ref[...]
    lane = _iota2(cw.shape, 1)
    hids = []
    for j in range(EXPERTS_PER_STEP):
        w = jnp.sum(jnp.where(lane == s * EXPERTS_PER_STEP + j, cw, 0.0), axis=-1, keepdims=True)
        hids.append((_silu(_dot(t, wg_ref[j])) * _dot(t, wu_ref[j]) * w).astype(BF16))
    acc_ref[...] += _dot(jnp.concatenate(hids, axis=1), wd_ref[...])

    @pl.when(s == pl.num_programs(1) - 1)
    def _():
        h2 = h_ref[...] + m5_ref[0] * acc_ref[...]
        ms = jnp.mean(h2 * h2, axis=-1, keepdims=True)
        o_ref[...] = h2 * lax.rsqrt(ms + NORM_EPS) * fg_ref[...]


def experts_residual_norm(t, cw, h2d, m5, final_g, w_gate, w_up, w_down, rows_per_mod, tm=1024):
    r, d = h2d.shape
    ne, _, f = w_gate.shape
    eps = EXPERTS_PER_STEP
    tiles_per_mod = rows_per_mod // tm
    return pl.pallas_call(
        _experts_kernel,
        grid=(r // tm, ne // eps),
        in_specs=[pl.BlockSpec((tm, d), lambda i, e: (i, 0)), pl.BlockSpec((tm, LANE), lambda i, e: (i, 0)),
                  pl.BlockSpec((tm, d), lambda i, e: (i, 0)),
                  pl.BlockSpec((1, 1, d), lambda i, e: (i // tiles_per_mod, 0, 0)),
                  pl.BlockSpec((1, d), lambda i, e: (0, 0)),
                  pl.BlockSpec((eps, d, f), lambda i, e: (e, 0, 0)), pl.BlockSpec((eps, d, f), lambda i, e: (e, 0, 0)),
                  pl.BlockSpec((eps * f, d), lambda i, e: (e, 0))],
        out_specs=pl.BlockSpec((tm, d), lambda i, e: (i, 0)),
        out_shape=jax.ShapeDtypeStruct((r, d), F32),
        compiler_params=_cparams("parallel", "arbitrary"),
        name="moe_experts",
    )(t, cw, h2d, m5, final_g.reshape(1, d), w_gate, w_up, w_down)


def _to_col_major(x, rows):
    b, l, c = x.shape
    return x.reshape(b, rows, GRID_W, c).transpose(0, 2, 1, 3).reshape(b, l, c)


def _to_row_major(x, rows):
    b, l, c = x.shape
    return x.reshape(b, GRID_W, rows, c).transpose(0, 2, 1, 3).reshape(b, l, c)


def _pad_rows(a, n):
    return jnp.pad(a, ((0, n - a.shape[0]),) + ((0, 0),) * (a.ndim - 1))


def _gdn_gates(pa, a_log, dt_bias):
    b, l, _ = pa.shape
    ab = pa[..., 4 * C_A:4 * C_A + 4 * H_A]
    a = ab[..., :2 * H_A].reshape(b, l, 2, H_A)
    bt = ab[..., 2 * H_A:].reshape(b, l, 2, H_A)
    g = -jnp.exp(a_log) * jax.nn.softplus(a + dt_bias)
    beta = jax.nn.sigmoid(bt)
    return jnp.concatenate([g, beta], axis=2).transpose(0, 3, 1, 2)


def kernel(x, c, ctx, c_ctx, ada_w, ada_b, norm_mix_g, norm_ffn_g, w_in, gdn_conv, gdn_a_log, gdn_dt_bias,
           gdn_onorm_g, rwkv_mu, rwkv_w0, rwkv_w2, rwkv_a0, rwkv_a2, rwkv_g2, rwkv_k_k, rwkv_k_a, rwkv_r_k,
           rwkv_lnx_g, rwkv_lnx_b, w_o_a, w_o_b, w_out, router_grp, router_grp_b, router_exp, router_exp_b,
           moe_w_gate, moe_w_up, moe_w_down, final_norm_g):
    bsz, seq, d = x.shape
    lc = ctx.shape[1]
    rows = seq // GRID_W
    a_cols = 4 * C_A + 4 * H_A
    b_cols = 3 * C_B + 2 * LORA_W + 2 * LORA_A + LORA_G

    cc = _pad_rows(jnp.concatenate([c, c_ctx[None]], axis=0), 2 * SUBLANE)
    mod = ada_modulation(cc, ada_w[0], ada_b[0])
    m_lat = [mod[:bsz, i * d:(i + 1) * d].reshape(bsz, 1, d) for i in range(6)]
    m_ctx = [mod[bsz:bsz + 1, i * d:(i + 1) * d].reshape(1, 1, d) for i in range(2)]

    w = w_in[0]
    w_a = jnp.pad(w[:, :a_cols], ((0, 0), (0, 4 * C_A + LANE - a_cols))).astype(BF16)
    w_b = w[:, a_cols:a_cols + b_cols].astype(BF16)
    w_g = w[:, a_cols + b_cols:].astype(BF16)

    x2d = x.reshape(bsz * seq, d)
    pa_lat, pg_lat = norm_modulate_project(x2d, norm_mix_g[0], m_lat[0], m_lat[1], [w_a, w_g], seq)
    (pb_lat,) = norm_modulate_project(_to_col_major(x, rows).reshape(bsz * seq, d), norm_mix_g[0],
                                      m_lat[0], m_lat[1], [w_b], seq)
    pa_ctx, pb_ctx = norm_modulate_project(ctx.reshape(bsz * lc, d), norm_mix_g[0], m_ctx[0], m_ctx[1],
                                           [w_a, w_b], bsz * lc)
    pa_lat = pa_lat.reshape(bsz, seq, -1)
    pb_lat = pb_lat.reshape(bsz, seq, -1)
    pa_ctx = pa_ctx.reshape(bsz, lc, -1)
    pb_ctx = pb_ctx.reshape(bsz, lc, -1)

    gates = jnp.concatenate([_gdn_gates(pa_ctx, gdn_a_log[0], gdn_dt_bias[0]),
                             _gdn_gates(pa_lat, gdn_a_log[0], gdn_dt_bias[0])], axis=2)
    nc = (lc + seq) // CHUNK
    gcol = gates.reshape(bsz, H_A, nc, CHUNK, 4)
    grow = gcol.transpose(0, 1, 2, 4, 3)
    oa = gdn_branch(pa_ctx, pa_lat, _pad_rows(gdn_conv[0], SUBLANE), gcol, grow, gdn_onorm_g[0])

    mu = rwkv_mu[0]
    pch = jnp.stack([rwkv_k_k[0], rwkv_k_a[0], rwkv_r_k[0].reshape(C_B), rwkv_lnx_g[0], rwkv_lnx_b[0],
                     mu[:C_B], mu[C_B:2 * C_B], mu[2 * C_B:3 * C_B]])
    plo = _pad_rows(mu[3 * C_B:].reshape(3, LANE), SUBLANE)
    w0a0 = _pad_rows(jnp.concatenate([rwkv_w0[0], rwkv_a0[0]], axis=0), SUBLANE)
    zw = jnp.zeros((LORA_W, C_B), F32)
    w2pad = jnp.stack([jnp.concatenate([rwkv_w2[0, 0], zw]), jnp.concatenate([zw, rwkv_w2[0, 1]])])
    a2pad = jnp.stack([jnp.concatenate([rwkv_a2[0, 0], zw]), jnp.concatenate([zw, rwkv_a2[0, 1]])])
    ob = rwkv_branch(pb_ctx, pb_lat, pch, plo, w0a0, w2pad, a2pad, rwkv_g2[0])
    ob = _to_row_major(ob, rows)

    h1 = merge_residual(x2d, pg_lat, oa.reshape(bsz * seq, C_A), ob.reshape(bsz * seq, C_B), m_lat[2],
                        w_o_a[0].astype(BF16), w_o_b[0].astype(BF16), w_out[0].astype(BF16), seq)

    w_router = jnp.pad(jnp.concatenate([router_exp[0], router_grp[0]], axis=1),
                       ((0, 0), (0, LANE - N_EXPERTS - N_GROUPS)))
    b_router = jnp.pad(jnp.concatenate([router_exp_b[0], router_grp_b[0]]),
                       (0, LANE - N_EXPERTS - N_GROUPS)).reshape(1, LANE)
    t, cw = route(h1, norm_ffn_g[0], m_lat[3], m_lat[4], w_router, b_router, seq)
    out = experts_residual_norm(t, cw, h1, m_lat[5], final_norm_g,
                                moe_w_gate[0].reshape(N_EXPERTS, d, D_EXPERT).astype(BF16),
                                moe_w_up[0].reshape(N_EXPERTS, d, D_EXPERT).astype(BF16),
                                moe_w_down[0].reshape(N_EXPERTS * D_EXPERT, d).astype(BF16), seq)
    return out.reshape(bsz, seq, d)
```
